```python
import math
import jax, jax.numpy as jnp
from jax import lax
import numpy as np

D_MODEL = 2048
BATCH = 2
SEQ = 4096
DEPTH = 4

HEAD_DIM = 128
MIX_WIDTH = D_MODEL
A_HEADS = MIX_WIDTH // 4 // HEAD_DIM
B_HEADS = MIX_WIDTH // 2 // HEAD_DIM
B_KV_HEADS = B_HEADS // 4
C_HEADS = MIX_WIDTH // 4 // HEAD_DIM
A_W = A_HEADS * HEAD_DIM
B_QW = B_HEADS * HEAD_DIM
B_KVW = B_KV_HEADS * HEAD_DIM
C_W = C_HEADS * HEAD_DIM
DILATED_PATTERNS = ((128, 1), (512, 4), (2048, 16))
ROPE_THETA = 500000.0
ROPE_DIM = HEAD_DIM // 4
AXIAL_THETA = 10000.0
GRID_W = 64
Q_BLOCK = 128
CONV_K = 5
DELTA_CHUNK = 64
N_EXPERTS = 16
EC_CAPACITY = 2
EXPERT_FF = D_MODEL // 2
PLE_DIM = 256
NORM_EPS = 1e-6
IN_WIDTHS = (A_W, A_W, A_W, B_QW, B_KVW, B_KVW, 3 * C_W, C_W, 2 * C_HEADS, 2 * C_HEADS)
D_IN = 3 * A_W + B_QW + 2 * B_KVW + 4 * C_W + 4 * C_HEADS

kernel_name = "hybrid_parallel_heads_ec_moe_encoder"

F32 = jnp.float32


def rms_norm(x, gain):
    xf = x.astype(F32)
    y = xf * lax.rsqrt(jnp.mean(xf * xf, axis=-1, keepdims=True) + NORM_EPS)
    return (y * gain.astype(F32)).astype(x.dtype)


def rope_tables(pos, dim, theta):
    inv = theta ** (-jnp.arange(0, dim, 2, dtype=F32) / dim)
    ang = pos.astype(F32)[:, None] * inv[None, :]
    return jnp.cos(ang), jnp.sin(ang)


def rotate(x, cos, sin):
    half = x.shape[-1] // 2
    x1, x2 = x[..., :half], x[..., half:]
    cos, sin = cos.astype(x.dtype), sin.astype(x.dtype)
    return jnp.concatenate([x1 * cos - x2 * sin, x2 * cos + x1 * sin], axis=-1)


def partial_rope(x, cos, sin):
    return jnp.concatenate([rotate(x[..., :ROPE_DIM], cos, sin), x[..., ROPE_DIM:]], axis=-1)


def axial_rope(x, row_cs, col_cs):
    h = HEAD_DIM // 2
    return jnp.concatenate([rotate(x[..., :h], *row_cs), rotate(x[..., h:], *col_cs)], axis=-1)


def banded_attention(q, k, v, half):
    lead = q.shape[:-2]
    L, hd = q.shape[-2:]
    nl = len(lead)
    blk = half
    nb = -(-L // blk)
    lp = nb * blk
    qb = jnp.pad(q, [(0, 0)] * nl + [(0, lp - L), (0, 0)]).reshape(*lead, nb, blk, hd)

    def windows(t):
        tb = jnp.pad(t, [(0, 0)] * nl + [(blk, lp - L + blk), (0, 0)]).reshape(*lead, nb + 2, blk, hd)
        return jnp.concatenate([tb[..., :-2, :, :], tb[..., 1:-1, :, :], tb[..., 2:, :, :]], axis=-2)

    kw, vw = windows(k), windows(v)
    qpos = jnp.arange(lp).reshape(nb, blk)
    kpos = jnp.arange(nb)[:, None] * blk - blk + jnp.arange(3 * blk)[None, :]
    mask = ((jnp.abs(qpos[:, :, None] - kpos[:, None, :]) <= half)
            & (kpos >= 0)[:, None, :] & (kpos < L)[:, None, :])
    s = jnp.einsum('...nqd,...nkd->...nqk', qb.astype(F32), kw.astype(F32)) * (hd ** -0.5)
    s = jnp.where(mask, s, -jnp.inf)
    m = jnp.max(s, axis=-1, keepdims=True)
    e = jnp.exp(s - m)
    den = jnp.sum(e, axis=-1, keepdims=True)
    o = jnp.einsum('...nqk,...nkd->...nqd', e / den, vw.astype(F32))
    lse = (m + jnp.log(den))[..., 0]
    return o.reshape(*lead, lp, hd)[..., :L, :], lse.reshape(*lead, lp)[..., :L]


def dilated_mixer(q, k, v):
    B, H, S, hd = q.shape
    outs, lses = [], []
    for window, dil in DILATED_PATTERNS:
        half = window // (2 * dil)
        fold = lambda t: t.reshape(B, H, S // dil, dil, hd).swapaxes(2, 3)
        o, lse = banded_attention(fold(q), fold(k), fold(v), half)
        outs.append(o.swapaxes(2, 3).reshape(B, H, S, hd))
        lses.append(lse.swapaxes(2, 3).reshape(B, H, S))
    wts = jax.nn.softmax(jnp.stack(lses, axis=0), axis=0)
    return jnp.sum(wts[..., None] * jnp.stack(outs, axis=0), axis=0).astype(q.dtype)


def gqa_blocked(q, k, v):
    B, HQ, S, hd = q.shape
    hkv = k.shape[1]
    grp = HQ // hkv
    nblk = S // Q_BLOCK
    qb = q.reshape(B, hkv, grp, nblk, Q_BLOCK, hd).transpose(3, 0, 1, 2, 4, 5)
    kf, vf = k.astype(F32), v.astype(F32)
    scale = hd ** -0.5

    def attend(q_blk):
        s = jnp.einsum('bhgqd,bhkd->bhgqk', q_blk.astype(F32), kf) * scale
        return jnp.einsum('bhgqk,bhkd->bhgqd', jax.nn.softmax(s, axis=-1), vf)

    o = lax.map(attend, qb)
    return o.transpose(1, 2, 3, 0, 4, 5).reshape(B, HQ, S, hd).astype(q.dtype)


def centred_depthwise_conv(x, w):
    K = w.shape[0]
    pad = K // 2
    S = x.shape[1]
    xp = jnp.pad(x, ((0, 0), (pad, pad), (0, 0)))
    acc = xp[:, 0:S] * w[0]
    for j in range(1, K):
        acc = acc + xp[:, j:j + S] * w[j]
    return acc


def chunk_gated_delta_rule(q, k, v, g, beta):
    B, H, L, DK = q.shape
    DV = v.shape[-1]
    C = DELTA_CHUNK
    n = L // C
    q = q * (DK ** -0.5)
    kb = k * beta[..., None]
    vb = v * beta[..., None]
    chunks = lambda t: t.reshape(B, H, n, C, *t.shape[3:])
    q, k, kb, vb, g = chunks(q), chunks(k), chunks(kb), chunks(vb), chunks(g)
    gc = jnp.cumsum(g, axis=-1)
    incl = jnp.tril(jnp.ones((C, C), dtype=bool))
    strict = jnp.tril(jnp.ones((C, C), dtype=bool), -1)
    decay = jnp.exp(jnp.where(incl, gc[..., :, None] - gc[..., None, :], -jnp.inf))
    a = jnp.where(strict, jnp.einsum('bhnik,bhnjk->bhnij', kb, k) * decay, 0.0)
    rhs = jnp.concatenate([vb, kb * jnp.exp(gc)[..., None]], axis=-1)
    sol = lax.linalg.triangular_solve(a, rhs, left_side=True, lower=True, unit_diagonal=True)
    u, w = sol[..., :DV], sol[..., DV:]
    qk = jnp.einsum('bhnik,bhnjk->bhnij', q, k) * decay

    def step(state, xs):
        q_i, k_i, u_i, w_i, gc_i, qk_i = xs
        v_new = u_i - jnp.einsum('bhck,bhkv->bhcv', w_i, state)
        o_i = (jnp.einsum('bhck,bhkv->bhcv', q_i * jnp.exp(gc_i)[..., None], state)
               + jnp.einsum('bhij,bhjv->bhiv', qk_i, v_new))
        g_last = gc_i[..., -1]
        state = (state * jnp.exp(g_last)[..., None, None]
                 + jnp.einsum('bhck,bhcv->bhkv', k_i * jnp.exp(g_last[..., None] - gc_i)[..., None], v_new))
        return state, o_i

    xs = tuple(jnp.moveaxis(t, 2, 0) for t in (q, k, u, w, gc, qk))
    _, o = lax.scan(step, jnp.zeros((B, H, DK, DV), F32), xs)
    return jnp.moveaxis(o, 0, 2).reshape(B, H, L, DV)


def gated_deltanet_mixer(qkv, z, beta_logits, a_logits, conv_w, a_log, dt_bias, norm_g):
    B, S, _ = qkv.shape
    dtype = qkv.dtype
    qkv = jax.nn.silu(centred_depthwise_conv(qkv, conv_w)).astype(F32)
    q, k, v = jnp.split(qkv, 3, axis=-1)
    heads = lambda t: t.reshape(B, S, C_HEADS, HEAD_DIM).transpose(0, 2, 1, 3)
    l2n = lambda t: t * lax.rsqrt(jnp.sum(t * t, axis=-1, keepdims=True) + NORM_EPS)
    q, k, v = l2n(heads(q)), l2n(heads(k)), heads(v)
    dirs = lambda t: t.astype(F32).reshape(B, S, 2, C_HEADS).transpose(2, 0, 3, 1)
    beta = jax.nn.sigmoid(dirs(beta_logits))
    g = -jnp.exp(a_log.astype(F32))[:, None, :, None] * jax.nn.softplus(
        dirs(a_logits) + dt_bias.astype(F32)[:, None, :, None])
    o_fwd = chunk_gated_delta_rule(q, k, v, g[0], beta[0])
    flip = lambda t: jnp.flip(t, axis=2)
    o_bwd = flip(chunk_gated_delta_rule(flip(q), flip(k), flip(v), flip(g[1]), flip(beta[1])))
    o = (o_fwd + o_bwd).transpose(0, 2, 1, 3)
    o = rms_norm(o, norm_g) * jax.nn.silu(z.astype(F32).reshape(B, S, C_HEADS, HEAD_DIM))
    return o.reshape(B, S, C_W).astype(dtype)


def expert_choice_moe(h, w_router, w_gate, w_up, w_down):
    B, S, D = h.shape
    cap = EC_CAPACITY * S // N_EXPERTS
    aff = jax.nn.softmax(jnp.einsum('bsd,de->bse', h.astype(F32), w_router.astype(F32)), axis=-1)
    gates, idx = lax.top_k(jnp.swapaxes(aff, 1, 2), cap)
    xs = jnp.take_along_axis(h, idx.reshape(B, -1)[..., None], axis=1).reshape(B, N_EXPERTS, cap, D)
    hid = jax.nn.silu(jnp.einsum('becd,edf->becf', xs, w_gate)) * jnp.einsum('becd,edf->becf', xs, w_up)
    y = jnp.einsum('becf,efd->becd', hid, w_down) * gates[..., None].astype(h.dtype)
    b_ix = jnp.arange(B)[:, None, None]
    return jnp.zeros_like(h).at[b_ix, idx].add(y)


def setup_inputs(seed: int = 0) -> dict:
    key = jax.random.key(seed)
    ks = jax.random.split(key, 24)

    def dense(k, shape, fan_in):
        return jax.random.normal(k, shape, F32) * (fan_in ** -0.5)

    def gain(k, shape):
        return 1.0 + 0.02 * jax.random.normal(k, shape, F32)

    dt = jnp.exp(jax.random.uniform(ks[9], (DEPTH, 2, C_HEADS), F32,
                                    minval=math.log(1e-3), maxval=math.log(1e-1)))
    dt_bias = dt + jnp.log(-jnp.expm1(-dt))
    a_log = jnp.log(jax.random.uniform(ks[10], (DEPTH, 2, C_HEADS), F32, minval=1.0, maxval=16.0))
    return {
        "x": jax.random.normal(ks[0], (BATCH, SEQ, D_MODEL), F32),
        "p": jax.random.normal(ks[1], (DEPTH, BATCH, SEQ, PLE_DIM), F32),
        "norm_mix": gain(ks[2], (DEPTH, D_MODEL)),
        "w_in": dense(ks[3], (DEPTH, D_MODEL, D_IN), D_MODEL),
        "conv_w": dense(ks[4], (DEPTH, CONV_K, 3 * C_W), CONV_K),
        "q_norm": gain(ks[5], (DEPTH, HEAD_DIM)),
        "k_norm": gain(ks[6], (DEPTH, HEAD_DIM)),
        "out_norm_a": gain(ks[7], (DEPTH, A_W)),
        "out_norm_b": gain(ks[8], (DEPTH, B_QW)),
        "gdn_a_log": a_log,
        "gdn_dt_bias": dt_bias,
        "gdn_norm": gain(ks[11], (DEPTH, HEAD_DIM)),
        "w_out": dense(ks[12], (DEPTH, MIX_WIDTH, D_MODEL), MIX_WIDTH),
        "norm_moe": gain(ks[13], (DEPTH, D_MODEL)),
        "w_router": dense(ks[14], (DEPTH, D_MODEL, N_EXPERTS), D_MODEL),
        "w_gate": dense(ks[15], (DEPTH, N_EXPERTS, D_MODEL, EXPERT_FF), D_MODEL),
        "w_up": dense(ks[16], (DEPTH, N_EXPERTS, D_MODEL, EXPERT_FF), D_MODEL),
        "w_down": dense(ks[17], (DEPTH, N_EXPERTS, EXPERT_FF, D_MODEL), EXPERT_FF),
        "norm_ple": gain(ks[18], (DEPTH, D_MODEL)),
        "w_ple": dense(ks[19], (DEPTH, PLE_DIM, D_MODEL), PLE_DIM),
        "w_ple_gate": dense(ks[20], (DEPTH, D_MODEL, D_MODEL), D_MODEL),
        "norm_final": gain(ks[21], (D_MODEL,)),
    }


def reference(x, p, norm_mix, w_in, conv_w, q_norm, k_norm, out_norm_a, out_norm_b,
              gdn_a_log, gdn_dt_bias, gdn_norm, w_out, norm_moe, w_router, w_gate, w_up,
              w_down, norm_ple, w_ple, w_ple_gate, norm_final):
    B, S, D = x.shape
    t = jnp.arange(S)
    a_cos, a_sin = rope_tables(t, ROPE_DIM, ROPE_THETA)
    rows = S // GRID_W
    row_cs = rope_tables(jnp.repeat(jnp.arange(rows), GRID_W), HEAD_DIM // 2, AXIAL_THETA)
    col_cs = rope_tables(jnp.tile(jnp.arange(GRID_W), rows), HEAD_DIM // 2, AXIAL_THETA)
    offsets = [int(o) for o in np.cumsum(IN_WIDTHS)[:-1]]
    to_heads = lambda u, n: u.reshape(B, S, n, HEAD_DIM).transpose(0, 2, 1, 3)
    merge = lambda u: u.transpose(0, 2, 1, 3).reshape(B, S, -1)

    h = x
    for i in range(DEPTH):
        hn = rms_norm(h, norm_mix[i])
        proj = jnp.einsum('bsd,de->bse', hn, w_in[i])
        a_q, a_k, a_v, b_q, b_k, b_v, c_qkv, c_z, c_beta, c_a = jnp.split(proj, offsets, axis=-1)

        qa = partial_rope(to_heads(a_q, A_HEADS), a_cos, a_sin)
        ka = partial_rope(to_heads(a_k, A_HEADS), a_cos, a_sin)
        y_a = rms_norm(merge(dilated_mixer(qa, ka, to_heads(a_v, A_HEADS))), out_norm_a[i])

        qb = axial_rope(rms_norm(to_heads(b_q, B_HEADS), q_norm[i]), row_cs, col_cs)
        kb = axial_rope(rms_norm(to_heads(b_k, B_KV_HEADS), k_norm[i]), row_cs, col_cs)
        y_b = rms_norm(merge(gqa_blocked(qb, kb, to_heads(b_v, B_KV_HEADS))), out_norm_b[i])

        y_c = gated_deltanet_mixer(c_qkv, c_z, c_beta, c_a, conv_w[i], gdn_a_log[i],
                                   gdn_dt_bias[i], gdn_norm[i])

        y = jnp.concatenate([y_a, y_b, y_c], axis=-1)
        h = h + jnp.einsum('bse,ed->bsd', y, w_out[i])
        h = h + expert_choice_moe(rms_norm(h, norm_moe[i]), w_router[i], w_gate[i], w_up[i], w_down[i])
        gate = jax.nn.sigmoid(jnp.einsum('bsd,de->bse', rms_norm(h, norm_ple[i]), w_ple_gate[i]))
        h = h + gate * jnp.einsum('bsk,kd->bsd', p[i], w_ple[i])
    return rms_norm(h, norm_final)
```

```python
import functools

import jax
import jax.numpy as jnp
from jax import lax
from jax.experimental import pallas as pl
from jax.experimental.pallas import tpu as pltpu

F32 = jnp.float32
BF16 = jnp.bfloat16

D_MODEL = 2048
HEAD_DIM = 128
A_HEADS = 4
B_HEADS = 8
B_KV_HEADS = 2
B_GROUP = B_HEADS // B_KV_HEADS
C_HEADS = 4
A_W = A_HEADS * HEAD_DIM
B_QW = B_HEADS * HEAD_DIM
B_KVW = B_KV_HEADS * HEAD_DIM
C_W = C_HEADS * HEAD_DIM
DILATED_PATTERNS = ((128, 1), (512, 4), (2048, 16))
ROPE_THETA = 500000.0
ROPE_DIM = HEAD_DIM // 4
AXIAL_THETA = 10000.0
GRID_W = 64
CONV_K = 5
DELTA_CHUNK = 64
N_EXPERTS = 16
EC_CAPACITY = 2
EXPERT_FF = D_MODEL // 2
PLE_DIM = 256
NORM_EPS = 1e-6
D_MAIN = 3 * A_W + B_QW + 2 * B_KVW + 4 * C_W
D_SMALL = 4 * C_HEADS

COL_AQ, COL_AK, COL_AV = 0, 4, 8
COL_BQ, COL_BK, COL_BV = 12, 20, 22
COL_CQ, COL_CK, COL_CV, COL_CZ = 24, 28, 32, 36

LANES = 128
VMEM_LIMIT = 56 * 1024 * 1024
NEG_BIG = -1e30


def _params(*sem):
    return pltpu.CompilerParams(dimension_semantics=sem, vmem_limit_bytes=VMEM_LIMIT)


def _nt(a, b):
    return lax.dot_general(a, b, (((1,), (1,)), ((), ())), preferred_element_type=F32)


def _tn(a, b):
    return lax.dot_general(a, b, (((0,), (0,)), ((), ())), preferred_element_type=F32)


def _dot(a, b):
    return jnp.dot(a, b, preferred_element_type=F32)


def _split(x):
    hi = x.astype(BF16)
    lo = (x - hi.astype(F32)).astype(BF16)
    return hi, lo


def _dot3(a, b):
    ah, al = _split(a)
    bh, bl = _split(b)
    return _dot(ah, bh) + (_dot(ah, bl) + _dot(al, bh))


def _rms(x, gain):
    return x * lax.rsqrt(jnp.mean(x * x, axis=-1, keepdims=True) + NORM_EPS) * gain


def _sigmoid(x):
    return 1.0 / (1.0 + jnp.exp(-x))


def _silu(x):
    return x * _sigmoid(x)


def _row_chunks(n_rows, chunk, fn):
    def body(c, carry):
        fn(pl.ds(pl.multiple_of(c * chunk, chunk), chunk))
        return carry
    lax.fori_loop(0, n_rows // chunk, body, 0)


def _proj_body(x_ref, g_ref, w_ref, ws_ref, o_ref, os_ref, xn_ref):
    tm = x_ref.shape[0]

    @pl.when(pl.program_id(1) == 0)
    def _():
        def norm_rows(rows):
            xn_ref[rows, :] = _rms(x_ref[rows, :], g_ref[...]).astype(BF16)
        _row_chunks(tm, 128, norm_rows)
        os_ref[...] = _dot(xn_ref[...], ws_ref[...].astype(BF16))

    o_ref[...] = _dot(xn_ref[...], w_ref[...].astype(BF16)).astype(o_ref.dtype)


def _proj(h2d, gain, w_in, w_small, layer, tm=1024, tn=512):
    m = h2d.shape[0]
    return pl.pallas_call(
        _proj_body,
        grid=(m // tm, D_MAIN // tn),
        in_specs=[
            pl.BlockSpec((tm, D_MODEL), lambda i, j: (i, 0)),
            pl.BlockSpec((1, D_MODEL), lambda i, j: (0, 0)),
            pl.BlockSpec((None, D_MODEL, tn), lambda i, j: (layer, 0, j)),
            pl.BlockSpec((D_MODEL, LANES), lambda i, j: (0, 0)),
        ],
        out_specs=[
            pl.BlockSpec((tm, tn), lambda i, j: (i, j)),
            pl.BlockSpec((tm, LANES), lambda i, j: (i, 0)),
        ],
        out_shape=[
            jax.ShapeDtypeStruct((m, D_MAIN), BF16),
            jax.ShapeDtypeStruct((m, LANES), F32),
        ],
        scratch_shapes=[pltpu.VMEM((tm, D_MODEL), BF16)],
        compiler_params=_params("arbitrary", "arbitrary"),
        name="proj",
    )(h2d, gain, w_in, w_small)


def _mixa_body(q_ref, k_ref, v_ref, c_ref, sp_ref, sm_ref, o_ref,
               qf, kf, vf, qd, kp, vp, acc, den, mrun):
    S = q_ref.shape[0]
    half = 64
    blk = 128
    scale = HEAD_DIM ** -0.5

    def prep(rows):
        def rope(x):
            return (x * c_ref[rows, :] + pltpu.roll(x, ROPE_DIM // 2, 1) * sp_ref[rows, :]
                    + pltpu.roll(x, LANES - ROPE_DIM // 2, 1) * sm_ref[rows, :])
        qf[rows, :] = rope(q_ref[rows, :].astype(F32)) * scale
        kf[rows, :] = rope(k_ref[rows, :].astype(F32))
        vf[rows, :] = v_ref[rows, :].astype(F32)
        acc[rows, :] = jnp.zeros((512, LANES), F32)
        den[rows, :] = jnp.zeros((512, LANES), F32)
        mrun[rows, :] = jnp.full((512, LANES), NEG_BIG, F32)
    _row_chunks(S, 512, prep)

    row = lax.broadcasted_iota(jnp.int32, (blk, 2 * blk), 0)
    col = lax.broadcasted_iota(jnp.int32, (blk, 2 * blk), 1)
    in_band = jnp.abs(col - row - half) <= half

    for window, dil in DILATED_PATTERNS:
        assert window // (2 * dil) == half
        L = S // dil
        nblk = L // blk
        zk = jnp.zeros((half, LANES), BF16)
        zv = jnp.zeros((half, 2 * LANES), BF16)
        kp[0:half, :] = zk
        kp[half + L:2 * half + L, :] = zk
        vp[0:half, :] = zv
        vp[half + L:2 * half + L, :] = zv

        def per_residue(r, carry, dil=dil, L=L, nblk=nblk):
            def deinterleave(c, carry2):
                src = pl.ds(r + c * (256 * dil), 256, stride=dil) if dil > 1 else pl.ds(
                    pl.multiple_of(c * 256, 256), 256)
                dst = pl.ds(pl.multiple_of(half + c * 256, half), 256)
                qd[pl.ds(pl.multiple_of(c * 256, 256), 256), :] = qf[src, :].astype(BF16)
                kp[dst, :] = kf[src, :].astype(BF16)
                vp[dst, 0:LANES] = vf[src, :].astype(BF16)
                vp[dst, LANES:2 * LANES] = jnp.ones((256, LANES), BF16)
                return carry2
            lax.fori_loop(0, L // 256, deinterleave, 0)

            def block(nb, carry2):
                w0 = pl.multiple_of(nb * blk, blk)
                qb = qd[pl.ds(w0, blk), :]
                kw = kp[pl.ds(w0, 2 * blk), :]
                vw = vp[pl.ds(w0, 2 * blk), :]
                s = _nt(qb, kw)
                kpos = nb * blk - half + col
                valid = in_band & (kpos >= 0) & (kpos < L)
                s = jnp.where(valid, s, NEG_BIG)
                m_b = jnp.max(s, axis=1, keepdims=True)
                e = jnp.exp(s - m_b)
                od = _dot(e.astype(BF16), vw)
                if dil > 1:
                    rows = pl.ds(r + nb * (blk * dil), blk, stride=dil)
                else:
                    rows = pl.ds(w0, blk)
                m_old = mrun[rows, :]
                m_new = jnp.maximum(m_old, m_b)
                a_old = jnp.exp(m_old - m_new)
                a_new = jnp.exp(m_b - m_new)
                acc[rows, :] = acc[rows, :] * a_old + od[:, 0:LANES] * a_new
                den[rows, :] = den[rows, :] * a_old + od[:, LANES:2 * LANES] * a_new
                mrun[rows, :] = m_new
                return carry2
            lax.fori_loop(0, nblk, block, 0)
            return carry
        lax.fori_loop(0, dil, per_residue, 0)

    def finish(rows):
        o_ref[rows, :] = (acc[rows, :] / den[rows, :]).astype(o_ref.dtype)
    _row_chunks(S, 512, finish)


def _mix_a(proj3, tabs):
    B, S, _ = proj3.shape
    head = lambda base: pl.BlockSpec((None, S, LANES), lambda b, h: (b, 0, base + h))
    tab = pl.BlockSpec((S, LANES), lambda b, h: (0, 0))
    return pl.pallas_call(
        _mixa_body,
        grid=(B, A_HEADS),
        in_specs=[head(COL_AQ), head(COL_AK), head(COL_AV), tab, tab, tab],
        out_specs=pl.BlockSpec((None, S, LANES), lambda b, h: (b, 0, h)),
        out_shape=jax.ShapeDtypeStruct((B, S, A_W), BF16),
        scratch_shapes=[
            pltpu.VMEM((S, LANES), F32), pltpu.VMEM((S, LANES), F32), pltpu.VMEM((S, LANES), F32),
            pltpu.VMEM((S, LANES), BF16),
            pltpu.VMEM((S + 128, LANES), BF16),
            pltpu.VMEM((S + 128, 2 * LANES), BF16),
            pltpu.VMEM((S, LANES), F32), pltpu.VMEM((S, LANES), F32), pltpu.VMEM((S, LANES), F32),
        ],
        compiler_params=_params("arbitrary", "arbitrary"),
        name="mix_a",
    )(proj3, proj3, proj3, *tabs)


def _mixb_body(q_ref, k_ref, v_ref, c_ref, sp_ref, sm_ref, qg_ref, kg_ref, o_ref,
               kt_ref, va_ref, q4_ref, m_ref, acc_ref, *, tk):
    S = k_ref.shape[0]
    tq = q_ref.shape[0]
    qi = pl.program_id(2)
    quarter = HEAD_DIM // 4

    def rope(x, rows):
        return (x * c_ref[rows, :] + pltpu.roll(x, quarter, 1) * sp_ref[rows, :]
                + pltpu.roll(x, LANES - quarter, 1) * sm_ref[rows, :])

    @pl.when(qi == 0)
    def _():
        def kv_chunk(c, carry):
            rows = pl.ds(pl.multiple_of(c * tk, tk), tk)
            k = rope(_rms(k_ref[rows, :].astype(F32), kg_ref[...]), rows)
            kt_ref[c] = k.T.astype(BF16)
            va_ref[rows, 0:LANES] = v_ref[rows, :]
            va_ref[rows, LANES:2 * LANES] = jnp.ones((tk, LANES), BF16)
            return carry
        lax.fori_loop(0, S // tk, kv_chunk, 0)

    rows_q = pl.ds(pl.multiple_of(qi * tq, tq), tq)
    for g in range(B_GROUP):
        q = _rms(q_ref[:, g * LANES:(g + 1) * LANES].astype(F32), qg_ref[...])
        q = rope(q, rows_q) * (HEAD_DIM ** -0.5)
        q4_ref[g * tq:(g + 1) * tq, :] = q.astype(BF16)
    m_ref[...] = jnp.full(m_ref.shape, NEG_BIG, F32)
    acc_ref[...] = jnp.zeros(acc_ref.shape, F32)

    def kv_step(c, carry):
        rows = pl.ds(pl.multiple_of(c * tk, tk), tk)
        s = _dot(q4_ref[...], kt_ref[c])
        m_old = m_ref[...]
        m_new = jnp.maximum(m_old, jnp.max(s, axis=1, keepdims=True))
        alpha = jnp.exp(m_old - m_new)
        p = jnp.exp(s - pltpu.repeat(m_new, tk // LANES, axis=1))
        acc_ref[...] = acc_ref[...] * pltpu.repeat(alpha, 2, axis=1) + _dot(p.astype(BF16), va_ref[rows, :])
        m_ref[...] = m_new
        return carry
    lax.fori_loop(0, S // tk, kv_step, 0)

    o = acc_ref[:, 0:LANES] / acc_ref[:, LANES:2 * LANES]
    for g in range(B_GROUP):
        o_ref[:, g * LANES:(g + 1) * LANES] = o[g * tq:(g + 1) * tq, :].astype(o_ref.dtype)


def _mix_b(proj3, tabs, q_gain, k_gain, tq=128, tk=512):
    B, S, _ = proj3.shape
    qw = B_GROUP * LANES
    tab = pl.BlockSpec((S, LANES), lambda b, h, i: (0, 0))
    gain = pl.BlockSpec((1, LANES), lambda b, h, i: (0, 0))
    return pl.pallas_call(
        functools.partial(_mixb_body, tk=tk),
        grid=(B, B_KV_HEADS, S // tq),
        in_specs=[
            pl.BlockSpec((None, tq, qw), lambda b, h, i: (b, i, COL_BQ // B_GROUP + h)),
            pl.BlockSpec((None, S, LANES), lambda b, h, i: (b, 0, COL_BK + h)),
            pl.BlockSpec((None, S, LANES), lambda b, h, i: (b, 0, COL_BV + h)),
            tab, tab, tab, gain, gain,
        ],
        out_specs=pl.BlockSpec((None, tq, qw), lambda b, h, i: (b, i, h)),
        out_shape=jax.ShapeDtypeStruct((B, S, B_QW), BF16),
        scratch_shapes=[
            pltpu.VMEM((S // tk, LANES, tk), BF16),
            pltpu.VMEM((S, 2 * LANES), BF16),
            pltpu.VMEM((B_GROUP * tq, LANES), BF16),
            pltpu.VMEM((B_GROUP * tq, LANES), F32),
            pltpu.VMEM((B_GROUP * tq, 2 * LANES), F32),
        ],
        compiler_params=_params("arbitrary", "arbitrary", "arbitrary"),
        name="mix_b",
    )(proj3, proj3, proj3, *tabs, q_gain, k_gain)


def _softplus(x):
    return jnp.maximum(x, 0.0) + jnp.log(1.0 + jnp.exp(-jnp.abs(x)))


def _mixc_body(q_ref, k_ref, v_ref, z_ref, cwq_ref, cwk_ref, cwv_ref, lg_ref, par_ref, gn_ref, o_ref,
               xp, qn, kn, vn, tab, qp_ref, op_ref, p_ref, n_ref):
    S = q_ref.shape[0]
    C = DELTA_CHUNK
    n_chunks = S // C
    pad = 8

    for src, cw, dst, kind in ((q_ref, cwq_ref, qn, "q"), (k_ref, cwk_ref, kn, "k"), (v_ref, cwv_ref, vn, "v")):
        xp[0:pad, :] = jnp.zeros((pad, LANES), F32)
        xp[pad + S:2 * pad + S, :] = jnp.zeros((pad, LANES), F32)

        def load(rows, src=src):
            xp[pl.ds(pl.multiple_of(rows.start + pad, pad), 512), :] = src[rows, :].astype(F32)
        _row_chunks(S, 512, load)
        for c in range(S // 512):
            base = pad - CONV_K // 2 + c * 512
            y = xp[base:base + 512, :] * cw[0:1, :]
            for j in range(1, CONV_K):
                y = y + xp[base + j:base + j + 512, :] * cw[j:j + 1, :]
            y = _silu(y)
            if kind != "v":
                y = y * lax.rsqrt(jnp.sum(y * y, axis=-1, keepdims=True) + NORM_EPS)
            if kind == "q":
                y = y * (HEAD_DIM ** -0.5)
            dst[c * 512:(c + 1) * 512, :] = y

    ri = lax.broadcasted_iota(jnp.int32, (C, C), 0)
    ci = lax.broadcasted_iota(jnp.int32, (C, C), 1)
    eye = ri == ci
    eye_f = jnp.where(eye, 1.0, 0.0)
    same_block = [jnp.right_shift(ri, s) == jnp.right_shift(ci, s) for s in (3, 4, 5, 6)]
    for d in range(2):
        g = -jnp.exp(par_ref[d:d + 1, :]) * _softplus(lg_ref[2 + d] + par_ref[2 + d:3 + d, :])
        cum = jnp.where(ri <= ci, 1.0, 0.0) if d == 0 else jnp.where(ri >= ci, 1.0, 0.0)
        tab[d] = _dot3(g, cum.astype(F32))
        tab[2 + d] = _sigmoid(lg_ref[d])

    def chunk_pre(n, d):
        rows = pl.ds(pl.multiple_of(n * C, C), C)
        q = qn[rows, :]
        k = kn[rows, :]
        v = vn[rows, :]
        gr = tab[d, pl.ds(n, 1), :]
        br = tab[2 + d, pl.ds(n, 1), :]
        gcol = jnp.sum(jnp.where(eye, gr, 0.0), axis=1, keepdims=True)
        bcol = jnp.sum(jnp.where(eye, br, 0.0), axis=1, keepdims=True)
        incl = (ci <= ri) if d == 0 else (ci >= ri)
        strict = (ci < ri) if d == 0 else (ci > ri)
        decay = jnp.exp(jnp.where(incl, gcol - gr, NEG_BIG))
        kb = k * bcol
        vb = v * bcol
        k16 = k.astype(BF16)
        a = jnp.where(strict, _nt(kb.astype(BF16), k16) * decay, 0.0)
        eg = jnp.exp(gcol)
        rhs = jnp.concatenate([vb, kb * eg], axis=1)
        x = jnp.where(same_block[0], -a, 0.0)
        s1 = eye_f + x
        x2 = _dot3(x, x)
        s2 = s1 + _dot3(x2, s1)
        t = s2 + _dot3(_dot3(x2, x2), s2)
        for lvl in range(1, len(same_block)):
            a_off = jnp.where(same_block[lvl] & jnp.logical_not(same_block[lvl - 1]), a, 0.0)
            t = t - _dot3(t, _dot3(a_off, t))
        r16 = _dot(t.astype(BF16), rhs.astype(BF16)).astype(BF16)
        qk = _nt(q.astype(BF16), k16) * decay
        g_last = gr[:, C - 1:C] if d == 0 else gr[:, 0:1]
        kprime = k * jnp.exp(g_last - gcol)
        np_ = _dot(kprime.T.astype(BF16), r16)
        qo = _dot(qk.astype(BF16), r16)
        qp_ref[d, rows, :] = (q * eg - qo[:, LANES:2 * LANES]).astype(BF16)
        op_ref[d, rows, :] = qo[:, 0:LANES]
        n_ref[d, n] = np_[:, 0:LANES]
        p_ref[d, n] = np_[:, LANES:2 * LANES].astype(BF16)

    def pre_body(n, carry):
        chunk_pre(n, 0)
        chunk_pre(n, 1)
        return carry
    lax.fori_loop(0, n_chunks, pre_body, 0)

    def scan_body(i, carry):
        new = []
        for d, st in enumerate(carry):
            n = i if d == 0 else n_chunks - 1 - i
            rows = pl.ds(pl.multiple_of(n * C, C), C)
            gr = tab[d, pl.ds(n, 1), :]
            g_last = gr[:, C - 1:C] if d == 0 else gr[:, 0:1]
            s16 = st.astype(BF16)
            op_ref[d, rows, :] = op_ref[d, rows, :] + _dot(qp_ref[d, rows, :], s16)
            new.append(st * jnp.exp(g_last) - _dot(p_ref[d, n], s16) + n_ref[d, n])
        return tuple(new)
    zero = jnp.zeros((HEAD_DIM, HEAD_DIM), F32)
    lax.fori_loop(0, n_chunks, scan_body, (zero, zero))

    def finish(rows):
        o = op_ref[0, rows, :] + op_ref[1, rows, :]
        o_ref[rows, :] = (_rms(o, gn_ref[...]) * _silu(z_ref[rows, :].astype(F32))).astype(o_ref.dtype)
    _row_chunks(S, 512, finish)


def _mix_c(proj3, conv_w, logits_rows, par, gn, layer):
    B, S, _ = proj3.shape
    n_chunks = S // DELTA_CHUNK
    head = lambda base: pl.BlockSpec((None, S, LANES), lambda b, h: (b, 0, base + h))
    cw = lambda base: pl.BlockSpec((None, CONV_K, LANES), lambda b, h: (layer, 0, base + h))
    return pl.pallas_call(
        _mixc_body,
        grid=(B, C_HEADS),
        in_specs=[
            head(COL_CQ), head(COL_CK), head(COL_CV), head(COL_CZ),
            cw(0), cw(C_HEADS), cw(2 * C_HEADS),
            pl.BlockSpec((None, 4, None, n_chunks, DELTA_CHUNK), lambda b, h: (b, 0, h, 0, 0)),
            pl.BlockSpec((None, 4, DELTA_CHUNK), lambda b, h: (h, 0, 0)),
            pl.BlockSpec((1, LANES), lambda b, h: (0, 0)),
        ],
        out_specs=pl.BlockSpec((None, S, LANES), lambda b, h: (b, 0, h)),
        out_shape=jax.ShapeDtypeStruct((B, S, C_W), BF16),
        scratch_shapes=[
            pltpu.VMEM((S + 16, LANES), F32),
            pltpu.VMEM((S, LANES), F32), pltpu.VMEM((S, LANES), F32), pltpu.VMEM((S, LANES), F32),
            pltpu.VMEM((4, n_chunks, DELTA_CHUNK), F32),
            pltpu.VMEM((2, S, LANES), BF16),
            pltpu.VMEM((2, S, LANES), F32),
            pltpu.VMEM((2, n_chunks, HEAD_DIM, HEAD_DIM), BF16),
            pltpu.VMEM((2, n_chunks, HEAD_DIM, HEAD_DIM), F32),
        ],
        compiler_params=_params("arbitrary", "arbitrary"),
        name="mix_c",
    )(proj3, proj3, proj3, proj3, conv_w, conv_w, conv_w, logits_rows, par, gn)


def _outproj_body(ya_ref, yb_ref, yc_ref, ga_ref, gb_ref, w_ref, h_ref, o_ref, yn_ref):
    tm = ya_ref.shape[0]

    @pl.when(pl.program_id(1) == 0)
    def _():
        def norm_rows(rows):
            yn_ref[rows, 0:A_W] = _rms(ya_ref[rows, :].astype(F32), ga_ref[...]).astype(BF16)
            yn_ref[rows, A_W:A_W + B_QW] = _rms(yb_ref[rows, :].astype(F32), gb_ref[...]).astype(BF16)
            yn_ref[rows, A_W + B_QW:D_MODEL] = yc_ref[rows, :]
        _row_chunks(tm, 128, norm_rows)

    o_ref[...] = h_ref[...] + _dot(yn_ref[...], w_ref[...].astype(BF16))


def _out_proj(ya, yb, yc, ga, gb, w_out, h2d, layer, tm=1024, tn=512):
    m = h2d.shape[0]
    return pl.pallas_call(
        _outproj_body,
        grid=(m // tm, D_MODEL // tn),
        in_specs=[
            pl.BlockSpec((tm, A_W), lambda i, j: (i, 0)),
            pl.BlockSpec((tm, B_QW), lambda i, j: (i, 0)),
            pl.BlockSpec((tm, C_W), lambda i, j: (i, 0)),
            pl.BlockSpec((1, A_W), lambda i, j: (0, 0)),
            pl.BlockSpec((1, B_QW), lambda i, j: (0, 0)),
            pl.BlockSpec((None, D_MODEL, tn), lambda i, j: (layer, 0, j)),
            pl.BlockSpec((tm, tn), lambda i, j: (i, j)),
        ],
        out_specs=pl.BlockSpec((tm, tn), lambda i, j: (i, j)),
        out_shape=jax.ShapeDtypeStruct((m, D_MODEL), F32),
        scratch_shapes=[pltpu.VMEM((tm, D_MODEL), BF16)],
        compiler_params=_params("arbitrary", "arbitrary"),
        name="out_proj",
    )(ya, yb, yc, ga, gb, w_out, h2d)


def _moepre_body(h_ref, g_ref, wr_ref, hn_ref, aff_ref):
    xn = _rms(h_ref[...], g_ref[...])
    hi, lo = _split(xn)
    hn_ref[...] = hi
    whi, wlo = _split(wr_ref[...])
    logits = _nt(whi, hi) + (_nt(whi, lo) + _nt(wlo, hi))
    e = jnp.exp(logits - jnp.max(logits, axis=0, keepdims=True))
    aff_ref[...] = e / jnp.sum(e, axis=0, keepdims=True)


def _moe_pre(h3, gain, w_router_t, tm=256):
    B, S, _ = h3.shape
    return pl.pallas_call(
        _moepre_body,
        grid=(B, S // tm),
        in_specs=[
            pl.BlockSpec((None, tm, D_MODEL), lambda b, i: (b, i, 0)),
            pl.BlockSpec((1, D_MODEL), lambda b, i: (0, 0)),
            pl.BlockSpec((N_EXPERTS, D_MODEL), lambda b, i: (0, 0)),
        ],
        out_specs=[
            pl.BlockSpec((None, tm, D_MODEL), lambda b, i: (b, i, 0)),
            pl.BlockSpec((None, N_EXPERTS, tm), lambda b, i: (b, 0, i)),
        ],
        out_shape=[
            jax.ShapeDtypeStruct((B, S, D_MODEL), BF16),
            jax.ShapeDtypeStruct((B, N_EXPERTS, S), F32),
        ],
        compiler_params=_params("arbitrary", "arbitrary"),
        name="moe_pre",
    )(h3, gain, w_router_t)


def _topk_body(aff_ref, rank_ref, gate_ref, offs_ref, *, cap):
    x = aff_ref[...]
    E, S = x.shape
    xb = pltpu.bitcast(x, jnp.int32)
    count = lambda mask: jnp.sum(jnp.where(mask, 1.0, 0.0), axis=1, keepdims=True)

    def value_bit(it, t):
        cand = t | jnp.left_shift(jnp.int32(1), 30 - it)
        return jnp.where(count(xb >= cand) >= cap, cand, t)
    thr = lax.fori_loop(0, 31, value_bit, jnp.zeros((E, 1), jnp.int32))

    above = xb > thr
    tied = xb == thr
    need = cap - count(above)
    idx = lax.broadcasted_iota(jnp.int32, (E, S), 1)

    def index_bit(it, j):
        cand = j | jnp.left_shift(jnp.int32(1), 11 - it)
        return jnp.where(count(tied & (idx < cand)) < need, cand, j)
    assert S == 4096
    jmax = lax.fori_loop(0, 12, index_bit, jnp.zeros((E, 1), jnp.int32))
    sel = above | (tied & (idx <= jmax) & (need > 0.0))
    gate_ref[...] = jnp.where(sel, x, 0.0)

    ri = lax.broadcasted_iota(jnp.int32, (LANES, LANES), 0)
    ci = lax.broadcasted_iota(jnp.int32, (LANES, LANES), 1)
    before = jnp.where(ri < ci, 1.0, 0.0).astype(BF16)
    lane = lax.broadcasted_iota(jnp.int32, (E, LANES), 1)
    off = jnp.zeros((E, 1), F32)
    offs = jnp.zeros((E, LANES), F32)
    self_f = jnp.where(sel, 1.0, 0.0)
    for c in range(S // LANES):
        m_c = self_f[:, c * LANES:(c + 1) * LANES]
        rank_c = _dot(m_c.astype(BF16), before)
        rank_ref[:, c * LANES:(c + 1) * LANES] = jnp.where(m_c > 0.0, rank_c + off, -1.0)
        offs = jnp.where(lane == c, off, offs)
        off = off + jnp.sum(m_c, axis=1, keepdims=True)
    offs = jnp.where(lane == S // LANES, off, offs)
    offs_ref[...] = offs.astype(jnp.int32)


def _topk(aff_t, cap):
    B, E, S = aff_t.shape
    spec = pl.BlockSpec((None, E, S), lambda b: (b, 0, 0))
    return pl.pallas_call(
        functools.partial(_topk_body, cap=cap),
        grid=(B,),
        in_specs=[spec],
        out_specs=[spec, spec, pl.BlockSpec((None, E, LANES), lambda b: (b, 0, 0))],
        out_shape=[
            jax.ShapeDtypeStruct((B, E, S), F32),
            jax.ShapeDtypeStruct((B, E, S), F32),
            jax.ShapeDtypeStruct((B, E, LANES), jnp.int32),
        ],
        compiler_params=_params("arbitrary"),
        name="topk",
    )(aff_t)


def _gather_body(offs_ref, hn_ref, rk_ref, gt_ref, xs_ref, gs_ref, acc_ref, gacc_ref, *, cap):
    S = hn_ref.shape[0]
    b = pl.program_id(0)
    e = pl.program_id(1)
    base = (b * N_EXPERTS + e) * LANES
    n_sb = cap // LANES
    acc_ref[...] = jnp.zeros(acc_ref.shape, F32)
    gacc_ref[...] = jnp.zeros(gacc_ref.shape, F32)
    slot = lax.broadcasted_iota(jnp.int32, (LANES, LANES), 0).astype(F32)

    def chunk(c, carry):
        lo = offs_ref[base + c]
        hi = offs_ref[base + c + 1]
        rk = rk_ref[pl.ds(c, 1), :]
        gt = gt_ref[pl.ds(c, 1), :]
        h_c = hn_ref[pl.ds(pl.multiple_of(c * LANES, LANES), LANES), :]
        for sb in range(n_sb):
            @pl.when((lo < (sb + 1) * LANES) & (hi > sb * LANES))
            def _():
                onehot = (slot + float(sb * LANES)) == rk
                acc_ref[sb] = acc_ref[sb] + _dot(jnp.where(onehot, 1.0, 0.0).astype(BF16), h_c)
                gacc_ref[sb] = gacc_ref[sb] + jnp.sum(jnp.where(onehot, gt, 0.0), axis=1, keepdims=True)
        return carry
    lax.fori_loop(0, S // LANES, chunk, 0)

    for sb in range(n_sb):
        xs_ref[sb * LANES:(sb + 1) * LANES, :] = acc_ref[sb].astype(xs_ref.dtype)
        gs_ref[sb * LANES:(sb + 1) * LANES, :] = gacc_ref[sb]


def _gather(offs_flat, hn3, rank4, gate4, cap):
    B, S, _ = hn3.shape
    n_chunks = S // LANES
    row = pl.BlockSpec((None, None, n_chunks, LANES), lambda b, e, offs: (b, e, 0, 0))
    return pl.pallas_call(
        functools.partial(_gather_body, cap=cap),
        grid_spec=pltpu.PrefetchScalarGridSpec(
            num_scalar_prefetch=1,
            grid=(B, N_EXPERTS),
            in_specs=[
                pl.BlockSpec((None, S, D_MODEL), lambda b, e, offs: (b, 0, 0)),
                row, row,
            ],
            out_specs=[
                pl.BlockSpec((None, None, cap, D_MODEL), lambda b, e, offs: (b, e, 0, 0)),
                pl.BlockSpec((None, None, cap, LANES), lambda b, e, offs: (b, e, 0, 0)),
            ],
            scratch_shapes=[
                pltpu.VMEM((cap // LANES, LANES, D_MODEL), F32),
                pltpu.VMEM((cap // LANES, LANES, LANES), F32),
            ],
        ),
        out_shape=[
            jax.ShapeDtypeStruct((B, N_EXPERTS, cap, D_MODEL), BF16),
            jax.ShapeDtypeStruct((B, N_EXPERTS, cap, LANES), F32),
        ],
        compiler_params=_params("arbitrary", "arbitrary"),
        name="gather",
    )(offs_flat, hn3, rank4, gate4)


def _ffn_body(xs_ref, wg_ref, wu_ref, wd_ref, gs_ref, y_ref, acc_ref):
    f = pl.program_id(1)
    n_f = pl.num_programs(1)
    wg = wg_ref[...].astype(BF16)
    wu = wu_ref[...].astype(BF16)
    wd = wd_ref[...].astype(BF16)
    for b in range(xs_ref.shape[0]):
        x = xs_ref[b]
        hg = _dot(x, wg)
        hu = _dot(x, wu)
        contrib = _dot((_silu(hg) * hu).astype(BF16), wd)

        @pl.when(f == 0)
        def _():
            acc_ref[b] = contrib

        @pl.when(f > 0)
        def _():
            acc_ref[b] = acc_ref[b] + contrib

        @pl.when(f == n_f - 1)
        def _():
            y_ref[b] = (acc_ref[b] * gs_ref[b][:, 0:1]).astype(y_ref.dtype)


def _ffn(xs, w_gate, w_up, w_down, gslot, layer, tf=256):
    B, E, cap, _ = xs.shape
    return pl.pallas_call(
        _ffn_body,
        grid=(E, EXPERT_FF // tf),
        in_specs=[
            pl.BlockSpec((B, None, cap, D_MODEL), lambda e, f: (0, e, 0, 0)),
            pl.BlockSpec((None, None, D_MODEL, tf), lambda e, f: (layer, e, 0, f)),
            pl.BlockSpec((None, None, D_MODEL, tf), lambda e, f: (layer, e, 0, f)),
            pl.BlockSpec((None, None, tf, D_MODEL), lambda e, f: (layer, e, f, 0)),
            pl.BlockSpec((B, None, cap, LANES), lambda e, f: (0, e, 0, 0)),
        ],
        out_specs=pl.BlockSpec((B, None, cap, D_MODEL), lambda e, f: (0, e, 0, 0)),
        out_shape=jax.ShapeDtypeStruct((B, E, cap, D_MODEL), BF16),
        scratch_shapes=[pltpu.VMEM((B, cap, D_MODEL), F32)],
        compiler_params=_params("arbitrary", "arbitrary"),
        name="ffn",
    )(xs, w_gate, w_up, w_down, gslot)


def _scatter_body(offs_ref, h_ref, y_ref, rk_ref, o_ref, *, cap):
    tp = h_ref.shape[0]
    b = pl.program_id(0)
    t = pl.program_id(1)
    e = pl.program_id(2)
    base = (b * N_EXPERTS + e) * LANES
    ri = lax.broadcasted_iota(jnp.int32, (LANES, LANES), 0)
    ci = lax.broadcasted_iota(jnp.int32, (LANES, LANES), 1)
    eye = ri == ci
    lane_slot = ci.astype(F32)

    @pl.when(e == 0)
    def _():
        o_ref[...] = h_ref[...]

    for cc in range(tp // LANES):
        c = t * (tp // LANES) + cc
        lo = offs_ref[base + c]
        hi = offs_ref[base + c + 1]
        for sb in range(cap // LANES):
            @pl.when((lo < (sb + 1) * LANES) & (hi > sb * LANES))
            def _():
                rk = rk_ref[pl.ds(c, 1), :]
                rk_col = jnp.sum(jnp.where(eye, rk, 0.0), axis=1, keepdims=True)
                onehot = (lane_slot + float(sb * LANES)) == rk_col
                rows = slice(cc * LANES, (cc + 1) * LANES)
                o_ref[rows, :] = o_ref[rows, :] + _dot(
                    jnp.where(onehot, 1.0, 0.0).astype(BF16), y_ref[sb * LANES:(sb + 1) * LANES, :])


def _scatter(offs_flat, h3, y, rank4, cap, tp=1024):
    B, S, _ = h3.shape
    n_chunks = S // LANES
    return pl.pallas_call(
        functools.partial(_scatter_body, cap=cap),
        grid_spec=pltpu.PrefetchScalarGridSpec(
            num_scalar_prefetch=1,
            grid=(B, S // tp, N_EXPERTS),
            in_specs=[
                pl.BlockSpec((None, tp, D_MODEL), lambda b, t, e, offs: (b, t, 0)),
                pl.BlockSpec((None, None, cap, D_MODEL), lambda b, t, e, offs: (b, e, 0, 0)),
                pl.BlockSpec((None, None, n_chunks, LANES), lambda b, t, e, offs: (b, e, 0, 0)),
            ],
            out_specs=pl.BlockSpec((None, tp, D_MODEL), lambda b, t, e, offs: (b, t, 0)),
        ),
        out_shape=jax.ShapeDtypeStruct((B, S, D_MODEL), F32),
        compiler_params=_params("arbitrary", "arbitrary", "arbitrary"),
        name="scatter",
    )(offs_flat, h3, y, rank4)


def _ple_body(hrow_ref, g_ref, wg_ref, p_ref, wp_ref, hblk_ref, o_ref, hn_ref):
    tm = hrow_ref.shape[0]

    @pl.when(pl.program_id(1) == 0)
    def _():
        def norm_rows(rows):
            hn_ref[rows, :] = _rms(hrow_ref[rows, :], g_ref[...]).astype(BF16)
        _row_chunks(tm, 128, norm_rows)

    gate = _sigmoid(_dot(hn_ref[...], wg_ref[...].astype(BF16)))
    emb = _dot(p_ref[...].astype(BF16), wp_ref[...].astype(BF16))
    o_ref[...] = hblk_ref[...] + gate * emb


def _ple(h2d, gain, w_gate, p2d, w_ple, layer, tm=1024, tn=512):
    m = h2d.shape[0]
    return pl.pallas_call(
        _ple_body,
        grid=(m // tm, D_MODEL // tn),
        in_specs=[
            pl.BlockSpec((tm, D_MODEL), lambda i, j: (i, 0)),
            pl.BlockSpec((1, D_MODEL), lambda i, j: (0, 0)),
            pl.BlockSpec((None, D_MODEL, tn), lambda i, j: (layer, 0, j)),
            pl.BlockSpec((None, tm, PLE_DIM), lambda i, j: (layer, i, 0)),
            pl.BlockSpec((None, PLE_DIM, tn), lambda i, j: (layer, 0, j)),
            pl.BlockSpec((tm, tn), lambda i, j: (i, j)),
        ],
        out_specs=pl.BlockSpec((tm, tn), lambda i, j: (i, j)),
        out_shape=jax.ShapeDtypeStruct((m, D_MODEL), F32),
        scratch_shapes=[pltpu.VMEM((tm, D_MODEL), BF16)],
        compiler_params=_params("arbitrary", "arbitrary"),
        name="ple",
    )(h2d, gain, w_gate, p2d, w_ple, h2d)


def _final_body(h_ref, g_ref, o_ref):
    o_ref[...] = _rms(h_ref[...], g_ref[...])


def _final_norm(h2d, gain, tm=256):
    m = h2d.shape[0]
    return pl.pallas_call(
        _final_body,
        grid=(m // tm,),
        in_specs=[pl.BlockSpec((tm, D_MODEL), lambda i: (i, 0)), pl.BlockSpec((1, D_MODEL), lambda i: (0, 0))],
        out_specs=pl.BlockSpec((tm, D_MODEL), lambda i: (i, 0)),
        out_shape=jax.ShapeDtypeStruct((m, D_MODEL), F32),
        compiler_params=_params("arbitrary"),
        name="final_norm",
    )(h2d, gain)


def _rope_cs(pos, dim, theta):
    inv = theta ** (-jnp.arange(0, dim, 2, dtype=F32) / dim)
    ang = pos.astype(F32)[:, None] * inv[None, :]
    return jnp.cos(ang), jnp.sin(ang)


def _tables(S):
    cos, sin = _rope_cs(jnp.arange(S), ROPE_DIM, ROPE_THETA)
    z = jnp.zeros_like(sin)
    rest = HEAD_DIM - ROPE_DIM
    a_c = jnp.concatenate([cos, cos, jnp.ones((S, rest), F32)], axis=1)
    a_sp = jnp.concatenate([z, sin, jnp.zeros((S, rest), F32)], axis=1)
    a_sm = jnp.concatenate([-sin, z, jnp.zeros((S, rest), F32)], axis=1)
    rows = S // GRID_W
    rc, rs = _rope_cs(jnp.repeat(jnp.arange(rows), GRID_W), HEAD_DIM // 2, AXIAL_THETA)
    cc, cs = _rope_cs(jnp.tile(jnp.arange(GRID_W), rows), HEAD_DIM // 2, AXIAL_THETA)
    zz = jnp.zeros_like(rs)
    b_c = jnp.concatenate([rc, rc, cc, cc], axis=1)
    b_sp = jnp.concatenate([zz, rs, zz, cs], axis=1)
    b_sm = jnp.concatenate([-rs, zz, -cs, zz], axis=1)
    return (a_c, a_sp, a_sm), (b_c, b_sp, b_sm)


def kernel(x, p, norm_mix, w_in, conv_w, q_norm, k_norm, out_norm_a, out_norm_b, gdn_a_log, gdn_dt_bias,
           gdn_norm, w_out, norm_moe, w_router, w_gate, w_up, w_down, norm_ple, w_ple, w_ple_gate, norm_final):
    B, S, D = x.shape
    depth = w_in.shape[0]
    assert D == D_MODEL and S % 1024 == 0
    cap = EC_CAPACITY * S // N_EXPERTS
    n_chunks = S // DELTA_CHUNK
    tabs_a, tabs_b = _tables(S)
    p2d = p.reshape(depth, B * S, PLE_DIM)
    row = lambda v: v.reshape(1, -1)

    h = x.reshape(B * S, D)
    for i in range(depth):
        w_small = jnp.pad(w_in[i, :, D_MAIN:], ((0, 0), (0, LANES - D_SMALL)))
        proj, small = _proj(h, row(norm_mix[i]), w_in, w_small, i)
        proj3 = proj.reshape(B, S, D_MAIN)

        y_a = _mix_a(proj3, tabs_a)
        y_b = _mix_b(proj3, tabs_b, row(q_norm[i]), row(k_norm[i]))

        lg = small[:, :D_SMALL].reshape(B, n_chunks, DELTA_CHUNK, 4, C_HEADS).transpose(0, 3, 4, 1, 2)
        par = jnp.concatenate([gdn_a_log[i], gdn_dt_bias[i]], axis=0)
        par = jnp.broadcast_to(par.T[:, :, None], (C_HEADS, 4, DELTA_CHUNK))
        y_c = _mix_c(proj3, conv_w, lg, par, row(gdn_norm[i]), i)

        h = _out_proj(y_a.reshape(B * S, A_W), y_b.reshape(B * S, B_QW), y_c.reshape(B * S, C_W),
                      row(out_norm_a[i]), row(out_norm_b[i]), w_out, h, i)

        hn, aff_t = _moe_pre(h.reshape(B, S, D), row(norm_moe[i]), w_router[i].T)
        rank, gates, offs = _topk(aff_t, cap)
        offs_flat = offs.reshape(-1)
        rank4 = rank.reshape(B, N_EXPERTS, S // LANES, LANES)
        gate4 = gates.reshape(B, N_EXPERTS, S // LANES, LANES)
        xs, gslot = _gather(offs_flat, hn, rank4, gate4, cap)
        y = _ffn(xs, w_gate, w_up, w_down, gslot, i)
        h = _scatter(offs_flat, h.reshape(B, S, D), y, rank4, cap).reshape(B * S, D)

        h = _ple(h, row(norm_ple[i]), w_ple_gate, p2d, w_ple, i)
    return _final_norm(h, row(norm_final)).reshape(B, S, D)
```

```python
import functools

import jax
import jax.numpy as jnp
from jax import lax
from jax.experimental import pallas as pl
from jax.experimental.pallas import tpu as pltpu

F32 = jnp.float32
BF16 = jnp.bfloat16

D_MODEL = 2048
HEAD_DIM = 128
A_HEADS = 4
B_HEADS = 8
B_KV_HEADS = 2
B_GROUP = B_HEADS // B_KV_HEADS
C_HEADS = 4
A_W = A_HEADS * HEAD_DIM
B_QW = B_HEADS * HEAD_DIM
B_KVW = B_KV_HEADS * HEAD_DIM
C_W = C_HEADS * HEAD_DIM
DILATED_PATTERNS = ((128, 1), (512, 4), (2048, 16))
ROPE_THETA = 500000.0
ROPE_DIM = HEAD_DIM // 4
AXIAL_THETA = 10000.0
GRID_W = 64
CONV_K = 5
DELTA_CHUNK = 64
N_EXPERTS = 16
EC_CAPACITY = 2
EXPERT_FF = D_MODEL // 2
PLE_DIM = 256
NORM_EPS = 1e-6
D_MAIN = 3 * A_W + B_QW + 2 * B_KVW + 4 * C_W
D_SMALL = 4 * C_HEADS

COL_AQ, COL_AK, COL_AV = 0, 4, 8
COL_BQ, COL_BK, COL_BV = 12, 20, 22
COL_CQ, COL_CK, COL_CV, COL_CZ = 24, 28, 32, 36

LANES = 128
VMEM_LIMIT = 56 * 1024 * 1024
NEG_BIG = -1e30
GROUP = 4
PRE_GROUPS = 4
A_BLOCKS = 4


def _params(*sem):
    return pltpu.CompilerParams(dimension_semantics=sem, vmem_limit_bytes=VMEM_LIMIT)


def _nt(a, b):
    return lax.dot_general(a, b, (((1,), (1,)), ((), ())), preferred_element_type=F32)


def _tn(a, b):
    return lax.dot_general(a, b, (((0,), (0,)), ((), ())), preferred_element_type=F32)


def _dot(a, b):
    return jnp.dot(a, b, preferred_element_type=F32)


def _split(x):
    hi = x.astype(BF16)
    lo = (x - hi.astype(F32)).astype(BF16)
    return hi, lo


def _dotb(a, b):
    return _dot(a.astype(BF16), b.astype(BF16))


def _dot3(a, b):
    ah, al = _split(a)
    bh, bl = _split(b)
    return _dot(ah, bh) + (_dot(ah, bl) + _dot(al, bh))


def _rms(x, gain):
    return x * lax.rsqrt(jnp.mean(x * x, axis=-1, keepdims=True) + NORM_EPS) * gain


def _sigmoid(x):
    return 1.0 / (1.0 + jnp.exp(-x))


def _silu(x):
    return x * _sigmoid(x)


def _row_chunks(n_rows, chunk, fn):
    def body(c, carry):
        fn(pl.ds(pl.multiple_of(c * chunk, chunk), chunk))
        return carry
    lax.fori_loop(0, n_rows // chunk, body, 0)


def _proj_body(x_ref, g_ref, w_ref, ws_ref, o_ref, os_ref, xn_ref):
    tm = x_ref.shape[0]

    @pl.when(pl.program_id(1) == 0)
    def _():
        def norm_rows(rows):
            xn_ref[rows, :] = _rms(x_ref[rows, :], g_ref[...]).astype(BF16)
        _row_chunks(tm, 128, norm_rows)
        os_ref[...] = _dot(xn_ref[...], ws_ref[...].astype(BF16))

    o_ref[...] = _dot(xn_ref[...], w_ref[...].astype(BF16)).astype(o_ref.dtype)


def _proj(h2d, gain, w_in, w_small, layer, tm=1024, tn=512):
    m = h2d.shape[0]
    return pl.pallas_call(
        _proj_body,
        grid=(m // tm, D_MAIN // tn),
        in_specs=[
            pl.BlockSpec((tm, D_MODEL), lambda i, j: (i, 0)),
            pl.BlockSpec((1, D_MODEL), lambda i, j: (0, 0)),
            pl.BlockSpec((None, D_MODEL, tn), lambda i, j: (layer, 0, j)),
            pl.BlockSpec((D_MODEL, LANES), lambda i, j: (0, 0)),
        ],
        out_specs=[
            pl.BlockSpec((tm, tn), lambda i, j: (i, j)),
            pl.BlockSpec((tm, LANES), lambda i, j: (i, 0)),
        ],
        out_shape=[
            jax.ShapeDtypeStruct((m, D_MAIN), BF16),
            jax.ShapeDtypeStruct((m, LANES), F32),
        ],
        scratch_shapes=[pltpu.VMEM((tm, D_MODEL), BF16)],
        compiler_params=_params("arbitrary", "arbitrary"),
        name="proj",
    )(h2d, gain, w_in, w_small)


def _mixa_body(q_ref, k_ref, v_ref, c_ref, sp_ref, sm_ref, o_ref,
               qf, kf, vf, qd, kp, vp, acc, den, mrun):
    S = q_ref.shape[0]
    half = 64
    blk = 128
    scale = HEAD_DIM ** -0.5

    def prep(rows):
        def rope(x):
            return (x * c_ref[rows, :] + pltpu.roll(x, ROPE_DIM // 2, 1) * sp_ref[rows, :]
                    + pltpu.roll(x, LANES - ROPE_DIM // 2, 1) * sm_ref[rows, :])
        qf[rows, :] = rope(q_ref[rows, :].astype(F32)) * scale
        kf[rows, :] = rope(k_ref[rows, :].astype(F32))
        vf[rows, :] = v_ref[rows, :].astype(F32)
        acc[rows, :] = jnp.zeros((512, LANES), F32)
        den[rows, :] = jnp.zeros((512, LANES), F32)
        mrun[rows, :] = jnp.full((512, LANES), NEG_BIG, F32)
    _row_chunks(S, 512, prep)

    row = lax.broadcasted_iota(jnp.int32, (blk, 2 * blk), 0)
    col = lax.broadcasted_iota(jnp.int32, (blk, 2 * blk), 1)
    in_band = jnp.abs(col - row - half) <= half

    for window, dil in DILATED_PATTERNS:
        assert window // (2 * dil) == half
        L = S // dil
        nblk = L // blk
        zk = jnp.zeros((half, LANES), BF16)
        zv = jnp.zeros((half, 2 * LANES), BF16)
        kp[0:half, :] = zk
        kp[half + L:2 * half + L, :] = zk
        vp[0:half, :] = zv
        vp[half + L:2 * half + L, :] = zv

        def per_residue(r, carry, dil=dil, L=L, nblk=nblk):
            def deinterleave(c, carry2):
                src = pl.ds(r + c * (256 * dil), 256, stride=dil) if dil > 1 else pl.ds(
                    pl.multiple_of(c * 256, 256), 256)
                dst = pl.ds(pl.multiple_of(half + c * 256, half), 256)
                qd[pl.ds(pl.multiple_of(c * 256, 256), 256), :] = qf[src, :].astype(BF16)
                kp[dst, :] = kf[src, :].astype(BF16)
                vp[dst, 0:LANES] = vf[src, :].astype(BF16)
                vp[dst, LANES:2 * LANES] = jnp.ones((256, LANES), BF16)
                return carry2
            lax.fori_loop(0, L // 256, deinterleave, 0)

            n_par = min(A_BLOCKS, nblk)

            def blocks(it, carry2):
                nbs = [it * n_par + u for u in range(n_par)]
                w0 = [pl.multiple_of(nb * blk, blk) for nb in nbs]
                s = [_nt(qd[pl.ds(w, blk), :], kp[pl.ds(w, 2 * blk), :]) for w in w0]
                kpos = [nb * blk - half + col for nb in nbs]
                s = [jnp.where(in_band & (kp_ >= 0) & (kp_ < L), s_, NEG_BIG) for s_, kp_ in zip(s, kpos)]
                m_b = [jnp.max(s_, axis=1, keepdims=True) for s_ in s]
                e = [jnp.exp(s_ - m_).astype(BF16) for s_, m_ in zip(s, m_b)]
                od = [_dot(e_, vp[pl.ds(w, 2 * blk), :]) for e_, w in zip(e, w0)]
                for nb, w, m_, od_ in zip(nbs, w0, m_b, od):
                    rows = pl.ds(r + nb * (blk * dil), blk, stride=dil) if dil > 1 else pl.ds(w, blk)
                    m_old = mrun[rows, :]
                    m_new = jnp.maximum(m_old, m_)
                    a_old = jnp.exp(m_old - m_new)
                    a_new = jnp.exp(m_ - m_new)
                    acc[rows, :] = acc[rows, :] * a_old + od_[:, 0:LANES] * a_new
                    den[rows, :] = den[rows, :] * a_old + od_[:, LANES:2 * LANES] * a_new
                    mrun[rows, :] = m_new
                return carry2
            lax.fori_loop(0, nblk // n_par, blocks, 0)
            return carry
        lax.fori_loop(0, dil, per_residue, 0)

    def finish(rows):
        o_ref[rows, :] = (acc[rows, :] / den[rows, :]).astype(o_ref.dtype)
    _row_chunks(S, 512, finish)


def _mix_a(proj3, tabs):
    B, S, _ = proj3.shape
    head = lambda base: pl.BlockSpec((None, S, LANES), lambda b, h: (b, 0, base + h))
    tab = pl.BlockSpec((S, LANES), lambda b, h: (0, 0))
    return pl.pallas_call(
        _mixa_body,
        grid=(B, A_HEADS),
        in_specs=[head(COL_AQ), head(COL_AK), head(COL_AV), tab, tab, tab],
        out_specs=pl.BlockSpec((None, S, LANES), lambda b, h: (b, 0, h)),
        out_shape=jax.ShapeDtypeStruct((B, S, A_W), BF16),
        scratch_shapes=[
            pltpu.VMEM((S, LANES), F32), pltpu.VMEM((S, LANES), F32), pltpu.VMEM((S, LANES), F32),
            pltpu.VMEM((S, LANES), BF16),
            pltpu.VMEM((S + 128, LANES), BF16),
            pltpu.VMEM((S + 128, 2 * LANES), BF16),
            pltpu.VMEM((S, LANES), F32), pltpu.VMEM((S, LANES), F32), pltpu.VMEM((S, LANES), F32),
        ],
        compiler_params=_params("arbitrary", "arbitrary"),
        name="mix_a",
    )(proj3, proj3, proj3, *tabs)


def _mixb_body(q_ref, k_ref, v_ref, c_ref, sp_ref, sm_ref, qg_ref, kg_ref, o_ref,
               kt_ref, va_ref, q4_ref, m_ref, acc_ref, *, tk):
    S = k_ref.shape[0]
    tq = q_ref.shape[0]
    qi = pl.program_id(2)
    quarter = HEAD_DIM // 4

    def rope(x, rows):
        return (x * c_ref[rows, :] + pltpu.roll(x, quarter, 1) * sp_ref[rows, :]
                + pltpu.roll(x, LANES - quarter, 1) * sm_ref[rows, :])

    @pl.when(qi == 0)
    def _():
        def kv_chunk(c, carry):
            rows = pl.ds(pl.multiple_of(c * tk, tk), tk)
            k = rope(_rms(k_ref[rows, :].astype(F32), kg_ref[...]), rows)
            kt_ref[c] = k.T.astype(BF16)
            va_ref[rows, 0:LANES] = v_ref[rows, :]
            va_ref[rows, LANES:2 * LANES] = jnp.ones((tk, LANES), BF16)
            return carry
        lax.fori_loop(0, S // tk, kv_chunk, 0)

    rows_q = pl.ds(pl.multiple_of(qi * tq, tq), tq)
    for g in range(B_GROUP):
        q = _rms(q_ref[:, g * LANES:(g + 1) * LANES].astype(F32), qg_ref[...])
        q = rope(q, rows_q) * (HEAD_DIM ** -0.5)
        q4_ref[g * tq:(g + 1) * tq, :] = q.astype(BF16)
    m_ref[...] = jnp.full(m_ref.shape, NEG_BIG, F32)
    acc_ref[...] = jnp.zeros(acc_ref.shape, F32)

    def kv_step(c, carry):
        rows = pl.ds(pl.multiple_of(c * tk, tk), tk)
        s = _dot(q4_ref[...], kt_ref[c])
        m_old = m_ref[...]
        m_new = jnp.maximum(m_old, jnp.max(s, axis=1, keepdims=True))
        alpha = jnp.exp(m_old - m_new)
        p = jnp.exp(s - pltpu.repeat(m_new, tk // LANES, axis=1))
        acc_ref[...] = acc_ref[...] * pltpu.repeat(alpha, 2, axis=1) + _dot(p.astype(BF16), va_ref[rows, :])
        m_ref[...] = m_new
        return carry
    lax.fori_loop(0, S // tk, kv_step, 0)

    o = acc_ref[:, 0:LANES] / acc_ref[:, LANES:2 * LANES]
    for g in range(B_GROUP):
        o_ref[:, g * LANES:(g + 1) * LANES] = o[g * tq:(g + 1) * tq, :].astype(o_ref.dtype)


def _mix_b(proj3, tabs, q_gain, k_gain, tq=512, tk=512):
    B, S, _ = proj3.shape
    qw = B_GROUP * LANES
    tab = pl.BlockSpec((S, LANES), lambda b, h, i: (0, 0))
    gain = pl.BlockSpec((1, LANES), lambda b, h, i: (0, 0))
    return pl.pallas_call(
        functools.partial(_mixb_body, tk=tk),
        grid=(B, B_KV_HEADS, S // tq),
        in_specs=[
            pl.BlockSpec((None, tq, qw), lambda b, h, i: (b, i, COL_BQ // B_GROUP + h)),
            pl.BlockSpec((None, S, LANES), lambda b, h, i: (b, 0, COL_BK + h)),
            pl.BlockSpec((None, S, LANES), lambda b, h, i: (b, 0, COL_BV + h)),
            tab, tab, tab, gain, gain,
        ],
        out_specs=pl.BlockSpec((None, tq, qw), lambda b, h, i: (b, i, h)),
        out_shape=jax.ShapeDtypeStruct((B, S, B_QW), BF16),
        scratch_shapes=[
            pltpu.VMEM((S // tk, LANES, tk), BF16),
            pltpu.VMEM((S, 2 * LANES), BF16),
            pltpu.VMEM((B_GROUP * tq, LANES), BF16),
            pltpu.VMEM((B_GROUP * tq, LANES), F32),
            pltpu.VMEM((B_GROUP * tq, 2 * LANES), F32),
        ],
        compiler_params=_params("arbitrary", "arbitrary", "arbitrary"),
        name="mix_b",
    )(proj3, proj3, proj3, *tabs, q_gain, k_gain)


def _softplus(x):
    return jnp.maximum(x, 0.0) + jnp.log(1.0 + jnp.exp(-jnp.abs(x)))


def _mixc_body(q_ref, k_ref, v_ref, z_ref, cwq_ref, cwk_ref, cwv_ref, lg_ref, par_ref, gn_ref, o_ref,
               xp, qn, kn, vn, tab, qp_ref, op_ref, p_ref, n_ref):
    S = q_ref.shape[0]
    C = DELTA_CHUNK
    n_chunks = S // C
    pad = 8

    for src, cw, dst, kind in ((q_ref, cwq_ref, qn, "q"), (k_ref, cwk_ref, kn, "k"), (v_ref, cwv_ref, vn, "v")):
        xp[0:pad, :] = jnp.zeros((pad, LANES), F32)
        xp[pad + S:2 * pad + S, :] = jnp.zeros((pad, LANES), F32)

        def load(rows, src=src):
            xp[pl.ds(pl.multiple_of(rows.start + pad, pad), 512), :] = src[rows, :].astype(F32)
        _row_chunks(S, 512, load)
        for c in range(S // 512):
            base = pad - CONV_K // 2 + c * 512
            y = xp[base:base + 512, :] * cw[0:1, :]
            for j in range(1, CONV_K):
                y = y + xp[base + j:base + j + 512, :] * cw[j:j + 1, :]
            y = _silu(y)
            if kind != "v":
                y = y * lax.rsqrt(jnp.sum(y * y, axis=-1, keepdims=True) + NORM_EPS)
            if kind == "q":
                y = y * (HEAD_DIM ** -0.5)
            dst[c * 512:(c + 1) * 512, :] = y

    W = GROUP * C
    n_groups = S // W
    ri = lax.broadcasted_iota(jnp.int32, (W, W), 0)
    ci = lax.broadcasted_iota(jnp.int32, (W, W), 1)
    eye = ri == ci
    eye_f = jnp.where(eye, 1.0, 0.0)
    same_block = [jnp.right_shift(ri, s) == jnp.right_shift(ci, s) for s in (3, 4, 5, 6)]
    same_chunk = same_block[-1]
    chunk_start = jnp.right_shift(ri, 6) * C
    stack_mask = (jnp.right_shift(lax.broadcasted_iota(jnp.int32, (GROUP * HEAD_DIM, W), 0), 7)
                  == jnp.right_shift(lax.broadcasted_iota(jnp.int32, (GROUP * HEAD_DIM, W), 1), 6))
    lane_w = lax.broadcasted_iota(jnp.int32, (1, W), 1)
    for d in range(2):
        g = -jnp.exp(par_ref[d:d + 1, :]) * _softplus(lg_ref[2 + d] + par_ref[2 + d:3 + d, :])
        cum = same_chunk & ((ri <= ci) if d == 0 else (ri >= ci))
        tab[d] = _dot3(g, jnp.where(cum, 1.0, 0.0))
        tab[2 + d] = _sigmoid(lg_ref[d])

    def groups_pre(chains):
        each = lambda fn, *lists: [fn(*args) for args in zip(*lists)]
        ds = [d for _, d in chains]
        rows = [pl.ds(pl.multiple_of(i * W, W), W) for i, _ in chains]
        q = [qn[r, :] for r in rows]
        k = [kn[r, :] for r in rows]
        v = [vn[r, :] for r in rows]
        gr = [tab[d, pl.ds(i, 1), :] for i, d in chains]
        br = [tab[2 + d, pl.ds(i, 1), :] for i, d in chains]
        to_col = lambda mask, r: jnp.sum(jnp.where(mask, r, 0.0), axis=1, keepdims=True)
        gcol = [to_col(eye, g) for g in gr]
        bcol = [to_col(eye, b) for b in br]
        glast = [to_col(ci == (chunk_start + (C - 1) if d == 0 else chunk_start), g) for d, g in zip(ds, gr)]
        incl = [same_chunk & ((ci <= ri) if d == 0 else (ci >= ri)) for d in ds]
        strict = [same_chunk & ((ci < ri) if d == 0 else (ci > ri)) for d in ds]
        decay = each(lambda m, gc, g: jnp.exp(jnp.where(m, gc - g, NEG_BIG)), incl, gcol, gr)
        kb = each(lambda a, b: a * b, k, bcol)
        k16 = [a.astype(BF16) for a in k]
        kk = each(lambda a, b: _nt(a.astype(BF16), b), kb, k16)
        a = each(lambda m, p, dc: jnp.where(m, p * dc, 0.0), strict, kk, decay)
        eg = [jnp.exp(g) for g in gcol]
        rhs = each(lambda vv, b, kbb, e: jnp.concatenate([vv * b, kbb * e], axis=1), v, bcol, kb, eg)
        x = [jnp.where(same_block[0], -m, 0.0) for m in a]
        s1 = [eye_f + m for m in x]
        x2 = each(_dotb, x, x)
        x2s1 = each(_dotb, x2, s1)
        x4 = each(_dotb, x2, x2)
        s2 = each(lambda p, m: p + m, s1, x2s1)
        t = each(lambda p, m, n: p + _dotb(m, n), s2, x4, s2)
        for lvl in range(1, len(same_block)):
            off = same_block[lvl] & jnp.logical_not(same_block[lvl - 1])
            y = each(lambda m, tt: _dotb(jnp.where(off, m, 0.0), tt), a, t)
            t = each(lambda tt, yy: tt - _dotb(tt, yy), t, y)
        r16 = each(lambda tt, r: _dotb(tt, r).astype(BF16), t, rhs)
        qk = each(lambda a_, b_, dc: _nt(a_.astype(BF16), b_) * dc, q, k16, decay)
        stacked = each(lambda kk_, gl, gc: jnp.where(
            stack_mask, jnp.concatenate([(kk_ * jnp.exp(gl - gc)).T] * GROUP, axis=0), 0.0).astype(BF16),
            k, glast, gcol)
        np_ = each(_dot, stacked, r16)
        qo = each(lambda m, r: _dot(m.astype(BF16), r), qk, r16)
        for j, (i, d) in enumerate(chains):
            qp_ref[d, rows[j], :] = (q[j] * eg[j] - qo[j][:, LANES:2 * LANES]).astype(BF16)
            op_ref[d, rows[j], :] = qo[j][:, 0:LANES]
            chunks = pl.ds(i * GROUP, GROUP)
            n_ref[d, chunks] = np_[j][:, 0:LANES].reshape(GROUP, HEAD_DIM, HEAD_DIM)
            p_ref[d, chunks] = np_[j][:, LANES:2 * LANES].astype(BF16).reshape(GROUP, HEAD_DIM, HEAD_DIM)

    def pre_body(i, carry):
        groups_pre([(i * PRE_GROUPS + u, d) for u in range(PRE_GROUPS) for d in range(2)])
        return carry
    lax.fori_loop(0, n_groups // PRE_GROUPS, pre_body, 0)

    def scan_body(i, carry):
        new = []
        for d, st in enumerate(carry):
            n = i if d == 0 else n_chunks - 1 - i
            rows = pl.ds(pl.multiple_of(n * C, C), C)
            gr = tab[d, pl.ds(n // GROUP, 1), :]
            target = (n % GROUP) * C + (C - 1 if d == 0 else 0)
            g_last = jnp.sum(jnp.where(lane_w == target, gr, 0.0), axis=1, keepdims=True)
            s16 = st.astype(BF16)
            op_ref[d, rows, :] = op_ref[d, rows, :] + _dot(qp_ref[d, rows, :], s16)
            new.append(st * jnp.exp(g_last) - _dot(p_ref[d, n], s16) + n_ref[d, n])
        return tuple(new)
    zero = jnp.zeros((HEAD_DIM, HEAD_DIM), F32)
    lax.fori_loop(0, n_chunks, scan_body, (zero, zero))

    def finish(rows):
        o = op_ref[0, rows, :] + op_ref[1, rows, :]
        o_ref[rows, :] = (_rms(o, gn_ref[...]) * _silu(z_ref[rows, :].astype(F32))).astype(o_ref.dtype)
    _row_chunks(S, 512, finish)


def _mix_c(proj3, conv_w, logits_rows, par, gn, layer):
    B, S, _ = proj3.shape
    n_chunks = S // DELTA_CHUNK
    width = GROUP * DELTA_CHUNK
    n_groups = S // width
    head = lambda base: pl.BlockSpec((None, S, LANES), lambda b, h: (b, 0, base + h))
    cw = lambda base: pl.BlockSpec((None, CONV_K, LANES), lambda b, h: (layer, 0, base + h))
    return pl.pallas_call(
        _mixc_body,
        grid=(B, C_HEADS),
        in_specs=[
            head(COL_CQ), head(COL_CK), head(COL_CV), head(COL_CZ),
            cw(0), cw(C_HEADS), cw(2 * C_HEADS),
            pl.BlockSpec((None, 4, None, n_groups, width), lambda b, h: (b, 0, h, 0, 0)),
            pl.BlockSpec((None, 4, width), lambda b, h: (h, 0, 0)),
            pl.BlockSpec((1, LANES), lambda b, h: (0, 0)),
        ],
        out_specs=pl.BlockSpec((None, S, LANES), lambda b, h: (b, 0, h)),
        out_shape=jax.ShapeDtypeStruct((B, S, C_W), BF16),
        scratch_shapes=[
            pltpu.VMEM((S + 16, LANES), F32),
            pltpu.VMEM((S, LANES), F32), pltpu.VMEM((S, LANES), F32), pltpu.VMEM((S, LANES), F32),
            pltpu.VMEM((4, n_groups, width), F32),
            pltpu.VMEM((2, S, LANES), BF16),
            pltpu.VMEM((2, S, LANES), F32),
            pltpu.VMEM((2, n_chunks, HEAD_DIM, HEAD_DIM), BF16),
            pltpu.VMEM((2, n_chunks, HEAD_DIM, HEAD_DIM), F32),
        ],
        compiler_params=_params("arbitrary", "arbitrary"),
        name="mix_c",
    )(proj3, proj3, proj3, proj3, conv_w, conv_w, conv_w, logits_rows, par, gn)


def _outproj_body(ya_ref, yb_ref, yc_ref, ga_ref, gb_ref, w_ref, h_ref, o_ref, yn_ref):
    tm = ya_ref.shape[0]

    @pl.when(pl.program_id(1) == 0)
    def _():
        def norm_rows(rows):
            yn_ref[rows, 0:A_W] = _rms(ya_ref[rows, :].astype(F32), ga_ref[...]).astype(BF16)
            yn_ref[rows, A_W:A_W + B_QW] = _rms(yb_ref[rows, :].astype(F32), gb_ref[...]).astype(BF16)
            yn_ref[rows, A_W + B_QW:D_MODEL] = yc_ref[rows, :]
        _row_chunks(tm, 128, norm_rows)

    o_ref[...] = h_ref[...] + _dot(yn_ref[...], w_ref[...].astype(BF16))


def _out_proj(ya, yb, yc, ga, gb, w_out, h2d, layer, tm=1024, tn=512):
    m = h2d.shape[0]
    return pl.pallas_call(
        _outproj_body,
        grid=(m // tm, D_MODEL // tn),
        in_specs=[
            pl.BlockSpec((tm, A_W), lambda i, j: (i, 0)),
            pl.BlockSpec((tm, B_QW), lambda i, j: (i, 0)),
            pl.BlockSpec((tm, C_W), lambda i, j: (i, 0)),
            pl.BlockSpec((1, A_W), lambda i, j: (0, 0)),
            pl.BlockSpec((1, B_QW), lambda i, j: (0, 0)),
            pl.BlockSpec((None, D_MODEL, tn), lambda i, j: (layer, 0, j)),
            pl.BlockSpec((tm, tn), lambda i, j: (i, j)),
        ],
        out_specs=pl.BlockSpec((tm, tn), lambda i, j: (i, j)),
        out_shape=jax.ShapeDtypeStruct((m, D_MODEL), F32),
        scratch_shapes=[pltpu.VMEM((tm, D_MODEL), BF16)],
        compiler_params=_params("arbitrary", "arbitrary"),
        name="out_proj",
    )(ya, yb, yc, ga, gb, w_out, h2d)


def _moepre_body(h_ref, g_ref, wr_ref, hn_ref, aff_ref):
    xn = _rms(h_ref[...], g_ref[...])
    hi, lo = _split(xn)
    hn_ref[...] = hi
    whi, wlo = _split(wr_ref[...])
    logits = _nt(whi, hi) + (_nt(whi, lo) + _nt(wlo, hi))
    e = jnp.exp(logits - jnp.max(logits, axis=0, keepdims=True))
    aff_ref[...] = e / jnp.sum(e, axis=0, keepdims=True)


def _moe_pre(h3, gain, w_router_t, tm=256):
    B, S, _ = h3.shape
    return pl.pallas_call(
        _moepre_body,
        grid=(B, S // tm),
        in_specs=[
            pl.BlockSpec((None, tm, D_MODEL), lambda b, i: (b, i, 0)),
            pl.BlockSpec((1, D_MODEL), lambda b, i: (0, 0)),
            pl.BlockSpec((N_EXPERTS, D_MODEL), lambda b, i: (0, 0)),
        ],
        out_specs=[
            pl.BlockSpec((None, tm, D_MODEL), lambda b, i: (b, i, 0)),
            pl.BlockSpec((None, N_EXPERTS, tm), lambda b, i: (b, 0, i)),
        ],
        out_shape=[
            jax.ShapeDtypeStruct((B, S, D_MODEL), BF16),
            jax.ShapeDtypeStruct((B, N_EXPERTS, S), F32),
        ],
        compiler_params=_params("arbitrary", "arbitrary"),
        name="moe_pre",
    )(h3, gain, w_router_t)


def _topk_body(aff_ref, rank_ref, gate_ref, offs_ref, *, cap):
    x = aff_ref[...]
    E, S = x.shape
    xb = pltpu.bitcast(x, jnp.int32)
    count = lambda mask: jnp.sum(jnp.where(mask, 1.0, 0.0), axis=1, keepdims=True)

    def value_bit(it, t):
        cand = t | jnp.left_shift(jnp.int32(1), 30 - it)
        return jnp.where(count(xb >= cand) >= cap, cand, t)
    thr = lax.fori_loop(0, 31, value_bit, jnp.zeros((E, 1), jnp.int32))

    above = xb > thr
    tied = xb == thr
    need = cap - count(above)
    idx = lax.broadcasted_iota(jnp.int32, (E, S), 1)

    def index_bit(it, j):
        cand = j | jnp.left_shift(jnp.int32(1), 11 - it)
        return jnp.where(count(tied & (idx < cand)) < need, cand, j)
    assert S == 4096
    jmax = lax.fori_loop(0, 12, index_bit, jnp.zeros((E, 1), jnp.int32))
    sel = above | (tied & (idx <= jmax) & (need > 0.0))
    gate_ref[...] = jnp.where(sel, x, 0.0)

    ri = lax.broadcasted_iota(jnp.int32, (LANES, LANES), 0)
    ci = lax.broadcasted_iota(jnp.int32, (LANES, LANES), 1)
    before = jnp.where(ri < ci, 1.0, 0.0).astype(BF16)
    lane = lax.broadcasted_iota(jnp.int32, (E, LANES), 1)
    off = jnp.zeros((E, 1), F32)
    offs = jnp.zeros((E, LANES), F32)
    self_f = jnp.where(sel, 1.0, 0.0)
    for c in range(S // LANES):
        m_c = self_f[:, c * LANES:(c + 1) * LANES]
        rank_c = _dot(m_c.astype(BF16), before)
        rank_ref[:, c * LANES:(c + 1) * LANES] = jnp.where(m_c > 0.0, rank_c + off, -1.0)
        offs = jnp.where(lane == c, off, offs)
        off = off + jnp.sum(m_c, axis=1, keepdims=True)
    offs = jnp.where(lane == S // LANES, off, offs)
    offs_ref[...] = offs.astype(jnp.int32)


def _topk(aff_t, cap):
    B, E, S = aff_t.shape
    spec = pl.BlockSpec((None, E, S), lambda b: (b, 0, 0))
    return pl.pallas_call(
        functools.partial(_topk_body, cap=cap),
        grid=(B,),
        in_specs=[spec],
        out_specs=[spec, spec, pl.BlockSpec((None, E, LANES), lambda b: (b, 0, 0))],
        out_shape=[
            jax.ShapeDtypeStruct((B, E, S), F32),
            jax.ShapeDtypeStruct((B, E, S), F32),
            jax.ShapeDtypeStruct((B, E, LANES), jnp.int32),
        ],
        compiler_params=_params("arbitrary"),
        name="topk",
    )(aff_t)


def _gather_body(offs_ref, hn_ref, rk_ref, gt_ref, xs_ref, gs_ref, acc_ref, gacc_ref, *, cap):
    S = hn_ref.shape[0]
    b = pl.program_id(0)
    e = pl.program_id(1)
    base = (b * N_EXPERTS + e) * LANES
    n_sb = cap // LANES
    acc_ref[...] = jnp.zeros(acc_ref.shape, F32)
    gacc_ref[...] = jnp.zeros(gacc_ref.shape, F32)
    slot = lax.broadcasted_iota(jnp.int32, (LANES, LANES), 0).astype(F32)

    def chunk(c, carry):
        lo = offs_ref[base + c]
        hi = offs_ref[base + c + 1]
        rk = rk_ref[pl.ds(c, 1), :]
        gt = gt_ref[pl.ds(c, 1), :]
        h_c = hn_ref[pl.ds(pl.multiple_of(c * LANES, LANES), LANES), :]
        for sb in range(n_sb):
            @pl.when((lo < (sb + 1) * LANES) & (hi > sb * LANES))
            def _():
                onehot = (slot + float(sb * LANES)) == rk
                acc_ref[sb] = acc_ref[sb] + _dot(jnp.where(onehot, 1.0, 0.0).astype(BF16), h_c)
                gacc_ref[sb] = gacc_ref[sb] + jnp.sum(jnp.where(onehot, gt, 0.0), axis=1, keepdims=True)
        return carry
    lax.fori_loop(0, S // LANES, chunk, 0)

    for sb in range(n_sb):
        xs_ref[sb * LANES:(sb + 1) * LANES, :] = acc_ref[sb].astype(xs_ref.dtype)
        gs_ref[sb * LANES:(sb + 1) * LANES, :] = gacc_ref[sb]


def _gather(offs_flat, hn3, rank4, gate4, cap):
    B, S, _ = hn3.shape
    n_chunks = S // LANES
    row = pl.BlockSpec((None, None, n_chunks, LANES), lambda b, e, offs: (b, e, 0, 0))
    return pl.pallas_call(
        functools.partial(_gather_body, cap=cap),
        grid_spec=pltpu.PrefetchScalarGridSpec(
            num_scalar_prefetch=1,
            grid=(B, N_EXPERTS),
            in_specs=[
                pl.BlockSpec((None, S, D_MODEL), lambda b, e, offs: (b, 0, 0)),
                row, row,
            ],
            out_specs=[
                pl.BlockSpec((None, None, cap, D_MODEL), lambda b, e, offs: (b, e, 0, 0)),
                pl.BlockSpec((None, None, cap, LANES), lambda b, e, offs: (b, e, 0, 0)),
            ],
            scratch_shapes=[
                pltpu.VMEM((cap // LANES, LANES, D_MODEL), F32),
                pltpu.VMEM((cap // LANES, LANES, LANES), F32),
            ],
        ),
        out_shape=[
            jax.ShapeDtypeStruct((B, N_EXPERTS, cap, D_MODEL), BF16),
            jax.ShapeDtypeStruct((B, N_EXPERTS, cap, LANES), F32),
        ],
        compiler_params=_params("arbitrary", "arbitrary"),
        name="gather",
    )(offs_flat, hn3, rank4, gate4)


def _ffn_body(xs_ref, wg_ref, wu_ref, wd_ref, gs_ref, y_ref, acc_ref):
    f = pl.program_id(1)
    n_f = pl.num_programs(1)
    wg = wg_ref[...].astype(BF16)
    wu = wu_ref[...].astype(BF16)
    wd = wd_ref[...].astype(BF16)
    for b in range(xs_ref.shape[0]):
        x = xs_ref[b]
        hg = _dot(x, wg)
        hu = _dot(x, wu)
        contrib = _dot((_silu(hg) * hu).astype(BF16), wd)

        @pl.when(f == 0)
        def _():
            acc_ref[b] = contrib

        @pl.when(f > 0)
        def _():
            acc_ref[b] = acc_ref[b] + contrib

        @pl.when(f == n_f - 1)
        def _():
            y_ref[b] = (acc_ref[b] * gs_ref[b][:, 0:1]).astype(y_ref.dtype)


def _ffn(xs, w_gate, w_up, w_down, gslot, layer, tf=256):
    B, E, cap, _ = xs.shape
    return pl.pallas_call(
        _ffn_body,
        grid=(E, EXPERT_FF // tf),
        in_specs=[
            pl.BlockSpec((B, None, cap, D_MODEL), lambda e, f: (0, e, 0, 0)),
            pl.BlockSpec((None, None, D_MODEL, tf), lambda e, f: (layer, e, 0, f)),
            pl.BlockSpec((None, None, D_MODEL, tf), lambda e, f: (layer, e, 0, f)),
            pl.BlockSpec((None, None, tf, D_MODEL), lambda e, f: (layer, e, f, 0)),
            pl.BlockSpec((B, None, cap, LANES), lambda e, f: (0, e, 0, 0)),
        ],
        out_specs=pl.BlockSpec((B, None, cap, D_MODEL), lambda e, f: (0, e, 0, 0)),
        out_shape=jax.ShapeDtypeStruct((B, E, cap, D_MODEL), BF16),
        scratch_shapes=[pltpu.VMEM((B, cap, D_MODEL), F32)],
        compiler_params=_params("arbitrary", "arbitrary"),
        name="ffn",
    )(xs, w_gate, w_up, w_down, gslot)


def _scatter_body(offs_ref, h_ref, y_ref, rk_ref, o_ref, *, cap):
    tp = h_ref.shape[0]
    b = pl.program_id(0)
    t = pl.program_id(1)
    e = pl.program_id(2)
    base = (b * N_EXPERTS + e) * LANES
    ri = lax.broadcasted_iota(jnp.int32, (LANES, LANES), 0)
    ci = lax.broadcasted_iota(jnp.int32, (LANES, LANES), 1)
    eye = ri == ci
    lane_slot = ci.astype(F32)

    @pl.when(e == 0)
    def _():
        o_ref[...] = h_ref[...]

    for cc in range(tp // LANES):
        c = t * (tp // LANES) + cc
        lo = offs_ref[base + c]
        hi = offs_ref[base + c + 1]
        for sb in range(cap // LANES):
            @pl.when((lo < (sb + 1) * LANES) & (hi > sb * LANES))
            def _():
                rk = rk_ref[pl.ds(c, 1), :]
                rk_col = jnp.sum(jnp.where(eye, rk, 0.0), axis=1, keepdims=True)
                onehot = (lane_slot + float(sb * LANES)) == rk_col
                rows = slice(cc * LANES, (cc + 1) * LANES)
                o_ref[rows, :] = o_ref[rows, :] + _dot(
                    jnp.where(onehot, 1.0, 0.0).astype(BF16), y_ref[sb * LANES:(sb + 1) * LANES, :])


def _scatter(offs_flat, h3, y, rank4, cap, tp=1024):
    B, S, _ = h3.shape
    n_chunks = S // LANES
    return pl.pallas_call(
        functools.partial(_scatter_body, cap=cap),
        grid_spec=pltpu.PrefetchScalarGridSpec(
            num_scalar_prefetch=1,
            grid=(B, S // tp, N_EXPERTS),
            in_specs=[
                pl.BlockSpec((None, tp, D_MODEL), lambda b, t, e, offs: (b, t, 0)),
                pl.BlockSpec((None, None, cap, D_MODEL), lambda b, t, e, offs: (b, e, 0, 0)),
                pl.BlockSpec((None, None, n_chunks, LANES), lambda b, t, e, offs: (b, e, 0, 0)),
            ],
            out_specs=pl.BlockSpec((None, tp, D_MODEL), lambda b, t, e, offs: (b, t, 0)),
        ),
        out_shape=jax.ShapeDtypeStruct((B, S, D_MODEL), F32),
        compiler_params=_params("arbitrary", "arbitrary", "arbitrary"),
        name="scatter",
    )(offs_flat, h3, y, rank4)


def _ple_body(hrow_ref, g_ref, wg_ref, p_ref, wp_ref, hblk_ref, o_ref, hn_ref):
    tm = hrow_ref.shape[0]

    @pl.when(pl.program_id(1) == 0)
    def _():
        def norm_rows(rows):
            hn_ref[rows, :] = _rms(hrow_ref[rows, :], g_ref[...]).astype(BF16)
        _row_chunks(tm, 128, norm_rows)

    gate = _sigmoid(_dot(hn_ref[...], wg_ref[...].astype(BF16)))
    emb = _dot(p_ref[...].astype(BF16), wp_ref[...].astype(BF16))
    o_ref[...] = hblk_ref[...] + gate * emb


def _ple(h2d, gain, w_gate, p2d, w_ple, layer, tm=1024, tn=512):
    m = h2d.shape[0]
    return pl.pallas_call(
        _ple_body,
        grid=(m // tm, D_MODEL // tn),
        in_specs=[
            pl.BlockSpec((tm, D_MODEL), lambda i, j: (i, 0)),
            pl.BlockSpec((1, D_MODEL), lambda i, j: (0, 0)),
            pl.BlockSpec((None, D_MODEL, tn), lambda i, j: (layer, 0, j)),
            pl.BlockSpec((None, tm, PLE_DIM), lambda i, j: (layer, i, 0)),
            pl.BlockSpec((None, PLE_DIM, tn), lambda i, j: (layer, 0, j)),
            pl.BlockSpec((tm, tn), lambda i, j: (i, j)),
        ],
        out_specs=pl.BlockSpec((tm, tn), lambda i, j: (i, j)),
        out_shape=jax.ShapeDtypeStruct((m, D_MODEL), F32),
        scratch_shapes=[pltpu.VMEM((tm, D_MODEL), BF16)],
        compiler_params=_params("arbitrary", "arbitrary"),
        name="ple",
    )(h2d, gain, w_gate, p2d, w_ple, h2d)


def _final_body(h_ref, g_ref, o_ref):
    o_ref[...] = _rms(h_ref[...], g_ref[...])


def _final_norm(h2d, gain, tm=256):
    m = h2d.shape[0]
    return pl.pallas_call(
        _final_body,
        grid=(m // tm,),
        in_specs=[pl.BlockSpec((tm, D_MODEL), lambda i: (i, 0)), pl.BlockSpec((1, D_MODEL), lambda i: (0, 0))],
        out_specs=pl.BlockSpec((tm, D_MODEL), lambda i: (i, 0)),
        out_shape=jax.ShapeDtypeStruct((m, D_MODEL), F32),
        compiler_params=_params("arbitrary"),
        name="final_norm",
    )(h2d, gain)


def _rope_cs(pos, dim, theta):
    inv = theta ** (-jnp.arange(0, dim, 2, dtype=F32) / dim)
    ang = pos.astype(F32)[:, None] * inv[None, :]
    return jnp.cos(ang), jnp.sin(ang)


def _tables(S):
    cos, sin = _rope_cs(jnp.arange(S), ROPE_DIM, ROPE_THETA)
    z = jnp.zeros_like(sin)
    rest = HEAD_DIM - ROPE_DIM
    a_c = jnp.concatenate([cos, cos, jnp.ones((S, rest), F32)], axis=1)
    a_sp = jnp.concatenate([z, sin, jnp.zeros((S, rest), F32)], axis=1)
    a_sm = jnp.concatenate([-sin, z, jnp.zeros((S, rest), F32)], axis=1)
    rows = S // GRID_W
    rc, rs = _rope_cs(jnp.repeat(jnp.arange(rows), GRID_W), HEAD_DIM // 2, AXIAL_THETA)
    cc, cs = _rope_cs(jnp.tile(jnp.arange(GRID_W), rows), HEAD_DIM // 2, AXIAL_THETA)
    zz = jnp.zeros_like(rs)
    b_c = jnp.concatenate([rc, rc, cc, cc], axis=1)
    b_sp = jnp.concatenate([zz, rs, zz, cs], axis=1)
    b_sm = jnp.concatenate([-rs, zz, -cs, zz], axis=1)
    return (a_c, a_sp, a_sm), (b_c, b_sp, b_sm)


def kernel(x, p, norm_mix, w_in, conv_w, q_norm, k_norm, out_norm_a, out_norm_b, gdn_a_log, gdn_dt_bias,
           gdn_norm, w_out, norm_moe, w_router, w_gate, w_up, w_down, norm_ple, w_ple, w_ple_gate, norm_final):
    B, S, D = x.shape
    depth = w_in.shape[0]
    assert D == D_MODEL and S % 1024 == 0
    cap = EC_CAPACITY * S // N_EXPERTS
    width = GROUP * DELTA_CHUNK
    tabs_a, tabs_b = _tables(S)
    p2d = p.reshape(depth, B * S, PLE_DIM)
    row = lambda v: v.reshape(1, -1)

    h = x.reshape(B * S, D)
    for i in range(depth):
        w_small = jnp.pad(w_in[i, :, D_MAIN:], ((0, 0), (0, LANES - D_SMALL)))
        proj, small = _proj(h, row(norm_mix[i]), w_in, w_small, i)
        proj3 = proj.reshape(B, S, D_MAIN)

        y_a = _mix_a(proj3, tabs_a)
        y_b = _mix_b(proj3, tabs_b, row(q_norm[i]), row(k_norm[i]))

        lg = small[:, :D_SMALL].reshape(B, S // width, width, 4, C_HEADS).transpose(0, 3, 4, 1, 2)
        par = jnp.concatenate([gdn_a_log[i], gdn_dt_bias[i]], axis=0)
        par = jnp.broadcast_to(par.T[:, :, None], (C_HEADS, 4, width))
        y_c = _mix_c(proj3, conv_w, lg, par, row(gdn_norm[i]), i)

        h = _out_proj(y_a.reshape(B * S, A_W), y_b.reshape(B * S, B_QW), y_c.reshape(B * S, C_W),
                      row(out_norm_a[i]), row(out_norm_b[i]), w_out, h, i)

        hn, aff_t = _moe_pre(h.reshape(B, S, D), row(norm_moe[i]), w_router[i].T)
        rank, gates, offs = _topk(aff_t, cap)
        offs_flat = offs.reshape(-1)
        rank4 = rank.reshape(B, N_EXPERTS, S // LANES, LANES)
        gate4 = gates.reshape(B, N_EXPERTS, S // LANES, LANES)
        xs, gslot = _gather(offs_flat, hn, rank4, gate4, cap)
        y = _ffn(xs, w_gate, w_up, w_down, gslot, i)
        h = _scatter(offs_flat, h.reshape(B, S, D), y, rank4, cap).reshape(B * S, D)

        h = _ple(h, row(norm_ple[i]), w_ple_gate, p2d, w_ple, i)
    return _final_norm(h, row(norm_final)).reshape(B, S, D)
```

```python
import functools

import jax
import jax.numpy as jnp
from jax import lax
from jax.experimental import pallas as pl
from jax.experimental.pallas import tpu as pltpu

F32 = jnp.float32
BF16 = jnp.bfloat16

D_MODEL = 2048
HEAD_DIM = 128
A_HEADS = 4
B_HEADS = 8
B_KV_HEADS = 2
B_GROUP = B_HEADS // B_KV_HEADS
C_HEADS = 4
A_W = A_HEADS * HEAD_DIM
B_QW = B_HEADS * HEAD_DIM
B_KVW = B_KV_HEADS * HEAD_DIM
C_W = C_HEADS * HEAD_DIM
DILATED_PATTERNS = ((128, 1), (512, 4), (2048, 16))
ROPE_THETA = 500000.0
ROPE_DIM = HEAD_DIM // 4
AXIAL_THETA = 10000.0
GRID_W = 64
CONV_K = 5
DELTA_CHUNK = 64
N_EXPERTS = 16
EC_CAPACITY = 2
EXPERT_FF = D_MODEL // 2
PLE_DIM = 256
NORM_EPS = 1e-6
D_MAIN = 3 * A_W + B_QW + 2 * B_KVW + 4 * C_W
D_SMALL = 4 * C_HEADS

COL_AQ, COL_AK, COL_AV = 0, 4, 8
COL_BQ, COL_BK, COL_BV = 12, 20, 22
COL_CQ, COL_CK, COL_CV, COL_CZ = 24, 28, 32, 36

LANES = 128
VMEM_LIMIT = 56 * 1024 * 1024
NEG_BIG = -1e30
GROUP = 4
PRE_GROUPS = 4
A_BLOCKS = 4
SLOT_WIN = 48


def _params(*sem):
    return pltpu.CompilerParams(dimension_semantics=sem, vmem_limit_bytes=VMEM_LIMIT)


def _nt(a, b):
    return lax.dot_general(a, b, (((1,), (1,)), ((), ())), preferred_element_type=F32)


def _tn(a, b):
    return lax.dot_general(a, b, (((0,), (0,)), ((), ())), preferred_element_type=F32)


def _dot(a, b):
    return jnp.dot(a, b, preferred_element_type=F32)


def _split(x):
    hi = x.astype(BF16)
    lo = (x - hi.astype(F32)).astype(BF16)
    return hi, lo


def _dotb(a, b):
    return _dot(a.astype(BF16), b.astype(BF16))


def _dot3(a, b):
    ah, al = _split(a)
    bh, bl = _split(b)
    return _dot(ah, bh) + (_dot(ah, bl) + _dot(al, bh))


def _rms(x, gain):
    return x * lax.rsqrt(jnp.mean(x * x, axis=-1, keepdims=True) + NORM_EPS) * gain


def _sigmoid(x):
    return 1.0 / (1.0 + jnp.exp(-x))


def _silu(x):
    return x * _sigmoid(x)


def _row_chunks(n_rows, chunk, fn):
    def body(c, carry):
        fn(pl.ds(pl.multiple_of(c * chunk, chunk), chunk))
        return carry
    lax.fori_loop(0, n_rows // chunk, body, 0)


def _proj_body(x_ref, g_ref, w_ref, ws_ref, o_ref, os_ref, xn_ref):
    tm = x_ref.shape[0]

    @pl.when(pl.program_id(1) == 0)
    def _():
        def norm_rows(rows):
            xn_ref[rows, :] = _rms(x_ref[rows, :], g_ref[...]).astype(BF16)
        _row_chunks(tm, 128, norm_rows)
        lane = lax.broadcasted_iota(jnp.int32, ws_ref.shape, 1)
        os_ref[...] = _dot(xn_ref[...], jnp.where(lane < D_SMALL, ws_ref[...], 0.0).astype(BF16))

    o_ref[...] = _dot(xn_ref[...], w_ref[...].astype(BF16)).astype(o_ref.dtype)


def _proj(h2d, gain, w_in, layer, tm=1024, tn=512):
    m = h2d.shape[0]
    return pl.pallas_call(
        _proj_body,
        grid=(m // tm, D_MAIN // tn),
        in_specs=[
            pl.BlockSpec((tm, D_MODEL), lambda i, j: (i, 0)),
            pl.BlockSpec((1, D_MODEL), lambda i, j: (0, 0)),
            pl.BlockSpec((None, D_MODEL, tn), lambda i, j: (layer, 0, j)),
            pl.BlockSpec((None, D_MODEL, LANES), lambda i, j: (layer, 0, D_MAIN // LANES)),
        ],
        out_specs=[
            pl.BlockSpec((tm, tn), lambda i, j: (i, j)),
            pl.BlockSpec((tm, LANES), lambda i, j: (i, 0)),
        ],
        out_shape=[
            jax.ShapeDtypeStruct((m, D_MAIN), BF16),
            jax.ShapeDtypeStruct((m, LANES), F32),
        ],
        scratch_shapes=[pltpu.VMEM((tm, D_MODEL), BF16)],
        compiler_params=_params("arbitrary", "arbitrary"),
        name="proj",
    )(h2d, gain, w_in, w_in)


def _mixa_body(q_ref, k_ref, v_ref, c_ref, sp_ref, sm_ref, o_ref,
               qf, kf, vf, qd, kp, vp, acc, den, mrun):
    S = q_ref.shape[0]
    half = 64
    blk = 128
    scale = HEAD_DIM ** -0.5

    def prep(rows):
        def rope(x):
            return (x * c_ref[rows, :] + pltpu.roll(x, ROPE_DIM // 2, 1) * sp_ref[rows, :]
                    + pltpu.roll(x, LANES - ROPE_DIM // 2, 1) * sm_ref[rows, :])
        qf[rows, :] = rope(q_ref[rows, :].astype(F32)) * scale
        kf[rows, :] = rope(k_ref[rows, :].astype(F32))
        vf[rows, :] = v_ref[rows, :].astype(F32)
        acc[rows, :] = jnp.zeros((512, LANES), F32)
        den[rows, :] = jnp.zeros((512, LANES), F32)
        mrun[rows, :] = jnp.full((512, LANES), NEG_BIG, F32)
    _row_chunks(S, 512, prep)

    row = lax.broadcasted_iota(jnp.int32, (blk, 2 * blk), 0)
    col = lax.broadcasted_iota(jnp.int32, (blk, 2 * blk), 1)
    in_band = jnp.abs(col - row - half) <= half

    for window, dil in DILATED_PATTERNS:
        assert window // (2 * dil) == half
        L = S // dil
        nblk = L // blk
        zk = jnp.zeros((half, LANES), BF16)
        zv = jnp.zeros((half, 2 * LANES), BF16)
        kp[0:half, :] = zk
        kp[half + L:2 * half + L, :] = zk
        vp[0:half, :] = zv
        vp[half + L:2 * half + L, :] = zv

        def per_residue(r, carry, dil=dil, L=L, nblk=nblk):
            def deinterleave(c, carry2):
                src = pl.ds(r + c * (256 * dil), 256, stride=dil) if dil > 1 else pl.ds(
                    pl.multiple_of(c * 256, 256), 256)
                dst = pl.ds(pl.multiple_of(half + c * 256, half), 256)
                qd[pl.ds(pl.multiple_of(c * 256, 256), 256), :] = qf[src, :].astype(BF16)
                kp[dst, :] = kf[src, :].astype(BF16)
                vp[dst, 0:LANES] = vf[src, :].astype(BF16)
                vp[dst, LANES:2 * LANES] = jnp.ones((256, LANES), BF16)
                return carry2
            lax.fori_loop(0, L // 256, deinterleave, 0)

            n_par = min(A_BLOCKS, nblk)

            def blocks(it, carry2):
                nbs = [it * n_par + u for u in range(n_par)]
                w0 = [pl.multiple_of(nb * blk, blk) for nb in nbs]
                s = [_nt(qd[pl.ds(w, blk), :], kp[pl.ds(w, 2 * blk), :]) for w in w0]
                kpos = [nb * blk - half + col for nb in nbs]
                s = [jnp.where(in_band & (kp_ >= 0) & (kp_ < L), s_, NEG_BIG) for s_, kp_ in zip(s, kpos)]
                m_b = [jnp.max(s_, axis=1, keepdims=True) for s_ in s]
                e = [jnp.exp(s_ - m_).astype(BF16) for s_, m_ in zip(s, m_b)]
                od = [_dot(e_, vp[pl.ds(w, 2 * blk), :]) for e_, w in zip(e, w0)]
                for nb, w, m_, od_ in zip(nbs, w0, m_b, od):
                    rows = pl.ds(r + nb * (blk * dil), blk, stride=dil) if dil > 1 else pl.ds(w, blk)
                    m_old = mrun[rows, :]
                    m_new = jnp.maximum(m_old, m_)
                    a_old = jnp.exp(m_old - m_new)
                    a_new = jnp.exp(m_ - m_new)
                    acc[rows, :] = acc[rows, :] * a_old + od_[:, 0:LANES] * a_new
                    den[rows, :] = den[rows, :] * a_old + od_[:, LANES:2 * LANES] * a_new
                    mrun[rows, :] = m_new
                return carry2
            lax.fori_loop(0, nblk // n_par, blocks, 0)
            return carry
        lax.fori_loop(0, dil, per_residue, 0)

    def finish(rows):
        o_ref[rows, :] = (acc[rows, :] / den[rows, :]).astype(o_ref.dtype)
    _row_chunks(S, 512, finish)


def _mix_a(proj3, tabs):
    B, S, _ = proj3.shape
    head = lambda base: pl.BlockSpec((None, S, LANES), lambda b, h: (b, 0, base + h))
    tab = pl.BlockSpec((S, LANES), lambda b, h: (0, 0))
    return pl.pallas_call(
        _mixa_body,
        grid=(B, A_HEADS),
        in_specs=[head(COL_AQ), head(COL_AK), head(COL_AV), tab, tab, tab],
        out_specs=pl.BlockSpec((None, S, LANES), lambda b, h: (b, 0, h)),
        out_shape=jax.ShapeDtypeStruct((B, S, A_W), BF16),
        scratch_shapes=[
            pltpu.VMEM((S, LANES), F32), pltpu.VMEM((S, LANES), F32), pltpu.VMEM((S, LANES), F32),
            pltpu.VMEM((S, LANES), BF16),
            pltpu.VMEM((S + 128, LANES), BF16),
            pltpu.VMEM((S + 128, 2 * LANES), BF16),
            pltpu.VMEM((S, LANES), F32), pltpu.VMEM((S, LANES), F32), pltpu.VMEM((S, LANES), F32),
        ],
        compiler_params=_params("arbitrary", "arbitrary"),
        name="mix_a",
    )(proj3, proj3, proj3, *tabs)


def _mixb_body(q_ref, k_ref, v_ref, c_ref, sp_ref, sm_ref, qg_ref, kg_ref, o_ref,
               kt_ref, va_ref, q4_ref, m_ref, acc_ref, *, tk):
    S = k_ref.shape[0]
    tq = q_ref.shape[0]
    qi = pl.program_id(2)
    quarter = HEAD_DIM // 4

    def rope(x, rows):
        return (x * c_ref[rows, :] + pltpu.roll(x, quarter, 1) * sp_ref[rows, :]
                + pltpu.roll(x, LANES - quarter, 1) * sm_ref[rows, :])

    @pl.when(qi == 0)
    def _():
        def kv_chunk(c, carry):
            rows = pl.ds(pl.multiple_of(c * tk, tk), tk)
            k = rope(_rms(k_ref[rows, :].astype(F32), kg_ref[...]), rows)
            kt_ref[c] = k.T.astype(BF16)
            va_ref[rows, 0:LANES] = v_ref[rows, :]
            va_ref[rows, LANES:2 * LANES] = jnp.ones((tk, LANES), BF16)
            return carry
        lax.fori_loop(0, S // tk, kv_chunk, 0)

    rows_q = pl.ds(pl.multiple_of(qi * tq, tq), tq)
    for g in range(B_GROUP):
        q = _rms(q_ref[:, g * LANES:(g + 1) * LANES].astype(F32), qg_ref[...])
        q = rope(q, rows_q) * (HEAD_DIM ** -0.5)
        q4_ref[g * tq:(g + 1) * tq, :] = q.astype(BF16)
    m_ref[...] = jnp.full(m_ref.shape, NEG_BIG, F32)
    acc_ref[...] = jnp.zeros(acc_ref.shape, F32)

    def kv_step(c, carry):
        rows = pl.ds(pl.multiple_of(c * tk, tk), tk)
        s = _dot(q4_ref[...], kt_ref[c])
        m_old = m_ref[...]
        m_new = jnp.maximum(m_old, jnp.max(s, axis=1, keepdims=True))
        alpha = jnp.exp(m_old - m_new)
        p = jnp.exp(s - jnp.concatenate([m_new] * (tk // LANES), axis=1))
        acc_ref[...] = acc_ref[...] * jnp.concatenate([alpha, alpha], axis=1) + _dot(p.astype(BF16), va_ref[rows, :])
        m_ref[...] = m_new
        return carry
    lax.fori_loop(0, S // tk, kv_step, 0)

    o = acc_ref[:, 0:LANES] / acc_ref[:, LANES:2 * LANES]
    for g in range(B_GROUP):
        o_ref[:, g * LANES:(g + 1) * LANES] = o[g * tq:(g + 1) * tq, :].astype(o_ref.dtype)


def _mix_b(proj3, tabs, q_gain, k_gain, tq=512, tk=512):
    B, S, _ = proj3.shape
    qw = B_GROUP * LANES
    tab = pl.BlockSpec((S, LANES), lambda b, h, i: (0, 0))
    gain = pl.BlockSpec((1, LANES), lambda b, h, i: (0, 0))
    return pl.pallas_call(
        functools.partial(_mixb_body, tk=tk),
        grid=(B, B_KV_HEADS, S // tq),
        in_specs=[
            pl.BlockSpec((None, tq, qw), lambda b, h, i: (b, i, COL_BQ // B_GROUP + h)),
            pl.BlockSpec((None, S, LANES), lambda b, h, i: (b, 0, COL_BK + h)),
            pl.BlockSpec((None, S, LANES), lambda b, h, i: (b, 0, COL_BV + h)),
            tab, tab, tab, gain, gain,
        ],
        out_specs=pl.BlockSpec((None, tq, qw), lambda b, h, i: (b, i, h)),
        out_shape=jax.ShapeDtypeStruct((B, S, B_QW), BF16),
        scratch_shapes=[
            pltpu.VMEM((S // tk, LANES, tk), BF16),
            pltpu.VMEM((S, 2 * LANES), BF16),
            pltpu.VMEM((B_GROUP * tq, LANES), BF16),
            pltpu.VMEM((B_GROUP * tq, LANES), F32),
            pltpu.VMEM((B_GROUP * tq, 2 * LANES), F32),
        ],
        compiler_params=_params("arbitrary", "arbitrary", "arbitrary"),
        name="mix_b",
    )(proj3, proj3, proj3, *tabs, q_gain, k_gain)


def _softplus(x):
    return jnp.maximum(x, 0.0) + jnp.log(1.0 + jnp.exp(-jnp.abs(x)))


def _mixc_body(q_ref, k_ref, v_ref, z_ref, cwq_ref, cwk_ref, cwv_ref, lg_ref, par_ref, gn_ref, o_ref,
               xp, qn, kn, vn, tab, qp_ref, op_ref, p_ref, n_ref):
    S = q_ref.shape[0]
    C = DELTA_CHUNK
    n_chunks = S // C
    pad = 8

    for src, cw, dst, kind in ((q_ref, cwq_ref, qn, "q"), (k_ref, cwk_ref, kn, "k"), (v_ref, cwv_ref, vn, "v")):
        xp[0:pad, :] = jnp.zeros((pad, LANES), F32)
        xp[pad + S:2 * pad + S, :] = jnp.zeros((pad, LANES), F32)

        def load(rows, src=src):
            xp[pl.ds(pl.multiple_of(rows.start + pad, pad), 512), :] = src[rows, :].astype(F32)
        _row_chunks(S, 512, load)
        for c in range(S // 512):
            base = pad - CONV_K // 2 + c * 512
            y = xp[base:base + 512, :] * cw[0:1, :]
            for j in range(1, CONV_K):
                y = y + xp[base + j:base + j + 512, :] * cw[j:j + 1, :]
            y = _silu(y)
            if kind != "v":
                y = y * lax.rsqrt(jnp.sum(y * y, axis=-1, keepdims=True) + NORM_EPS)
            if kind == "q":
                y = y * (HEAD_DIM ** -0.5)
            dst[c * 512:(c + 1) * 512, :] = y

    W = GROUP * C
    n_groups = S // W
    ri = lax.broadcasted_iota(jnp.int32, (W, W), 0)
    ci = lax.broadcasted_iota(jnp.int32, (W, W), 1)
    eye = ri == ci
    eye_f = jnp.where(eye, 1.0, 0.0)
    same_block = [jnp.right_shift(ri, s) == jnp.right_shift(ci, s) for s in (3, 4, 5, 6)]
    same_chunk = same_block[-1]
    chunk_start = jnp.right_shift(ri, 6) * C
    stack_mask = (jnp.right_shift(lax.broadcasted_iota(jnp.int32, (GROUP * HEAD_DIM, W), 0), 7)
                  == jnp.right_shift(lax.broadcasted_iota(jnp.int32, (GROUP * HEAD_DIM, W), 1), 6))
    lane_w = lax.broadcasted_iota(jnp.int32, (1, W), 1)
    for d in range(2):
        g = -jnp.exp(par_ref[d:d + 1, :]) * _softplus(lg_ref[2 + d] + par_ref[2 + d:3 + d, :])
        cum = same_chunk & ((ri <= ci) if d == 0 else (ri >= ci))
        tab[d] = _dot3(g, jnp.where(cum, 1.0, 0.0))
        tab[2 + d] = _sigmoid(lg_ref[d])

    def groups_pre(chains):
        each = lambda fn, *lists: [fn(*args) for args in zip(*lists)]
        ds = [d for _, d in chains]
        rows = [pl.ds(pl.multiple_of(i * W, W), W) for i, _ in chains]
        q = [qn[r, :] for r in rows]
        k = [kn[r, :] for r in rows]
        v = [vn[r, :] for r in rows]
        gr = [tab[d, pl.ds(i, 1), :] for i, d in chains]
        br = [tab[2 + d, pl.ds(i, 1), :] for i, d in chains]
        to_col = lambda mask, r: jnp.sum(jnp.where(mask, r, 0.0), axis=1, keepdims=True)
        gcol = [to_col(eye, g) for g in gr]
        bcol = [to_col(eye, b) for b in br]
        glast = [to_col(ci == (chunk_start + (C - 1) if d == 0 else chunk_start), g) for d, g in zip(ds, gr)]
        incl = [same_chunk & ((ci <= ri) if d == 0 else (ci >= ri)) for d in ds]
        strict = [same_chunk & ((ci < ri) if d == 0 else (ci > ri)) for d in ds]
        decay = each(lambda m, gc, g: jnp.exp(jnp.where(m, gc - g, NEG_BIG)), incl, gcol, gr)
        kb = each(lambda a, b: a * b, k, bcol)
        k16 = [a.astype(BF16) for a in k]
        kk = each(lambda a, b: _nt(a.astype(BF16), b), kb, k16)
        a = each(lambda m, p, dc: jnp.where(m, p * dc, 0.0), strict, kk, decay)
        eg = [jnp.exp(g) for g in gcol]
        rhs = each(lambda vv, b, kbb, e: jnp.concatenate([vv * b, kbb * e], axis=1), v, bcol, kb, eg)
        x = [jnp.where(same_block[0], -m, 0.0) for m in a]
        s1 = [eye_f + m for m in x]
        x2 = each(_dotb, x, x)
        x2s1 = each(_dotb, x2, s1)
        x4 = each(_dotb, x2, x2)
        s2 = each(lambda p, m: p + m, s1, x2s1)
        t = each(lambda p, m, n: p + _dotb(m, n), s2, x4, s2)
        for lvl in range(1, len(same_block)):
            off = same_block[lvl] & jnp.logical_not(same_block[lvl - 1])
            y = each(lambda m, tt: _dotb(jnp.where(off, m, 0.0), tt), a, t)
            t = each(lambda tt, yy: tt - _dotb(tt, yy), t, y)
        r16 = each(lambda tt, r: _dotb(tt, r).astype(BF16), t, rhs)
        qk = each(lambda a_, b_, dc: _nt(a_.astype(BF16), b_) * dc, q, k16, decay)
        stacked = each(lambda kk_, gl, gc: jnp.where(
            stack_mask, jnp.concatenate([(kk_ * jnp.exp(gl - gc)).T] * GROUP, axis=0), 0.0).astype(BF16),
            k, glast, gcol)
        np_ = each(_dot, stacked, r16)
        qo = each(lambda m, r: _dot(m.astype(BF16), r), qk, r16)
        for j, (i, d) in enumerate(chains):
            qp_ref[d, rows[j], :] = (q[j] * eg[j] - qo[j][:, LANES:2 * LANES]).astype(BF16)
            op_ref[d, rows[j], :] = qo[j][:, 0:LANES]
            chunks = pl.ds(i * GROUP, GROUP)
            n_ref[d, chunks] = np_[j][:, 0:LANES].reshape(GROUP, HEAD_DIM, HEAD_DIM)
            p_ref[d, chunks] = np_[j][:, LANES:2 * LANES].astype(BF16).reshape(GROUP, HEAD_DIM, HEAD_DIM)

    def pre_body(i, carry):
        groups_pre([(i * PRE_GROUPS + u, d) for u in range(PRE_GROUPS) for d in range(2)])
        return carry
    lax.fori_loop(0, n_groups // PRE_GROUPS, pre_body, 0)

    def scan_body(i, carry):
        new = []
        for d, st in enumerate(carry):
            n = i if d == 0 else n_chunks - 1 - i
            rows = pl.ds(pl.multiple_of(n * C, C), C)
            gr = tab[d, pl.ds(n // GROUP, 1), :]
            target = (n % GROUP) * C + (C - 1 if d == 0 else 0)
            g_last = jnp.sum(jnp.where(lane_w == target, gr, 0.0), axis=1, keepdims=True)
            s16 = st.astype(BF16)
            op_ref[d, rows, :] = op_ref[d, rows, :] + _dot(qp_ref[d, rows, :], s16)
            new.append(st * jnp.exp(g_last) - _dot(p_ref[d, n], s16) + n_ref[d, n])
        return tuple(new)
    zero = jnp.zeros((HEAD_DIM, HEAD_DIM), F32)
    lax.fori_loop(0, n_chunks, scan_body, (zero, zero))

    def finish(rows):
        o = op_ref[0, rows, :] + op_ref[1, rows, :]
        o_ref[rows, :] = (_rms(o, gn_ref[...]) * _silu(z_ref[rows, :].astype(F32))).astype(o_ref.dtype)
    _row_chunks(S, 512, finish)


def _mix_c(proj3, conv_w, logits_rows, par, gn, layer):
    B, S, _ = proj3.shape
    n_chunks = S // DELTA_CHUNK
    width = GROUP * DELTA_CHUNK
    n_groups = S // width
    head = lambda base: pl.BlockSpec((None, S, LANES), lambda b, h: (b, 0, base + h))
    cw = lambda base: pl.BlockSpec((None, CONV_K, LANES), lambda b, h: (layer, 0, base + h))
    return pl.pallas_call(
        _mixc_body,
        grid=(B, C_HEADS),
        in_specs=[
            head(COL_CQ), head(COL_CK), head(COL_CV), head(COL_CZ),
            cw(0), cw(C_HEADS), cw(2 * C_HEADS),
            pl.BlockSpec((None, 4, None, n_groups, width), lambda b, h: (b, 0, h, 0, 0)),
            pl.BlockSpec((None, 4, width), lambda b, h: (h, 0, 0)),
            pl.BlockSpec((1, LANES), lambda b, h: (0, 0)),
        ],
        out_specs=pl.BlockSpec((None, S, LANES), lambda b, h: (b, 0, h)),
        out_shape=jax.ShapeDtypeStruct((B, S, C_W), BF16),
        scratch_shapes=[
            pltpu.VMEM((S + 16, LANES), F32),
            pltpu.VMEM((S, LANES), F32), pltpu.VMEM((S, LANES), F32), pltpu.VMEM((S, LANES), F32),
            pltpu.VMEM((4, n_groups, width), F32),
            pltpu.VMEM((2, S, LANES), BF16),
            pltpu.VMEM((2, S, LANES), F32),
            pltpu.VMEM((2, n_chunks, HEAD_DIM, HEAD_DIM), BF16),
            pltpu.VMEM((2, n_chunks, HEAD_DIM, HEAD_DIM), F32),
        ],
        compiler_params=_params("arbitrary", "arbitrary"),
        name="mix_c",
    )(proj3, proj3, proj3, proj3, conv_w, conv_w, conv_w, logits_rows, par, gn)


def _outproj_body(ya_ref, yb_ref, yc_ref, ga_ref, gb_ref, w_ref, h_ref, o_ref, yn_ref):
    tm = ya_ref.shape[0]

    @pl.when(pl.program_id(1) == 0)
    def _():
        def norm_rows(rows):
            yn_ref[rows, 0:A_W] = _rms(ya_ref[rows, :].astype(F32), ga_ref[...]).astype(BF16)
            yn_ref[rows, A_W:A_W + B_QW] = _rms(yb_ref[rows, :].astype(F32), gb_ref[...]).astype(BF16)
            yn_ref[rows, A_W + B_QW:D_MODEL] = yc_ref[rows, :]
        _row_chunks(tm, 128, norm_rows)

    o_ref[...] = h_ref[...] + _dot(yn_ref[...], w_ref[...].astype(BF16))


def _out_proj(ya, yb, yc, ga, gb, w_out, h2d, layer, tm=1024, tn=512):
    m = h2d.shape[0]
    return pl.pallas_call(
        _outproj_body,
        grid=(m // tm, D_MODEL // tn),
        in_specs=[
            pl.BlockSpec((tm, A_W), lambda i, j: (i, 0)),
            pl.BlockSpec((tm, B_QW), lambda i, j: (i, 0)),
            pl.BlockSpec((tm, C_W), lambda i, j: (i, 0)),
            pl.BlockSpec((1, A_W), lambda i, j: (0, 0)),
            pl.BlockSpec((1, B_QW), lambda i, j: (0, 0)),
            pl.BlockSpec((None, D_MODEL, tn), lambda i, j: (layer, 0, j)),
            pl.BlockSpec((tm, tn), lambda i, j: (i, j)),
        ],
        out_specs=pl.BlockSpec((tm, tn), lambda i, j: (i, j)),
        out_shape=jax.ShapeDtypeStruct((m, D_MODEL), F32),
        scratch_shapes=[pltpu.VMEM((tm, D_MODEL), BF16)],
        compiler_params=_params("arbitrary", "arbitrary"),
        name="out_proj",
    )(ya, yb, yc, ga, gb, w_out, h2d)


def _moepre_body(h_ref, g_ref, wr_ref, hn_ref, aff_ref):
    xn = _rms(h_ref[...], g_ref[...])
    hi, lo = _split(xn)
    hn_ref[...] = hi
    whi, wlo = _split(wr_ref[...])
    logits = _nt(whi, hi) + (_nt(whi, lo) + _nt(wlo, hi))
    e = jnp.exp(logits - jnp.max(logits, axis=0, keepdims=True))
    aff_ref[...] = e / jnp.sum(e, axis=0, keepdims=True)


def _moe_pre(h3, gain, w_router_t, tm=256):
    B, S, _ = h3.shape
    return pl.pallas_call(
        _moepre_body,
        grid=(B, S // tm),
        in_specs=[
            pl.BlockSpec((None, tm, D_MODEL), lambda b, i: (b, i, 0)),
            pl.BlockSpec((1, D_MODEL), lambda b, i: (0, 0)),
            pl.BlockSpec((N_EXPERTS, D_MODEL), lambda b, i: (0, 0)),
        ],
        out_specs=[
            pl.BlockSpec((None, tm, D_MODEL), lambda b, i: (b, i, 0)),
            pl.BlockSpec((None, N_EXPERTS, tm), lambda b, i: (b, 0, i)),
        ],
        out_shape=[
            jax.ShapeDtypeStruct((B, S, D_MODEL), BF16),
            jax.ShapeDtypeStruct((B, N_EXPERTS, S), F32),
        ],
        compiler_params=_params("arbitrary", "arbitrary"),
        name="moe_pre",
    )(h3, gain, w_router_t)


def _topk_body(aff_ref, rank_ref, gate_ref, offs_ref, *, cap):
    x = aff_ref[...]
    E, S = x.shape
    xb = pltpu.bitcast(x, jnp.int32)
    count = lambda mask: jnp.sum(jnp.where(mask, 1.0, 0.0), axis=1, keepdims=True)

    def value_bit(it, t):
        cand = t | jnp.left_shift(jnp.int32(1), 30 - it)
        return jnp.where(count(xb >= cand) >= cap, cand, t)
    thr = lax.fori_loop(0, 31, value_bit, jnp.zeros((E, 1), jnp.int32))

    above = xb > thr
    tied = xb == thr
    need = cap - count(above)
    idx = lax.broadcasted_iota(jnp.int32, (E, S), 1)

    def index_bit(it, j):
        cand = j | jnp.left_shift(jnp.int32(1), 11 - it)
        return jnp.where(count(tied & (idx < cand)) < need, cand, j)
    assert S == 4096
    jmax = lax.fori_loop(0, 12, index_bit, jnp.zeros((E, 1), jnp.int32))
    sel = above | (tied & (idx <= jmax) & (need > 0.0))
    gate_ref[...] = jnp.where(sel, x, 0.0)

    ri = lax.broadcasted_iota(jnp.int32, (LANES, LANES), 0)
    ci = lax.broadcasted_iota(jnp.int32, (LANES, LANES), 1)
    before = jnp.where(ri < ci, 1.0, 0.0).astype(BF16)
    lane = lax.broadcasted_iota(jnp.int32, (E, LANES), 1)
    off = jnp.zeros((E, 1), F32)
    offs = jnp.zeros((E, LANES), F32)
    self_f = jnp.where(sel, 1.0, 0.0)
    for c in range(S // LANES):
        m_c = self_f[:, c * LANES:(c + 1) * LANES]
        rank_c = _dot(m_c.astype(BF16), before)
        rank_ref[:, c * LANES:(c + 1) * LANES] = jnp.where(m_c > 0.0, rank_c + off, -1.0)
        offs = jnp.where(lane == c, off, offs)
        off = off + jnp.sum(m_c, axis=1, keepdims=True)
    offs = jnp.where(lane == S // LANES, off, offs)
    offs_ref[...] = offs.astype(jnp.int32)


def _topk(aff_t, cap):
    B, E, S = aff_t.shape
    spec = pl.BlockSpec((None, E, S), lambda b: (b, 0, 0))
    return pl.pallas_call(
        functools.partial(_topk_body, cap=cap),
        grid=(B,),
        in_specs=[spec],
        out_specs=[spec, spec, pl.BlockSpec((None, E, LANES), lambda b: (b, 0, 0))],
        out_shape=[
            jax.ShapeDtypeStruct((B, E, S), F32),
            jax.ShapeDtypeStruct((B, E, S), F32),
            jax.ShapeDtypeStruct((B, E, LANES), jnp.int32),
        ],
        compiler_params=_params("arbitrary"),
        name="topk",
    )(aff_t)


def _window_start(lo, step, cap):
    return pl.multiple_of(jnp.minimum((lo // 16) * 16 + step * SLOT_WIN, cap - SLOT_WIN), 16)


def _slot_onehot(row_f, start, first_valid, rk):
    slot = row_f + start.astype(F32)
    return (slot == rk) & (slot >= first_valid.astype(F32))


def _gather_body(offs_ref, hn_ref, rk_ref, gt_ref, xs_ref, gs_ref, *, cap):
    S = hn_ref.shape[0]
    b = pl.program_id(0)
    dq = pl.program_id(1)
    acc_ref = xs_ref
    gacc_ref = gs_ref

    def clear(e, carry):
        acc_ref[e] = jnp.zeros(acc_ref.shape[1:], acc_ref.dtype)

        @pl.when(dq == 0)
        def _():
            gacc_ref[e] = jnp.zeros(gacc_ref.shape[1:], F32)
        return carry
    lax.fori_loop(0, N_EXPERTS, clear, 0)

    row_f = lax.broadcasted_iota(jnp.int32, (SLOT_WIN, LANES), 0).astype(F32)

    def chunk(c, carry):
        h_c = hn_ref[pl.ds(pl.multiple_of(c * LANES, LANES), LANES), :]
        los = [offs_ref[(b * N_EXPERTS + e) * LANES + c] for e in range(N_EXPERTS)]
        his = [offs_ref[(b * N_EXPERTS + e) * LANES + c + 1] for e in range(N_EXPERTS)]
        firsts = [(lo // 16) * 16 for lo in los]
        starts = [_window_start(lo, 0, cap) for lo in los]
        hots = [_slot_onehot(row_f, starts[e], starts[e], rk_ref[e, pl.ds(c, 1), :]) for e in range(N_EXPERTS)]
        stacked = jnp.concatenate([jnp.where(h, 1.0, 0.0).astype(BF16) for h in hots], axis=0)
        res = _dot(stacked, h_c)
        for e in range(N_EXPERTS):
            win = pl.ds(starts[e], SLOT_WIN)
            acc_ref[e, win, :] = acc_ref[e, win, :] + res[e * SLOT_WIN:(e + 1) * SLOT_WIN, :].astype(acc_ref.dtype)

        @pl.when(dq == 0)
        def _():
            for e in range(N_EXPERTS):
                win = pl.ds(starts[e], SLOT_WIN)
                gacc_ref[e, win, :] = gacc_ref[e, win, :] + jnp.sum(
                    jnp.where(hots[e], gt_ref[e, pl.ds(c, 1), :], 0.0), axis=1, keepdims=True)

        overflow = his[0] - firsts[0] > SLOT_WIN
        for e in range(1, N_EXPERTS):
            overflow = overflow | (his[e] - firsts[e] > SLOT_WIN)

        @pl.when(overflow)
        def _():
            for e in range(N_EXPERTS):
                def extra(w, carry2, e=e):
                    start = _window_start(firsts[e], w, cap)
                    hot = _slot_onehot(row_f, start, firsts[e] + w * SLOT_WIN, rk_ref[e, pl.ds(c, 1), :])
                    win2 = pl.ds(start, SLOT_WIN)
                    acc_ref[e, win2, :] = acc_ref[e, win2, :] + _dot(
                        jnp.where(hot, 1.0, 0.0).astype(BF16), h_c).astype(acc_ref.dtype)

                    @pl.when(dq == 0)
                    def _():
                        gacc_ref[e, win2, :] = gacc_ref[e, win2, :] + jnp.sum(
                            jnp.where(hot, gt_ref[e, pl.ds(c, 1), :], 0.0), axis=1, keepdims=True)
                    return carry2
                lax.fori_loop(1, (his[e] - firsts[e] + SLOT_WIN - 1) // SLOT_WIN, extra, 0)
        return carry
    lax.fori_loop(0, S // LANES, chunk, 0)


def _gather(offs_flat, hn3, rank4, gate4, cap, dcols=512):
    B, S, _ = hn3.shape
    n_chunks = S // LANES
    rows = pl.BlockSpec((None, N_EXPERTS, n_chunks, LANES), lambda b, q, offs: (b, 0, 0, 0))
    return pl.pallas_call(
        functools.partial(_gather_body, cap=cap),
        grid_spec=pltpu.PrefetchScalarGridSpec(
            num_scalar_prefetch=1,
            grid=(B, D_MODEL // dcols),
            in_specs=[
                pl.BlockSpec((None, S, dcols), lambda b, q, offs: (b, 0, q)),
                rows, rows,
            ],
            out_specs=[
                pl.BlockSpec((None, N_EXPERTS, cap, dcols), lambda b, q, offs: (b, 0, 0, q)),
                pl.BlockSpec((None, N_EXPERTS, cap, LANES), lambda b, q, offs: (b, 0, 0, 0)),
            ],
        ),
        out_shape=[
            jax.ShapeDtypeStruct((B, N_EXPERTS, cap, D_MODEL), BF16),
            jax.ShapeDtypeStruct((B, N_EXPERTS, cap, LANES), F32),
        ],
        compiler_params=_params("arbitrary", "arbitrary"),
        name="gather",
    )(offs_flat, hn3, rank4, gate4)


def _ffn_body(xs_ref, wg_ref, wu_ref, wd_ref, gs_ref, y_ref, acc_ref):
    f = pl.program_id(1)
    n_f = pl.num_programs(1)
    wg = wg_ref[...].astype(BF16)
    wu = wu_ref[...].astype(BF16)
    wd = wd_ref[...].astype(BF16)
    for b in range(xs_ref.shape[0]):
        x = xs_ref[b]
        hg = _dot(x, wg)
        hu = _dot(x, wu)
        contrib = _dot((_silu(hg) * hu).astype(BF16), wd)

        @pl.when(f == 0)
        def _():
            acc_ref[b] = contrib

        @pl.when(f > 0)
        def _():
            acc_ref[b] = acc_ref[b] + contrib

        @pl.when(f == n_f - 1)
        def _():
            y_ref[b] = (acc_ref[b] * gs_ref[b][:, 0:1]).astype(y_ref.dtype)


def _ffn(xs, w_gate, w_up, w_down, gslot, layer, tf=256):
    B, E, cap, _ = xs.shape
    return pl.pallas_call(
        _ffn_body,
        grid=(E, EXPERT_FF // tf),
        in_specs=[
            pl.BlockSpec((B, None, cap, D_MODEL), lambda e, f: (0, e, 0, 0)),
            pl.BlockSpec((None, None, D_MODEL, tf), lambda e, f: (layer, e, 0, f)),
            pl.BlockSpec((None, None, D_MODEL, tf), lambda e, f: (layer, e, 0, f)),
            pl.BlockSpec((None, None, tf, D_MODEL), lambda e, f: (layer, e, f, 0)),
            pl.BlockSpec((B, None, cap, LANES), lambda e, f: (0, e, 0, 0)),
        ],
        out_specs=pl.BlockSpec((B, None, cap, D_MODEL), lambda e, f: (0, e, 0, 0)),
        out_shape=jax.ShapeDtypeStruct((B, E, cap, D_MODEL), BF16),
        scratch_shapes=[pltpu.VMEM((B, cap, D_MODEL), F32)],
        compiler_params=_params("arbitrary", "arbitrary"),
        name="ffn",
    )(xs, w_gate, w_up, w_down, gslot)


def _scatter_body(offs_ref, h_ref, y_ref, rk_ref, o_ref, ycat_ref, *, cap):
    tp = h_ref.shape[0]
    b = pl.program_id(0)
    t = pl.program_id(2)
    row_f = lax.broadcasted_iota(jnp.int32, (SLOT_WIN, LANES), 0).astype(F32)

    for cc in range(tp // LANES):
        c = t * (tp // LANES) + cc
        rows = slice(cc * LANES, (cc + 1) * LANES)
        los = [offs_ref[(b * N_EXPERTS + e) * LANES + c] for e in range(N_EXPERTS)]
        his = [offs_ref[(b * N_EXPERTS + e) * LANES + c + 1] for e in range(N_EXPERTS)]
        firsts = [(lo // 16) * 16 for lo in los]
        starts = [_window_start(lo, 0, cap) for lo in los]
        hots = [_slot_onehot(row_f, starts[e], starts[e], rk_ref[e, pl.ds(c, 1), :]) for e in range(N_EXPERTS)]
        for e in range(N_EXPERTS):
            ycat_ref[cc, e * SLOT_WIN:(e + 1) * SLOT_WIN, :] = y_ref[e, pl.ds(starts[e], SLOT_WIN), :]
        stacked = jnp.concatenate([jnp.where(h, 1.0, 0.0).astype(BF16) for h in hots], axis=0)
        o_ref[rows, :] = h_ref[rows, :] + _tn(stacked, ycat_ref[cc])

        overflow = his[0] - firsts[0] > SLOT_WIN
        for e in range(1, N_EXPERTS):
            overflow = overflow | (his[e] - firsts[e] > SLOT_WIN)

        @pl.when(overflow)
        def _():
            for e in range(N_EXPERTS):
                def extra(w, carry, e=e):
                    start = _window_start(firsts[e], w, cap)
                    hot = _slot_onehot(row_f, start, firsts[e] + w * SLOT_WIN, rk_ref[e, pl.ds(c, 1), :])
                    o_ref[rows, :] = o_ref[rows, :] + _tn(
                        jnp.where(hot, 1.0, 0.0).astype(BF16), y_ref[e, pl.ds(start, SLOT_WIN), :])
                    return carry
                lax.fori_loop(1, (his[e] - firsts[e] + SLOT_WIN - 1) // SLOT_WIN, extra, 0)


def _scatter(offs_flat, h3, y, rank4, cap, tp=512, dcols=512):
    B, S, _ = h3.shape
    n_chunks = S // LANES
    return pl.pallas_call(
        functools.partial(_scatter_body, cap=cap),
        grid_spec=pltpu.PrefetchScalarGridSpec(
            num_scalar_prefetch=1,
            grid=(B, D_MODEL // dcols, S // tp),
            in_specs=[
                pl.BlockSpec((None, tp, dcols), lambda b, q, t, offs: (b, t, q)),
                pl.BlockSpec((None, N_EXPERTS, cap, dcols), lambda b, q, t, offs: (b, 0, 0, q)),
                pl.BlockSpec((None, N_EXPERTS, n_chunks, LANES), lambda b, q, t, offs: (b, 0, 0, 0)),
            ],
            out_specs=pl.BlockSpec((None, tp, dcols), lambda b, q, t, offs: (b, t, q)),
            scratch_shapes=[pltpu.VMEM((tp // LANES, N_EXPERTS * SLOT_WIN, dcols), BF16)],
        ),
        out_shape=jax.ShapeDtypeStruct((B, S, D_MODEL), F32),
        compiler_params=_params("arbitrary", "arbitrary", "arbitrary"),
        name="scatter",
    )(offs_flat, h3, y, rank4)


def _ple_body(hrow_ref, g_ref, wg_ref, p_ref, wp_ref, hblk_ref, o_ref, hn_ref):
    tm = hrow_ref.shape[0]

    @pl.when(pl.program_id(1) == 0)
    def _():
        def norm_rows(rows):
            hn_ref[rows, :] = _rms(hrow_ref[rows, :], g_ref[...]).astype(BF16)
        _row_chunks(tm, 128, norm_rows)

    gate = _sigmoid(_dot(hn_ref[...], wg_ref[...].astype(BF16)))
    emb = _dot(p_ref[...].astype(BF16), wp_ref[...].astype(BF16))
    o_ref[...] = hblk_ref[...] + gate * emb


def _ple(h2d, gain, w_gate, p2d, w_ple, layer, tm=1024, tn=512):
    m = h2d.shape[0]
    return pl.pallas_call(
        _ple_body,
        grid=(m // tm, D_MODEL // tn),
        in_specs=[
            pl.BlockSpec((tm, D_MODEL), lambda i, j: (i, 0)),
            pl.BlockSpec((1, D_MODEL), lambda i, j: (0, 0)),
            pl.BlockSpec((None, D_MODEL, tn), lambda i, j: (layer, 0, j)),
            pl.BlockSpec((None, tm, PLE_DIM), lambda i, j: (layer, i, 0)),
            pl.BlockSpec((None, PLE_DIM, tn), lambda i, j: (layer, 0, j)),
            pl.BlockSpec((tm, tn), lambda i, j: (i, j)),
        ],
        out_specs=pl.BlockSpec((tm, tn), lambda i, j: (i, j)),
        out_shape=jax.ShapeDtypeStruct((m, D_MODEL), F32),
        scratch_shapes=[pltpu.VMEM((tm, D_MODEL), BF16)],
        compiler_params=_params("arbitrary", "arbitrary"),
        name="ple",
    )(h2d, gain, w_gate, p2d, w_ple, h2d)


def _final_body(h_ref, g_ref, o_ref):
    o_ref[...] = _rms(h_ref[...], g_ref[...])


def _final_norm(h2d, gain, tm=256):
    m = h2d.shape[0]
    return pl.pallas_call(
        _final_body,
        grid=(m // tm,),
        in_specs=[pl.BlockSpec((tm, D_MODEL), lambda i: (i, 0)), pl.BlockSpec((1, D_MODEL), lambda i: (0, 0))],
        out_specs=pl.BlockSpec((tm, D_MODEL), lambda i: (i, 0)),
        out_shape=jax.ShapeDtypeStruct((m, D_MODEL), F32),
        compiler_params=_params("arbitrary"),
        name="final_norm",
    )(h2d, gain)


def _rope_cs(pos, dim, theta):
    inv = theta ** (-jnp.arange(0, dim, 2, dtype=F32) / dim)
    ang = pos.astype(F32)[:, None] * inv[None, :]
    return jnp.cos(ang), jnp.sin(ang)


def _tables(S):
    cos, sin = _rope_cs(jnp.arange(S), ROPE_DIM, ROPE_THETA)
    z = jnp.zeros_like(sin)
    rest = HEAD_DIM - ROPE_DIM
    a_c = jnp.concatenate([cos, cos, jnp.ones((S, rest), F32)], axis=1)
    a_sp = jnp.concatenate([z, sin, jnp.zeros((S, rest), F32)], axis=1)
    a_sm = jnp.concatenate([-sin, z, jnp.zeros((S, rest), F32)], axis=1)
    rows = S // GRID_W
    rc, rs = _rope_cs(jnp.repeat(jnp.arange(rows), GRID_W), HEAD_DIM // 2, AXIAL_THETA)
    cc, cs = _rope_cs(jnp.tile(jnp.arange(GRID_W), rows), HEAD_DIM // 2, AXIAL_THETA)
    zz = jnp.zeros_like(rs)
    b_c = jnp.concatenate([rc, rc, cc, cc], axis=1)
    b_sp = jnp.concatenate([zz, rs, zz, cs], axis=1)
    b_sm = jnp.concatenate([-rs, zz, -cs, zz], axis=1)
    return (a_c, a_sp, a_sm), (b_c, b_sp, b_sm)


def kernel(x, p, norm_mix, w_in, conv_w, q_norm, k_norm, out_norm_a, out_norm_b, gdn_a_log, gdn_dt_bias,
           gdn_norm, w_out, norm_moe, w_router, w_gate, w_up, w_down, norm_ple, w_ple, w_ple_gate, norm_final):
    B, S, D = x.shape
    depth = w_in.shape[0]
    assert D == D_MODEL and S % 1024 == 0
    cap = EC_CAPACITY * S // N_EXPERTS
    width = GROUP * DELTA_CHUNK
    tabs_a, tabs_b = _tables(S)
    p2d = p.reshape(depth, B * S, PLE_DIM)
    row = lambda v: v.reshape(1, -1)

    h = x.reshape(B * S, D)
    for i in range(depth):
        proj, small = _proj(h, row(norm_mix[i]), w_in, i)
        proj3 = proj.reshape(B, S, D_MAIN)

        y_a = _mix_a(proj3, tabs_a)
        y_b = _mix_b(proj3, tabs_b, row(q_norm[i]), row(k_norm[i]))

        lg = small[:, :D_SMALL].reshape(B, S // width, width, 4, C_HEADS).transpose(0, 3, 4, 1, 2)
        par = jnp.concatenate([gdn_a_log[i], gdn_dt_bias[i]], axis=0)
        par = jnp.broadcast_to(par.T[:, :, None], (C_HEADS, 4, width))
        y_c = _mix_c(proj3, conv_w, lg, par, row(gdn_norm[i]), i)

        h = _out_proj(y_a.reshape(B * S, A_W), y_b.reshape(B * S, B_QW), y_c.reshape(B * S, C_W),
                      row(out_norm_a[i]), row(out_norm_b[i]), w_out, h, i)

        hn, aff_t = _moe_pre(h.reshape(B, S, D), row(norm_moe[i]), w_router[i].T)
        rank, gates, offs = _topk(aff_t, cap)
        offs_flat = offs.reshape(-1)
        rank4 = rank.reshape(B, N_EXPERTS, S // LANES, LANES)
        gate4 = gates.reshape(B, N_EXPERTS, S // LANES, LANES)
        xs, gslot = _gather(offs_flat, hn, rank4, gate4, cap)
        y = _ffn(xs, w_gate, w_up, w_down, gslot, i)
        h = _scatter(offs_flat, h.reshape(B, S, D), y, rank4, cap).reshape(B * S, D)

        h = _ple(h, row(norm_ple[i]), w_ple_gate, p2d, w_ple, i)
    return _final_norm(h, row(norm_final)).reshape(B, S, D)
```

```python
import functools

import jax
import jax.numpy as jnp
from jax import lax
from jax.experimental import pallas as pl
from jax.experimental.pallas import tpu as pltpu

F32 = jnp.float32
BF16 = jnp.bfloat16

D_MODEL = 2048
HEAD_DIM = 128
A_HEADS = 4
B_HEADS = 8
B_KV_HEADS = 2
B_GROUP = B_HEADS // B_KV_HEADS
C_HEADS = 4
A_W = A_HEADS * HEAD_DIM
B_QW = B_HEADS * HEAD_DIM
B_KVW = B_KV_HEADS * HEAD_DIM
C_W = C_HEADS * HEAD_DIM
DILATED_PATTERNS = ((128, 1), (512, 4), (2048, 16))
ROPE_THETA = 500000.0
ROPE_DIM = HEAD_DIM // 4
AXIAL_THETA = 10000.0
GRID_W = 64
CONV_K = 5
DELTA_CHUNK = 64
N_EXPERTS = 16
EC_CAPACITY = 2
EXPERT_FF = D_MODEL // 2
PLE_DIM = 256
NORM_EPS = 1e-6
D_MAIN = 3 * A_W + B_QW + 2 * B_KVW + 4 * C_W
D_SMALL = 4 * C_HEADS

COL_AQ, COL_AK, COL_AV = 0, 4, 8
COL_BQ, COL_BK, COL_BV = 12, 20, 22
COL_CQ, COL_CK, COL_CV, COL_CZ = 24, 28, 32, 36

LANES = 128
VMEM_LIMIT = 56 * 1024 * 1024
NEG_BIG = -1e30
GROUP = 4
PRE_GROUPS = 4
A_BLOCKS = 8
SLOT_WIN = 48
FFN_ROWS = 512


def _params(*sem):
    return pltpu.CompilerParams(dimension_semantics=sem, vmem_limit_bytes=VMEM_LIMIT)


def _nt(a, b):
    return lax.dot_general(a, b, (((1,), (1,)), ((), ())), preferred_element_type=F32)


def _tn(a, b):
    return lax.dot_general(a, b, (((0,), (0,)), ((), ())), preferred_element_type=F32)


def _dot(a, b):
    return jnp.dot(a, b, preferred_element_type=F32)


def _split(x):
    hi = x.astype(BF16)
    lo = (x - hi.astype(F32)).astype(BF16)
    return hi, lo


def _dotb(a, b):
    return _dot(a.astype(BF16), b.astype(BF16))


def _dot3(a, b):
    ah, al = _split(a)
    bh, bl = _split(b)
    return _dot(ah, bh) + (_dot(ah, bl) + _dot(al, bh))


def _rms(x, gain):
    return x * lax.rsqrt(jnp.mean(x * x, axis=-1, keepdims=True) + NORM_EPS) * gain


def _sigmoid(x):
    return 1.0 / (1.0 + jnp.exp(-x))


def _silu(x):
    return x * _sigmoid(x)


def _row_chunks(n_rows, chunk, fn):
    def body(c, carry):
        fn(pl.ds(pl.multiple_of(c * chunk, chunk), chunk))
        return carry
    lax.fori_loop(0, n_rows // chunk, body, 0)


def _proj_body(x_ref, g_ref, w_ref, ws_ref, o_ref, os_ref, xn_ref):
    tm = x_ref.shape[0]

    @pl.when(pl.program_id(1) == 0)
    def _():
        def norm_rows(rows):
            xn_ref[rows, :] = _rms(x_ref[rows, :], g_ref[...]).astype(BF16)
        _row_chunks(tm, 128, norm_rows)
        sub = lax.broadcasted_iota(jnp.int32, ws_ref.shape, 0)
        os_ref[...] = _nt(xn_ref[...], jnp.where(sub < D_SMALL, ws_ref[...], 0.0).astype(BF16))

    o_ref[...] = _nt(xn_ref[...], w_ref[...].astype(BF16)).astype(o_ref.dtype)


def _proj(h2d, gain, w_in_t, layer, tm=1024, tn=512):
    m = h2d.shape[0]
    return pl.pallas_call(
        _proj_body,
        grid=(m // tm, D_MAIN // tn),
        in_specs=[
            pl.BlockSpec((tm, D_MODEL), lambda i, j: (i, 0)),
            pl.BlockSpec((1, D_MODEL), lambda i, j: (0, 0)),
            pl.BlockSpec((None, tn, D_MODEL), lambda i, j: (layer, j, 0)),
            pl.BlockSpec((None, LANES, D_MODEL), lambda i, j: (layer, D_MAIN // LANES, 0)),
        ],
        out_specs=[
            pl.BlockSpec((tm, tn), lambda i, j: (i, j)),
            pl.BlockSpec((tm, LANES), lambda i, j: (i, 0)),
        ],
        out_shape=[
            jax.ShapeDtypeStruct((m, D_MAIN), BF16),
            jax.ShapeDtypeStruct((m, LANES), F32),
        ],
        scratch_shapes=[pltpu.VMEM((tm, D_MODEL), BF16)],
        compiler_params=_params("arbitrary", "arbitrary"),
        name="proj",
    )(h2d, gain, w_in_t, w_in_t)


def _mixa_body(q_ref, k_ref, v_ref, c_ref, sp_ref, sm_ref, o_ref,
               qf, kf, vf, qd, kp, vp, acc, den, mrun):
    S = q_ref.shape[0]
    half = 64
    blk = 128
    scale = HEAD_DIM ** -0.5

    def prep(rows):
        def rope(x):
            return (x * c_ref[rows, :] + pltpu.roll(x, ROPE_DIM // 2, 1) * sp_ref[rows, :]
                    + pltpu.roll(x, LANES - ROPE_DIM // 2, 1) * sm_ref[rows, :])
        qf[rows, :] = rope(q_ref[rows, :].astype(F32)) * scale
        kf[rows, :] = rope(k_ref[rows, :].astype(F32))
        vf[rows, :] = v_ref[rows, :].astype(F32)
        acc[rows, :] = jnp.zeros((512, LANES), F32)
        den[rows, :] = jnp.zeros((512, LANES), F32)
        mrun[rows, :] = jnp.full((512, LANES), NEG_BIG, F32)
    _row_chunks(S, 512, prep)

    row = lax.broadcasted_iota(jnp.int32, (blk, 2 * blk), 0)
    col = lax.broadcasted_iota(jnp.int32, (blk, 2 * blk), 1)
    in_band = jnp.abs(col - row - half) <= half

    for window, dil in DILATED_PATTERNS:
        assert window // (2 * dil) == half
        L = S // dil
        nblk = L // blk
        stride_k = L + 2 * half
        for j in range(dil):
            for base in (j * stride_k, j * stride_k + half + L):
                kp[base:base + half, :] = jnp.zeros((half, LANES), BF16)
                vp[base:base + half, :] = jnp.zeros((half, 2 * LANES), BF16)

        def fold(r, carry, dil=dil, L=L, stride_k=stride_k):
            def piece(c, carry2):
                src = pl.ds(r + c * (256 * dil), 256, stride=dil) if dil > 1 else pl.ds(
                    pl.multiple_of(c * 256, 256), 256)
                dst = pl.ds(pl.multiple_of(r * stride_k + half + c * 256, half), 256)
                qd[pl.ds(pl.multiple_of(r * L + c * 256, 256), 256), :] = qf[src, :].astype(BF16)
                kp[dst, :] = kf[src, :].astype(BF16)
                vp[dst, 0:LANES] = vf[src, :].astype(BF16)
                vp[dst, LANES:2 * LANES] = jnp.ones((256, LANES), BF16)
                return carry2
            lax.fori_loop(0, L // 256, piece, 0)
            return carry
        lax.fori_loop(0, dil, fold, 0)

        def blocks(it, carry, dil=dil, L=L, nblk=nblk, stride_k=stride_k):
            items = [it * A_BLOCKS + u for u in range(A_BLOCKS)]
            rs = [w // nblk for w in items]
            nbs = [w % nblk for w in items]
            q0 = [pl.multiple_of(r * L + nb * blk, blk) for r, nb in zip(rs, nbs)]
            k0 = [pl.multiple_of(r * stride_k + nb * blk, blk) for r, nb in zip(rs, nbs)]
            s = [_nt(qd[pl.ds(a, blk), :], kp[pl.ds(b, 2 * blk), :]) for a, b in zip(q0, k0)]
            kpos = [nb * blk - half + col for nb in nbs]
            s = [jnp.where(in_band & (kp_ >= 0) & (kp_ < L), s_, NEG_BIG) for s_, kp_ in zip(s, kpos)]
            m_b = [jnp.max(s_, axis=1, keepdims=True) for s_ in s]
            e = [jnp.exp(s_ - m_).astype(BF16) for s_, m_ in zip(s, m_b)]
            od = [_dot(e_, vp[pl.ds(b, 2 * blk), :]) for e_, b in zip(e, k0)]
            for r, nb, m_, od_ in zip(rs, nbs, m_b, od):
                rows = (pl.ds(r + nb * (blk * dil), blk, stride=dil) if dil > 1
                        else pl.ds(pl.multiple_of(nb * blk, blk), blk))
                m_old = mrun[rows, :]
                m_new = jnp.maximum(m_old, m_)
                a_old = jnp.exp(m_old - m_new)
                a_new = jnp.exp(m_ - m_new)
                acc[rows, :] = acc[rows, :] * a_old + od_[:, 0:LANES] * a_new
                den[rows, :] = den[rows, :] * a_old + od_[:, LANES:2 * LANES] * a_new
                mrun[rows, :] = m_new
            return carry
        lax.fori_loop(0, (dil * nblk) // A_BLOCKS, blocks, 0)

    def finish(rows):
        o_ref[rows, :] = (acc[rows, :] / den[rows, :]).astype(o_ref.dtype)
    _row_chunks(S, 512, finish)


def _mix_a(proj3, tabs):
    B, S, _ = proj3.shape
    max_dil = max(d for _, d in DILATED_PATTERNS)
    head = lambda base: pl.BlockSpec((None, S, LANES), lambda b, h: (b, 0, base + h))
    tab = pl.BlockSpec((S, LANES), lambda b, h: (0, 0))
    return pl.pallas_call(
        _mixa_body,
        grid=(B, A_HEADS),
        in_specs=[head(COL_AQ), head(COL_AK), head(COL_AV), tab, tab, tab],
        out_specs=pl.BlockSpec((None, S, LANES), lambda b, h: (b, 0, h)),
        out_shape=jax.ShapeDtypeStruct((B, S, A_W), BF16),
        scratch_shapes=[
            pltpu.VMEM((S, LANES), F32), pltpu.VMEM((S, LANES), F32), pltpu.VMEM((S, LANES), F32),
            pltpu.VMEM((S, LANES), BF16),
            pltpu.VMEM((S + max_dil * LANES, LANES), BF16),
            pltpu.VMEM((S + max_dil * LANES, 2 * LANES), BF16),
            pltpu.VMEM((S, LANES), F32), pltpu.VMEM((S, LANES), F32), pltpu.VMEM((S, LANES), F32),
        ],
        compiler_params=_params("arbitrary", "arbitrary"),
        name="mix_a",
    )(proj3, proj3, proj3, *tabs)


def _mixb_body(q_ref, k_ref, v_ref, c_ref, sp_ref, sm_ref, qg_ref, kg_ref, o_ref,
               kt_ref, va_ref, q4_ref, m_ref, acc_ref, *, tk):
    S = k_ref.shape[0]
    tq = q_ref.shape[0]
    qi = pl.program_id(2)
    quarter = HEAD_DIM // 4

    def rope(x, rows):
        return (x * c_ref[rows, :] + pltpu.roll(x, quarter, 1) * sp_ref[rows, :]
                + pltpu.roll(x, LANES - quarter, 1) * sm_ref[rows, :])

    @pl.when(qi == 0)
    def _():
        def kv_chunk(c, carry):
            rows = pl.ds(pl.multiple_of(c * tk, tk), tk)
            k = rope(_rms(k_ref[rows, :].astype(F32), kg_ref[...]), rows)
            kt_ref[c] = k.T.astype(BF16)
            va_ref[rows, 0:LANES] = v_ref[rows, :]
            va_ref[rows, LANES:2 * LANES] = jnp.ones((tk, LANES), BF16)
            return carry
        lax.fori_loop(0, S // tk, kv_chunk, 0)

    rows_q = pl.ds(pl.multiple_of(qi * tq, tq), tq)
    for g in range(B_GROUP):
        q = _rms(q_ref[:, g * LANES:(g + 1) * LANES].astype(F32), qg_ref[...])
        q = rope(q, rows_q) * (HEAD_DIM ** -0.5)
        q4_ref[g * tq:(g + 1) * tq, :] = q.astype(BF16)
    m_ref[...] = jnp.full(m_ref.shape, NEG_BIG, F32)
    acc_ref[...] = jnp.zeros(acc_ref.shape, F32)

    def kv_step(c, carry):
        rows = pl.ds(pl.multiple_of(c * tk, tk), tk)
        s = _dot(q4_ref[...], kt_ref[c])
        m_old = m_ref[...]
        m_new = jnp.maximum(m_old, jnp.max(s, axis=1, keepdims=True))
        alpha = jnp.exp(m_old - m_new)
        p = jnp.exp(s - jnp.concatenate([m_new] * (tk // LANES), axis=1))
        acc_ref[...] = acc_ref[...] * jnp.concatenate([alpha, alpha], axis=1) + _dot(p.astype(BF16), va_ref[rows, :])
        m_ref[...] = m_new
        return carry
    lax.fori_loop(0, S // tk, kv_step, 0)

    o = acc_ref[:, 0:LANES] / acc_ref[:, LANES:2 * LANES]
    for g in range(B_GROUP):
        o_ref[:, g * LANES:(g + 1) * LANES] = o[g * tq:(g + 1) * tq, :].astype(o_ref.dtype)


def _mix_b(proj3, tabs, q_gain, k_gain, tq=1024, tk=512):
    B, S, _ = proj3.shape
    qw = B_GROUP * LANES
    tab = pl.BlockSpec((S, LANES), lambda b, h, i: (0, 0))
    gain = pl.BlockSpec((1, LANES), lambda b, h, i: (0, 0))
    return pl.pallas_call(
        functools.partial(_mixb_body, tk=tk),
        grid=(B, B_KV_HEADS, S // tq),
        in_specs=[
            pl.BlockSpec((None, tq, qw), lambda b, h, i: (b, i, COL_BQ // B_GROUP + h)),
            pl.BlockSpec((None, S, LANES), lambda b, h, i: (b, 0, COL_BK + h)),
            pl.BlockSpec((None, S, LANES), lambda b, h, i: (b, 0, COL_BV + h)),
            tab, tab, tab, gain, gain,
        ],
        out_specs=pl.BlockSpec((None, tq, qw), lambda b, h, i: (b, i, h)),
        out_shape=jax.ShapeDtypeStruct((B, S, B_QW), BF16),
        scratch_shapes=[
            pltpu.VMEM((S // tk, LANES, tk), BF16),
            pltpu.VMEM((S, 2 * LANES), BF16),
            pltpu.VMEM((B_GROUP * tq, LANES), BF16),
            pltpu.VMEM((B_GROUP * tq, LANES), F32),
            pltpu.VMEM((B_GROUP * tq, 2 * LANES), F32),
        ],
        compiler_params=_params("arbitrary", "arbitrary", "arbitrary"),
        name="mix_b",
    )(proj3, proj3, proj3, *tabs, q_gain, k_gain)


def _softplus(x):
    return jnp.maximum(x, 0.0) + jnp.log(1.0 + jnp.exp(-jnp.abs(x)))


def _mixc_body(q_ref, k_ref, v_ref, z_ref, cwq_ref, cwk_ref, cwv_ref, lg_ref, par_ref, gn_ref, o_ref,
               xp, qn, kn, vn, tab, qp_ref, op_ref, p_ref, n_ref):
    S = q_ref.shape[0]
    C = DELTA_CHUNK
    n_chunks = S // C
    pad = 8

    for src, cw, dst, kind in ((q_ref, cwq_ref, qn, "q"), (k_ref, cwk_ref, kn, "k"), (v_ref, cwv_ref, vn, "v")):
        xp[0:pad, :] = jnp.zeros((pad, LANES), F32)
        xp[pad + S:2 * pad + S, :] = jnp.zeros((pad, LANES), F32)

        def load(rows, src=src):
            xp[pl.ds(pl.multiple_of(rows.start + pad, pad), 512), :] = src[rows, :].astype(F32)
        _row_chunks(S, 512, load)
        for c in range(S // 512):
            base = pad - CONV_K // 2 + c * 512
            y = xp[base:base + 512, :] * cw[0:1, :]
            for j in range(1, CONV_K):
                y = y + xp[base + j:base + j + 512, :] * cw[j:j + 1, :]
            y = _silu(y)
            if kind != "v":
                y = y * lax.rsqrt(jnp.sum(y * y, axis=-1, keepdims=True) + NORM_EPS)
            if kind == "q":
                y = y * (HEAD_DIM ** -0.5)
            dst[c * 512:(c + 1) * 512, :] = y

    W = GROUP * C
    n_groups = S // W
    set_chunks = PRE_GROUPS * GROUP
    n_sets = n_chunks // set_chunks
    ri = lax.broadcasted_iota(jnp.int32, (W, W), 0)
    ci = lax.broadcasted_iota(jnp.int32, (W, W), 1)
    eye = ri == ci
    eye_f = jnp.where(eye, 1.0, 0.0)
    same_block = [jnp.right_shift(ri, s) == jnp.right_shift(ci, s) for s in (3, 4, 5, 6)]
    same_chunk = same_block[-1]
    chunk_start = jnp.right_shift(ri, 6) * C
    stack_mask = (jnp.right_shift(lax.broadcasted_iota(jnp.int32, (GROUP * HEAD_DIM, W), 0), 7)
                  == jnp.right_shift(lax.broadcasted_iota(jnp.int32, (GROUP * HEAD_DIM, W), 1), 6))
    lane_w = lax.broadcasted_iota(jnp.int32, (1, W), 1)
    for d in range(2):
        g = -jnp.exp(par_ref[d:d + 1, :]) * _softplus(lg_ref[2 + d] + par_ref[2 + d:3 + d, :])
        cum = same_chunk & ((ri <= ci) if d == 0 else (ri >= ci))
        tab[d] = _dot3(g, jnp.where(cum, 1.0, 0.0))
        tab[2 + d] = _sigmoid(lg_ref[d])

    def groups_pre(chains, ring):
        each = lambda fn, *lists: [fn(*args) for args in zip(*lists)]
        ds = [d for _, d in chains]
        rows = [pl.ds(pl.multiple_of(i * W, W), W) for i, _ in chains]
        q = [qn[r, :] for r in rows]
        k = [kn[r, :] for r in rows]
        v = [vn[r, :] for r in rows]
        gr = [tab[d, pl.ds(i, 1), :] for i, d in chains]
        br = [tab[2 + d, pl.ds(i, 1), :] for i, d in chains]
        to_col = lambda mask, r: jnp.sum(jnp.where(mask, r, 0.0), axis=1, keepdims=True)
        gcol = [to_col(eye, g) for g in gr]
        bcol = [to_col(eye, b) for b in br]
        glast = [to_col(ci == (chunk_start + (C - 1) if d == 0 else chunk_start), g) for d, g in zip(ds, gr)]
        incl = [same_chunk & ((ci <= ri) if d == 0 else (ci >= ri)) for d in ds]
        strict = [same_chunk & ((ci < ri) if d == 0 else (ci > ri)) for d in ds]
        decay = each(lambda m, gc, g: jnp.exp(jnp.where(m, gc - g, NEG_BIG)), incl, gcol, gr)
        kb = each(lambda a, b: a * b, k, bcol)
        k16 = [a.astype(BF16) for a in k]
        kk = each(lambda a, b: _nt(a.astype(BF16), b), kb, k16)
        yield
        a = each(lambda m, p, dc: jnp.where(m, p * dc, 0.0), strict, kk, decay)
        eg = [jnp.exp(g) for g in gcol]
        rhs = each(lambda vv, b, kbb, e: jnp.concatenate([vv * b, kbb * e], axis=1), v, bcol, kb, eg)
        x = [jnp.where(same_block[0], -m, 0.0) for m in a]
        s1 = [eye_f + m for m in x]
        x2 = each(_dotb, x, x)
        yield
        x2s1 = each(_dotb, x2, s1)
        x4 = each(_dotb, x2, x2)
        yield
        s2 = each(lambda p, m: p + m, s1, x2s1)
        t = each(lambda p, m, n: p + _dotb(m, n), s2, x4, s2)
        yield
        for lvl in range(1, len(same_block)):
            off = same_block[lvl] & jnp.logical_not(same_block[lvl - 1])
            y = each(lambda m, tt: _dotb(jnp.where(off, m, 0.0), tt), a, t)
            yield
            t = each(lambda tt, yy: tt - _dotb(tt, yy), t, y)
            yield
        r16 = each(lambda tt, r: _dotb(tt, r).astype(BF16), t, rhs)
        yield
        qk = each(lambda a_, b_, dc: _nt(a_.astype(BF16), b_) * dc, q, k16, decay)
        yield
        stacked = each(lambda kk_, gl, gc: jnp.where(
            stack_mask, jnp.concatenate([(kk_ * jnp.exp(gl - gc)).T] * GROUP, axis=0), 0.0).astype(BF16),
            k, glast, gcol)
        np_ = each(_dot, stacked, r16)
        yield
        qo = each(lambda m, r: _dot(m.astype(BF16), r), qk, r16)
        for j, (i, d) in enumerate(chains):
            qp_ref[d, rows[j], :] = (q[j] * eg[j] - qo[j][:, LANES:2 * LANES]).astype(BF16)
            op_ref[d, rows[j], :] = qo[j][:, 0:LANES]
            chunks = pl.ds(ring + (i * GROUP) % set_chunks, GROUP)
            n_ref[d, chunks] = np_[j][:, 0:LANES].reshape(GROUP, HEAD_DIM, HEAD_DIM)
            p_ref[d, chunks] = np_[j][:, LANES:2 * LANES].astype(BF16).reshape(GROUP, HEAD_DIM, HEAD_DIM)

    def set_chains(j):
        return ([(j * PRE_GROUPS + u, 0) for u in range(PRE_GROUPS)]
                + [(n_groups - 1 - (j * PRE_GROUPS + u), 1) for u in range(PRE_GROUPS)])

    def scan_steps(j, state, out):
        st = list(state)
        for step in range(set_chunks):
            i = j * set_chunks + step
            new = []
            for d in range(2):
                n = i if d == 0 else n_chunks - 1 - i
                rows = pl.ds(pl.multiple_of(n * C, C), C)
                gr = tab[d, pl.ds(n // GROUP, 1), :]
                target = (n % GROUP) * C + (C - 1 if d == 0 else 0)
                g_last = jnp.sum(jnp.where(lane_w == target, gr, 0.0), axis=1, keepdims=True)
                s16 = st[d].astype(BF16)
                slot = (j % 2) * set_chunks + n % set_chunks
                op_ref[d, rows, :] = op_ref[d, rows, :] + _dot(qp_ref[d, rows, :], s16)
                new.append(st[d] * jnp.exp(g_last) - _dot(p_ref[d, slot], s16) + n_ref[d, slot])
            st = new
            yield
        out.extend(st)

    def run_together(*gens):
        live = list(gens)
        while live:
            for g in list(live):
                try:
                    next(g)
                except StopIteration:
                    live.remove(g)

    run_together(groups_pre(set_chains(0), 0))

    def piped(j, state):
        out = []
        run_together(groups_pre(set_chains(j), (j % 2) * set_chunks), scan_steps(j - 1, state, out))
        return tuple(out)
    zero = jnp.zeros((HEAD_DIM, HEAD_DIM), F32)
    state = lax.fori_loop(1, n_sets, piped, (zero, zero))
    run_together(scan_steps(n_sets - 1, state, []))

    def finish(rows):
        o = op_ref[0, rows, :] + op_ref[1, rows, :]
        o_ref[rows, :] = (_rms(o, gn_ref[...]) * _silu(z_ref[rows, :].astype(F32))).astype(o_ref.dtype)
    _row_chunks(S, 512, finish)


def _mix_c(proj3, conv_w, logits_rows, par, gn, layer):
    B, S, _ = proj3.shape
    n_chunks = S // DELTA_CHUNK
    width = GROUP * DELTA_CHUNK
    n_groups = S // width
    head = lambda base: pl.BlockSpec((None, S, LANES), lambda b, h: (b, 0, base + h))
    cw = lambda base: pl.BlockSpec((None, CONV_K, LANES), lambda b, h: (layer, 0, base + h))
    return pl.pallas_call(
        _mixc_body,
        grid=(B, C_HEADS),
        in_specs=[
            head(COL_CQ), head(COL_CK), head(COL_CV), head(COL_CZ),
            cw(0), cw(C_HEADS), cw(2 * C_HEADS),
            pl.BlockSpec((None, 4, None, n_groups, width), lambda b, h: (b, 0, h, 0, 0)),
            pl.BlockSpec((None, 4, width), lambda b, h: (h, 0, 0)),
            pl.BlockSpec((1, LANES), lambda b, h: (0, 0)),
        ],
        out_specs=pl.BlockSpec((None, S, LANES), lambda b, h: (b, 0, h)),
        out_shape=jax.ShapeDtypeStruct((B, S, C_W), BF16),
        scratch_shapes=[
            pltpu.VMEM((S + 16, LANES), F32),
            pltpu.VMEM((S, LANES), F32), pltpu.VMEM((S, LANES), F32), pltpu.VMEM((S, LANES), F32),
            pltpu.VMEM((4, n_groups, width), F32),
            pltpu.VMEM((2, S, LANES), BF16),
            pltpu.VMEM((2, S, LANES), F32),
            pltpu.VMEM((2, 2 * PRE_GROUPS * GROUP, HEAD_DIM, HEAD_DIM), BF16),
            pltpu.VMEM((2, 2 * PRE_GROUPS * GROUP, HEAD_DIM, HEAD_DIM), F32),
        ],
        compiler_params=_params("arbitrary", "arbitrary"),
        name="mix_c",
    )(proj3, proj3, proj3, proj3, conv_w, conv_w, conv_w, logits_rows, par, gn)


def _outproj_body(ya_ref, yb_ref, yc_ref, ga_ref, gb_ref, w_ref, h_ref, o_ref, yn_ref):
    tm = ya_ref.shape[0]

    @pl.when(pl.program_id(1) == 0)
    def _():
        def norm_rows(rows):
            yn_ref[rows, 0:A_W] = _rms(ya_ref[rows, :].astype(F32), ga_ref[...]).astype(BF16)
            yn_ref[rows, A_W:A_W + B_QW] = _rms(yb_ref[rows, :].astype(F32), gb_ref[...]).astype(BF16)
            yn_ref[rows, A_W + B_QW:D_MODEL] = yc_ref[rows, :]
        _row_chunks(tm, 128, norm_rows)

    o_ref[...] = h_ref[...] + _dot(yn_ref[...], w_ref[...].astype(BF16))


def _out_proj(ya, yb, yc, ga, gb, w_out, h2d, layer, tm=1024, tn=512):
    m = h2d.shape[0]
    return pl.pallas_call(
        _outproj_body,
        grid=(m // tm, D_MODEL // tn),
        in_specs=[
            pl.BlockSpec((tm, A_W), lambda i, j: (i, 0)),
            pl.BlockSpec((tm, B_QW), lambda i, j: (i, 0)),
            pl.BlockSpec((tm, C_W), lambda i, j: (i, 0)),
            pl.BlockSpec((1, A_W), lambda i, j: (0, 0)),
            pl.BlockSpec((1, B_QW), lambda i, j: (0, 0)),
            pl.BlockSpec((None, D_MODEL, tn), lambda i, j: (layer, 0, j)),
            pl.BlockSpec((tm, tn), lambda i, j: (i, j)),
        ],
        out_specs=pl.BlockSpec((tm, tn), lambda i, j: (i, j)),
        out_shape=jax.ShapeDtypeStruct((m, D_MODEL), F32),
        scratch_shapes=[pltpu.VMEM((tm, D_MODEL), BF16)],
        compiler_params=_params("arbitrary", "arbitrary"),
        name="out_proj",
    )(ya, yb, yc, ga, gb, w_out, h2d)


def _moepre_body(h_ref, g_ref, wr_ref, hn_ref, aff_ref):
    xn = _rms(h_ref[...], g_ref[...])
    hi, lo = _split(xn)
    hn_ref[...] = hi
    whi, wlo = _split(wr_ref[...])
    logits = _nt(whi, hi) + (_nt(whi, lo) + _nt(wlo, hi))
    e = jnp.exp(logits - jnp.max(logits, axis=0, keepdims=True))
    aff_ref[...] = e / jnp.sum(e, axis=0, keepdims=True)


def _moe_pre(h3, gain, w_router_t, tm=256):
    B, S, _ = h3.shape
    return pl.pallas_call(
        _moepre_body,
        grid=(B, S // tm),
        in_specs=[
            pl.BlockSpec((None, tm, D_MODEL), lambda b, i: (b, i, 0)),
            pl.BlockSpec((1, D_MODEL), lambda b, i: (0, 0)),
            pl.BlockSpec((N_EXPERTS, D_MODEL), lambda b, i: (0, 0)),
        ],
        out_specs=[
            pl.BlockSpec((None, tm, D_MODEL), lambda b, i: (b, i, 0)),
            pl.BlockSpec((None, N_EXPERTS, tm), lambda b, i: (b, 0, i)),
        ],
        out_shape=[
            jax.ShapeDtypeStruct((B, S, D_MODEL), BF16),
            jax.ShapeDtypeStruct((B, N_EXPERTS, S), F32),
        ],
        compiler_params=_params("arbitrary", "arbitrary"),
        name="moe_pre",
    )(h3, gain, w_router_t)


def _topk_body(aff_ref, rank_ref, gate_ref, offs_ref, *, cap):
    x = aff_ref[...]
    E, S = x.shape
    xb = pltpu.bitcast(x, jnp.int32)
    count = lambda mask: jnp.sum(jnp.where(mask, 1.0, 0.0), axis=1, keepdims=True)

    def value_bit(it, t):
        cand = t | jnp.left_shift(jnp.int32(1), 30 - it)
        return jnp.where(count(xb >= cand) >= cap, cand, t)
    thr = lax.fori_loop(0, 31, value_bit, jnp.zeros((E, 1), jnp.int32))

    above = xb > thr
    tied = xb == thr
    need = cap - count(above)
    idx = lax.broadcasted_iota(jnp.int32, (E, S), 1)

    def index_bit(it, j):
        cand = j | jnp.left_shift(jnp.int32(1), 11 - it)
        return jnp.where(count(tied & (idx < cand)) < need, cand, j)
    assert S == 4096
    jmax = lax.fori_loop(0, 12, index_bit, jnp.zeros((E, 1), jnp.int32))
    sel = above | (tied & (idx <= jmax) & (need > 0.0))
    gate_ref[...] = jnp.where(sel, x, 0.0)

    ri = lax.broadcasted_iota(jnp.int32, (LANES, LANES), 0)
    ci = lax.broadcasted_iota(jnp.int32, (LANES, LANES), 1)
    before = jnp.where(ri < ci, 1.0, 0.0).astype(BF16)
    lane = lax.broadcasted_iota(jnp.int32, (E, LANES), 1)
    off = jnp.zeros((E, 1), F32)
    offs = jnp.zeros((E, LANES), F32)
    self_f = jnp.where(sel, 1.0, 0.0)
    for c in range(S // LANES):
        m_c = self_f[:, c * LANES:(c + 1) * LANES]
        rank_c = _dot(m_c.astype(BF16), before)
        rank_ref[:, c * LANES:(c + 1) * LANES] = jnp.where(m_c > 0.0, rank_c + off, -1.0)
        offs = jnp.where(lane == c, off, offs)
        off = off + jnp.sum(m_c, axis=1, keepdims=True)
    offs = jnp.where(lane == S // LANES, off, offs)
    offs_ref[...] = offs.astype(jnp.int32)


def _topk(aff_t, cap):
    B, E, S = aff_t.shape
    spec = pl.BlockSpec((None, E, S), lambda b: (b, 0, 0))
    return pl.pallas_call(
        functools.partial(_topk_body, cap=cap),
        grid=(B,),
        in_specs=[spec],
        out_specs=[spec, spec, pl.BlockSpec((None, E, LANES), lambda b: (b, 0, 0))],
        out_shape=[
            jax.ShapeDtypeStruct((B, E, S), F32),
            jax.ShapeDtypeStruct((B, E, S), F32),
            jax.ShapeDtypeStruct((B, E, LANES), jnp.int32),
        ],
        compiler_params=_params("arbitrary"),
        name="topk",
    )(aff_t)


def _window_start(lo, step, cap):
    return pl.multiple_of(jnp.minimum((lo // 16) * 16 + step * SLOT_WIN, cap - SLOT_WIN), 16)


def _slot_onehot(row_f, start, first_valid, rk):
    slot = row_f + start.astype(F32)
    return (slot == rk) & (slot >= first_valid.astype(F32))


def _gather_body(offs_ref, hn_ref, rk_ref, gt_ref, xs_ref, gs_ref, *, cap):
    S = hn_ref.shape[0]
    b = pl.program_id(0)
    dq = pl.program_id(1)
    acc_ref = xs_ref
    gacc_ref = gs_ref

    def clear(e, carry):
        acc_ref[e] = jnp.zeros(acc_ref.shape[1:], acc_ref.dtype)

        @pl.when(dq == 0)
        def _():
            gacc_ref[e] = jnp.zeros(gacc_ref.shape[1:], F32)
        return carry
    lax.fori_loop(0, N_EXPERTS, clear, 0)

    row_f = lax.broadcasted_iota(jnp.int32, (SLOT_WIN, LANES), 0).astype(F32)

    def chunk(c, carry):
        h_c = hn_ref[pl.ds(pl.multiple_of(c * LANES, LANES), LANES), :]
        los = [offs_ref[(b * N_EXPERTS + e) * LANES + c] for e in range(N_EXPERTS)]
        his = [offs_ref[(b * N_EXPERTS + e) * LANES + c + 1] for e in range(N_EXPERTS)]
        firsts = [(lo // 16) * 16 for lo in los]
        starts = [_window_start(lo, 0, cap) for lo in los]
        hots = [_slot_onehot(row_f, starts[e], starts[e], rk_ref[e, pl.ds(c, 1), :]) for e in range(N_EXPERTS)]
        stacked = jnp.concatenate([jnp.where(h, 1.0, 0.0).astype(BF16) for h in hots], axis=0)
        res = _dot(stacked, h_c)
        for e in range(N_EXPERTS):
            win = pl.ds(starts[e], SLOT_WIN)
            acc_ref[e, win, :] = acc_ref[e, win, :] + res[e * SLOT_WIN:(e + 1) * SLOT_WIN, :].astype(acc_ref.dtype)

        @pl.when(dq == 0)
        def _():
            for e in range(N_EXPERTS):
                win = pl.ds(starts[e], SLOT_WIN)
                gacc_ref[e, win, :] = gacc_ref[e, win, :] + jnp.sum(
                    jnp.where(hots[e], gt_ref[e, pl.ds(c, 1), :], 0.0), axis=1, keepdims=True)

        overflow = his[0] - firsts[0] > SLOT_WIN
        for e in range(1, N_EXPERTS):
            overflow = overflow | (his[e] - firsts[e] > SLOT_WIN)

        @pl.when(overflow)
        def _():
            for e in range(N_EXPERTS):
                def extra(w, carry2, e=e):
                    start = _window_start(firsts[e], w, cap)
                    hot = _slot_onehot(row_f, start, firsts[e] + w * SLOT_WIN, rk_ref[e, pl.ds(c, 1), :])
                    win2 = pl.ds(start, SLOT_WIN)
                    acc_ref[e, win2, :] = acc_ref[e, win2, :] + _dot(
                        jnp.where(hot, 1.0, 0.0).astype(BF16), h_c).astype(acc_ref.dtype)

                    @pl.when(dq == 0)
                    def _():
                        gacc_ref[e, win2, :] = gacc_ref[e, win2, :] + jnp.sum(
                            jnp.where(hot, gt_ref[e, pl.ds(c, 1), :], 0.0), axis=1, keepdims=True)
                    return carry2
                lax.fori_loop(1, (his[e] - firsts[e] + SLOT_WIN - 1) // SLOT_WIN, extra, 0)
        return carry
    lax.fori_loop(0, S // LANES, chunk, 0)


def _gather(offs_flat, hn3, rank4, gate4, cap, dcols=512):
    B, S, _ = hn3.shape
    n_chunks = S // LANES
    rows = pl.BlockSpec((None, N_EXPERTS, n_chunks, LANES), lambda b, q, offs: (b, 0, 0, 0))
    return pl.pallas_call(
        functools.partial(_gather_body, cap=cap),
        grid_spec=pltpu.PrefetchScalarGridSpec(
            num_scalar_prefetch=1,
            grid=(B, D_MODEL // dcols),
            in_specs=[
                pl.BlockSpec((None, S, dcols), lambda b, q, offs: (b, 0, q)),
                rows, rows,
            ],
            out_specs=[
                pl.BlockSpec((None, N_EXPERTS, cap, dcols), lambda b, q, offs: (b, 0, 0, q)),
                pl.BlockSpec((None, N_EXPERTS, cap, LANES), lambda b, q, offs: (b, 0, 0, 0)),
            ],
        ),
        out_shape=[
            jax.ShapeDtypeStruct((B, N_EXPERTS, cap, D_MODEL), BF16),
            jax.ShapeDtypeStruct((B, N_EXPERTS, cap, LANES), F32),
        ],
        compiler_params=_params("arbitrary", "arbitrary"),
        name="gather",
    )(offs_flat, hn3, rank4, gate4)


def _ffn_body(xs_ref, wg_ref, wu_ref, wd_ref, gs_ref, y_ref, acc_ref):
    f = pl.program_id(1)
    n_f = pl.num_programs(1)
    wg = wg_ref[...].astype(BF16)
    wu = wu_ref[...].astype(BF16)
    wd = wd_ref[...].astype(BF16)
    n_b, cap, _ = xs_ref.shape
    blocks = [(b, slice(r, r + FFN_ROWS)) for b in range(n_b) for r in range(0, cap, FFN_ROWS)]
    xs = [xs_ref[b, rows, :] for b, rows in blocks]
    hg = [_dot(x, wg) for x in xs]
    hu = [_dot(x, wu) for x in xs]
    hid = [(_silu(g) * u).astype(BF16) for g, u in zip(hg, hu)]
    for (b, rows), h in zip(blocks, hid):
        acc_ref[b, rows, :] = jnp.where(f > 0, acc_ref[b, rows, :], 0.0) + _dot(h, wd)

    @pl.when(f == n_f - 1)
    def _():
        for b, rows in blocks:
            y_ref[b, rows, :] = (acc_ref[b, rows, :] * gs_ref[b, rows, :][:, 0:1]).astype(y_ref.dtype)


def _ffn(xs, w_gate, w_up, w_down, gslot, layer, tf=512):
    B, E, cap, _ = xs.shape
    return pl.pallas_call(
        _ffn_body,
        grid=(E, EXPERT_FF // tf),
        in_specs=[
            pl.BlockSpec((B, None, cap, D_MODEL), lambda e, f: (0, e, 0, 0)),
            pl.BlockSpec((None, None, D_MODEL, tf), lambda e, f: (layer, e, 0, f)),
            pl.BlockSpec((None, None, D_MODEL, tf), lambda e, f: (layer, e, 0, f)),
            pl.BlockSpec((None, None, tf, D_MODEL), lambda e, f: (layer, e, f, 0)),
            pl.BlockSpec((B, None, cap, LANES), lambda e, f: (0, e, 0, 0)),
        ],
        out_specs=pl.BlockSpec((B, None, cap, D_MODEL), lambda e, f: (0, e, 0, 0)),
        out_shape=jax.ShapeDtypeStruct((B, E, cap, D_MODEL), BF16),
        scratch_shapes=[pltpu.VMEM((B, cap, D_MODEL), F32)],
        compiler_params=_params("arbitrary", "arbitrary"),
        name="ffn",
    )(xs, w_gate, w_up, w_down, gslot)


def _scatter_body(offs_ref, h_ref, y_ref, rk_ref, o_ref, ycat_ref, *, cap):
    tp = h_ref.shape[0]
    b = pl.program_id(0)
    t = pl.program_id(2)
    row_f = lax.broadcasted_iota(jnp.int32, (SLOT_WIN, LANES), 0).astype(F32)

    for cc in range(tp // LANES):
        c = t * (tp // LANES) + cc
        rows = slice(cc * LANES, (cc + 1) * LANES)
        los = [offs_ref[(b * N_EXPERTS + e) * LANES + c] for e in range(N_EXPERTS)]
        his = [offs_ref[(b * N_EXPERTS + e) * LANES + c + 1] for e in range(N_EXPERTS)]
        firsts = [(lo // 16) * 16 for lo in los]
        starts = [_window_start(lo, 0, cap) for lo in los]
        hots = [_slot_onehot(row_f, starts[e], starts[e], rk_ref[e, pl.ds(c, 1), :]) for e in range(N_EXPERTS)]
        for e in range(N_EXPERTS):
            ycat_ref[cc, e * SLOT_WIN:(e + 1) * SLOT_WIN, :] = y_ref[e, pl.ds(starts[e], SLOT_WIN), :]
        stacked = jnp.concatenate([jnp.where(h, 1.0, 0.0).astype(BF16) for h in hots], axis=0)
        o_ref[rows, :] = h_ref[rows, :] + _tn(stacked, ycat_ref[cc])

        overflow = his[0] - firsts[0] > SLOT_WIN
        for e in range(1, N_EXPERTS):
            overflow = overflow | (his[e] - firsts[e] > SLOT_WIN)

        @pl.when(overflow)
        def _():
            for e in range(N_EXPERTS):
                def extra(w, carry, e=e):
                    start = _window_start(firsts[e], w, cap)
                    hot = _slot_onehot(row_f, start, firsts[e] + w * SLOT_WIN, rk_ref[e, pl.ds(c, 1), :])
                    o_ref[rows, :] = o_ref[rows, :] + _tn(
                        jnp.where(hot, 1.0, 0.0).astype(BF16), y_ref[e, pl.ds(start, SLOT_WIN), :])
                    return carry
                lax.fori_loop(1, (his[e] - firsts[e] + SLOT_WIN - 1) // SLOT_WIN, extra, 0)


def _scatter(offs_flat, h3, y, rank4, cap, tp=512, dcols=512):
    B, S, _ = h3.shape
    n_chunks = S // LANES
    return pl.pallas_call(
        functools.partial(_scatter_body, cap=cap),
        grid_spec=pltpu.PrefetchScalarGridSpec(
            num_scalar_prefetch=1,
            grid=(B, D_MODEL // dcols, S // tp),
            in_specs=[
                pl.BlockSpec((None, tp, dcols), lambda b, q, t, offs: (b, t, q)),
                pl.BlockSpec((None, N_EXPERTS, cap, dcols), lambda b, q, t, offs: (b, 0, 0, q)),
                pl.BlockSpec((None, N_EXPERTS, n_chunks, LANES), lambda b, q, t, offs: (b, 0, 0, 0)),
            ],
            out_specs=pl.BlockSpec((None, tp, dcols), lambda b, q, t, offs: (b, t, q)),
            scratch_shapes=[pltpu.VMEM((tp // LANES, N_EXPERTS * SLOT_WIN, dcols), BF16)],
        ),
        out_shape=jax.ShapeDtypeStruct((B, S, D_MODEL), F32),
        compiler_params=_params("arbitrary", "arbitrary", "arbitrary"),
        name="scatter",
    )(offs_flat, h3, y, rank4)


def _ple_body(hrow_ref, g_ref, wg_ref, p_ref, wp_ref, hblk_ref, o_ref, hn_ref):
    tm = hrow_ref.shape[0]

    @pl.when(pl.program_id(1) == 0)
    def _():
        def norm_rows(rows):
            hn_ref[rows, :] = _rms(hrow_ref[rows, :], g_ref[...]).astype(BF16)
        _row_chunks(tm, 128, norm_rows)

    gate = _sigmoid(_dot(hn_ref[...], wg_ref[...].astype(BF16)))
    emb = _dot(p_ref[...].astype(BF16), wp_ref[...].astype(BF16))
    o_ref[...] = hblk_ref[...] + gate * emb


def _ple(h2d, gain, w_gate, p2d, w_ple, layer, tm=1024, tn=512):
    m = h2d.shape[0]
    return pl.pallas_call(
        _ple_body,
        grid=(m // tm, D_MODEL // tn),
        in_specs=[
            pl.BlockSpec((tm, D_MODEL), lambda i, j: (i, 0)),
            pl.BlockSpec((1, D_MODEL), lambda i, j: (0, 0)),
            pl.BlockSpec((None, D_MODEL, tn), lambda i, j: (layer, 0, j)),
            pl.BlockSpec((None, tm, PLE_DIM), lambda i, j: (layer, i, 0)),
            pl.BlockSpec((None, PLE_DIM, tn), lambda i, j: (layer, 0, j)),
            pl.BlockSpec((tm, tn), lambda i, j: (i, j)),
        ],
        out_specs=pl.BlockSpec((tm, tn), lambda i, j: (i, j)),
        out_shape=jax.ShapeDtypeStruct((m, D_MODEL), F32),
        scratch_shapes=[pltpu.VMEM((tm, D_MODEL), BF16)],
        compiler_params=_params("arbitrary", "arbitrary"),
        name="ple",
    )(h2d, gain, w_gate, p2d, w_ple, h2d)


def _final_body(h_ref, g_ref, o_ref):
    o_ref[...] = _rms(h_ref[...], g_ref[...])


def _final_norm(h2d, gain, tm=256):
    m = h2d.shape[0]
    return pl.pallas_call(
        _final_body,
        grid=(m // tm,),
        in_specs=[pl.BlockSpec((tm, D_MODEL), lambda i: (i, 0)), pl.BlockSpec((1, D_MODEL), lambda i: (0, 0))],
        out_specs=pl.BlockSpec((tm, D_MODEL), lambda i: (i, 0)),
        out_shape=jax.ShapeDtypeStruct((m, D_MODEL), F32),
        compiler_params=_params("arbitrary"),
        name="final_norm",
    )(h2d, gain)


def _rope_cs(pos, dim, theta):
    inv = theta ** (-jnp.arange(0, dim, 2, dtype=F32) / dim)
    ang = pos.astype(F32)[:, None] * inv[None, :]
    return jnp.cos(ang), jnp.sin(ang)


def _tables(S):
    cos, sin = _rope_cs(jnp.arange(S), ROPE_DIM, ROPE_THETA)
    z = jnp.zeros_like(sin)
    rest = HEAD_DIM - ROPE_DIM
    a_c = jnp.concatenate([cos, cos, jnp.ones((S, rest), F32)], axis=1)
    a_sp = jnp.concatenate([z, sin, jnp.zeros((S, rest), F32)], axis=1)
    a_sm = jnp.concatenate([-sin, z, jnp.zeros((S, rest), F32)], axis=1)
    rows = S // GRID_W
    rc, rs = _rope_cs(jnp.repeat(jnp.arange(rows), GRID_W), HEAD_DIM // 2, AXIAL_THETA)
    cc, cs = _rope_cs(jnp.tile(jnp.arange(GRID_W), rows), HEAD_DIM // 2, AXIAL_THETA)
    zz = jnp.zeros_like(rs)
    b_c = jnp.concatenate([rc, rc, cc, cc], axis=1)
    b_sp = jnp.concatenate([zz, rs, zz, cs], axis=1)
    b_sm = jnp.concatenate([-rs, zz, -cs, zz], axis=1)
    return (a_c, a_sp, a_sm), (b_c, b_sp, b_sm)


def kernel(x, p, norm_mix, w_in, conv_w, q_norm, k_norm, out_norm_a, out_norm_b, gdn_a_log, gdn_dt_bias,
           gdn_norm, w_out, norm_moe, w_router, w_gate, w_up, w_down, norm_ple, w_ple, w_ple_gate, norm_final):
    B, S, D = x.shape
    depth = w_in.shape[0]
    assert D == D_MODEL and S % 1024 == 0
    cap = EC_CAPACITY * S // N_EXPERTS
    width = GROUP * DELTA_CHUNK
    tabs_a, tabs_b = _tables(S)
    p2d = p.reshape(depth, B * S, PLE_DIM)
    row = lambda v: v.reshape(1, -1)

    w_in_t = jnp.swapaxes(w_in, 1, 2)
    h = x.reshape(B * S, D)
    for i in range(depth):
        proj, small = _proj(h, row(norm_mix[i]), w_in_t, i)
        proj3 = proj.reshape(B, S, D_MAIN)

        y_a = _mix_a(proj3, tabs_a)
        y_b = _mix_b(proj3, tabs_b, row(q_norm[i]), row(k_norm[i]))

        lg = small[:, :D_SMALL].reshape(B, S // width, width, 4, C_HEADS).transpose(0, 3, 4, 1, 2)
        par = jnp.concatenate([gdn_a_log[i], gdn_dt_bias[i]], axis=0)
        par = jnp.broadcast_to(par.T[:, :, None], (C_HEADS, 4, width))
        y_c = _mix_c(proj3, conv_w, lg, par, row(gdn_norm[i]), i)

        h = _out_proj(y_a.reshape(B * S, A_W), y_b.reshape(B * S, B_QW), y_c.reshape(B * S, C_W),
                      row(out_norm_a[i]), row(out_norm_b[i]), w_out, h, i)

        hn, aff_t = _moe_pre(h.reshape(B, S, D), row(norm_moe[i]), w_router[i].T)
        rank, gates, offs = _topk(aff_t, cap)
        offs_flat = offs.reshape(-1)
        rank4 = rank.reshape(B, N_EXPERTS, S // LANES, LANES)
        gate4 = gates.reshape(B, N_EXPERTS, S // LANES, LANES)
        xs, gslot = _gather(offs_flat, hn, rank4, gate4, cap)
        y = _ffn(xs, w_gate, w_up, w_down, gslot, i)
        h = _scatter(offs_flat, h.reshape(B, S, D), y, rank4, cap).reshape(B * S, D)

        h = _ple(h, row(norm_ple[i]), w_ple_gate, p2d, w_ple, i)
    return _final_norm(h, row(norm_final)).reshape(B, S, D)
```

```python
import functools

import jax
import jax.numpy as jnp
from jax import lax
from jax.experimental import pallas as pl
from jax.experimental.pallas import tpu as pltpu

F32 = jnp.float32
BF16 = jnp.bfloat16

D_MODEL = 2048
HEAD_DIM = 128
A_HEADS = 4
B_HEADS = 8
B_KV_HEADS = 2
B_GROUP = B_HEADS // B_KV_HEADS
C_HEADS = 4
A_W = A_HEADS * HEAD_DIM
B_QW = B_HEADS * HEAD_DIM
B_KVW = B_KV_HEADS * HEAD_DIM
C_W = C_HEADS * HEAD_DIM
DILATED_PATTERNS = ((128, 1), (512, 4), (2048, 16))
ROPE_THETA = 500000.0
ROPE_DIM = HEAD_DIM // 4
AXIAL_THETA = 10000.0
GRID_W = 64
CONV_K = 5
DELTA_CHUNK = 64
N_EXPERTS = 16
EC_CAPACITY = 2
EXPERT_FF = D_MODEL // 2
PLE_DIM = 256
NORM_EPS = 1e-6
D_MAIN = 3 * A_W + B_QW + 2 * B_KVW + 4 * C_W
D_SMALL = 4 * C_HEADS

COL_AQ, COL_AK, COL_AV = 0, 4, 8
COL_BQ, COL_BK, COL_BV = 12, 20, 22
COL_CQ, COL_CK, COL_CV, COL_CZ = 24, 28, 32, 36

LANES = 128
VMEM_LIMIT = 56 * 1024 * 1024
NEG_BIG = -1e30
ROW_RESIDENT = dict(pipeline_mode=pl.Buffered(1))
GROUP = 4
PRE_GROUPS = 4
A_BLOCKS = 8
SLOT_WIN = 48
FFN_ROWS = 512


def _params(*sem):
    return pltpu.CompilerParams(dimension_semantics=sem, vmem_limit_bytes=VMEM_LIMIT)


def _nt(a, b):
    return lax.dot_general(a, b, (((1,), (1,)), ((), ())), preferred_element_type=F32)


def _tn(a, b):
    return lax.dot_general(a, b, (((0,), (0,)), ((), ())), preferred_element_type=F32)


def _dot(a, b):
    return jnp.dot(a, b, preferred_element_type=F32)


def _split(x):
    hi = x.astype(BF16)
    lo = (x - hi.astype(F32)).astype(BF16)
    return hi, lo


def _dotb(a, b):
    return _dot(a.astype(BF16), b.astype(BF16))


def _dot3(a, b):
    ah, al = _split(a)
    bh, bl = _split(b)
    return _dot(ah, bh) + (_dot(ah, bl) + _dot(al, bh))


def _rms(x, gain):
    return x * lax.rsqrt(jnp.mean(x * x, axis=-1, keepdims=True) + NORM_EPS) * gain


def _sigmoid(x):
    return 1.0 / (1.0 + jnp.exp(-x))


def _silu(x):
    return x * _sigmoid(x)


def _row_chunks(n_rows, chunk, fn):
    def body(c, carry):
        fn(pl.ds(pl.multiple_of(c * chunk, chunk), chunk))
        return carry
    lax.fori_loop(0, n_rows // chunk, body, 0)


def _proj_body(x_ref, g_ref, w_ref, ws_ref, o_ref, os_ref, xn_ref):
    tm = x_ref.shape[0]

    @pl.when(pl.program_id(1) == 0)
    def _():
        def norm_rows(rows):
            xn_ref[rows, :] = _rms(x_ref[rows, :], g_ref[...]).astype(BF16)
        _row_chunks(tm, 128, norm_rows)
        sub = lax.broadcasted_iota(jnp.int32, ws_ref.shape, 0)
        os_ref[...] = _nt(xn_ref[...], jnp.where(sub < D_SMALL, ws_ref[...], 0.0).astype(BF16))

    o_ref[...] = _nt(xn_ref[...], w_ref[...].astype(BF16)).astype(o_ref.dtype)


def _proj(h2d, gain, w_in_t, layer, tm=2048, tn=512):
    m = h2d.shape[0]
    return pl.pallas_call(
        _proj_body,
        grid=(m // tm, D_MAIN // tn),
        in_specs=[
            pl.BlockSpec((tm, D_MODEL), lambda i, j: (i, 0), **ROW_RESIDENT),
            pl.BlockSpec((1, D_MODEL), lambda i, j: (0, 0)),
            pl.BlockSpec((None, tn, D_MODEL), lambda i, j: (layer, j, 0)),
            pl.BlockSpec((None, LANES, D_MODEL), lambda i, j: (layer, D_MAIN // LANES, 0)),
        ],
        out_specs=[
            pl.BlockSpec((tm, tn), lambda i, j: (i, j)),
            pl.BlockSpec((tm, LANES), lambda i, j: (i, 0)),
        ],
        out_shape=[
            jax.ShapeDtypeStruct((m, D_MAIN), BF16),
            jax.ShapeDtypeStruct((m, LANES), F32),
        ],
        scratch_shapes=[pltpu.VMEM((tm, D_MODEL), BF16)],
        compiler_params=_params("arbitrary", "arbitrary"),
        name="proj",
    )(h2d, gain, w_in_t, w_in_t)


def _mixa_body(q_ref, k_ref, v_ref, c_ref, sp_ref, sm_ref, o_ref,
               qf, kf, vf, qd, kp, vp, acc, den, mrun):
    S = q_ref.shape[0]
    half = 64
    blk = 128
    scale = HEAD_DIM ** -0.5

    def prep(rows):
        def rope(x):
            return (x * c_ref[rows, :] + pltpu.roll(x, ROPE_DIM // 2, 1) * sp_ref[rows, :]
                    + pltpu.roll(x, LANES - ROPE_DIM // 2, 1) * sm_ref[rows, :])
        qf[rows, :] = rope(q_ref[rows, :].astype(F32)) * scale
        kf[rows, :] = rope(k_ref[rows, :].astype(F32))
        vf[rows, :] = v_ref[rows, :].astype(F32)
        acc[rows, :] = jnp.zeros((512, LANES), F32)
        den[rows, :] = jnp.zeros((512, LANES), F32)
        mrun[rows, :] = jnp.full((512, LANES), NEG_BIG, F32)
    _row_chunks(S, 512, prep)

    row = lax.broadcasted_iota(jnp.int32, (blk, 2 * blk), 0)
    col = lax.broadcasted_iota(jnp.int32, (blk, 2 * blk), 1)
    in_band = jnp.abs(col - row - half) <= half

    for window, dil in DILATED_PATTERNS:
        assert window // (2 * dil) == half
        L = S // dil
        nblk = L // blk
        stride_k = L + 2 * half
        for j in range(dil):
            for base in (j * stride_k, j * stride_k + half + L):
                kp[base:base + half, :] = jnp.zeros((half, LANES), BF16)
                vp[base:base + half, :] = jnp.zeros((half, 2 * LANES), BF16)

        def fold(r, carry, dil=dil, L=L, stride_k=stride_k):
            def piece(c, carry2):
                src = pl.ds(r + c * (256 * dil), 256, stride=dil) if dil > 1 else pl.ds(
                    pl.multiple_of(c * 256, 256), 256)
                dst = pl.ds(pl.multiple_of(r * stride_k + half + c * 256, half), 256)
                qd[pl.ds(pl.multiple_of(r * L + c * 256, 256), 256), :] = qf[src, :].astype(BF16)
                kp[dst, :] = kf[src, :].astype(BF16)
                vp[dst, 0:LANES] = vf[src, :].astype(BF16)
                vp[dst, LANES:2 * LANES] = jnp.ones((256, LANES), BF16)
                return carry2
            lax.fori_loop(0, L // 256, piece, 0)
            return carry
        lax.fori_loop(0, dil, fold, 0)

        def blocks(it, carry, dil=dil, L=L, nblk=nblk, stride_k=stride_k):
            items = [it * A_BLOCKS + u for u in range(A_BLOCKS)]
            rs = [w // nblk for w in items]
            nbs = [w % nblk for w in items]
            q0 = [pl.multiple_of(r * L + nb * blk, blk) for r, nb in zip(rs, nbs)]
            k0 = [pl.multiple_of(r * stride_k + nb * blk, blk) for r, nb in zip(rs, nbs)]
            s = [_nt(qd[pl.ds(a, blk), :], kp[pl.ds(b, 2 * blk), :]) for a, b in zip(q0, k0)]
            kpos = [nb * blk - half + col for nb in nbs]
            s = [jnp.where(in_band & (kp_ >= 0) & (kp_ < L), s_, NEG_BIG) for s_, kp_ in zip(s, kpos)]
            m_b = [jnp.max(s_, axis=1, keepdims=True) for s_ in s]
            e = [jnp.exp(s_ - m_).astype(BF16) for s_, m_ in zip(s, m_b)]
            od = [_dot(e_, vp[pl.ds(b, 2 * blk), :]) for e_, b in zip(e, k0)]
            for r, nb, m_, od_ in zip(rs, nbs, m_b, od):
                rows = (pl.ds(r + nb * (blk * dil), blk, stride=dil) if dil > 1
                        else pl.ds(pl.multiple_of(nb * blk, blk), blk))
                m_old = mrun[rows, :]
                m_new = jnp.maximum(m_old, m_)
                a_old = jnp.exp(m_old - m_new)
                a_new = jnp.exp(m_ - m_new)
                acc[rows, :] = acc[rows, :] * a_old + od_[:, 0:LANES] * a_new
                den[rows, :] = den[rows, :] * a_old + od_[:, LANES:2 * LANES] * a_new
                mrun[rows, :] = m_new
            return carry
        lax.fori_loop(0, (dil * nblk) // A_BLOCKS, blocks, 0)

    def finish(rows):
        o_ref[rows, :] = (acc[rows, :] / den[rows, :]).astype(o_ref.dtype)
    _row_chunks(S, 512, finish)


def _mix_a(proj3, tabs):
    B, S, _ = proj3.shape
    max_dil = max(d for _, d in DILATED_PATTERNS)
    head = lambda base: pl.BlockSpec((None, S, LANES), lambda b, h: (b, 0, base + h))
    tab = pl.BlockSpec((S, LANES), lambda b, h: (0, 0))
    return pl.pallas_call(
        _mixa_body,
        grid=(B, A_HEADS),
        in_specs=[head(COL_AQ), head(COL_AK), head(COL_AV), tab, tab, tab],
        out_specs=pl.BlockSpec((None, S, LANES), lambda b, h: (b, 0, h)),
        out_shape=jax.ShapeDtypeStruct((B, S, A_W), BF16),
        scratch_shapes=[
            pltpu.VMEM((S, LANES), F32), pltpu.VMEM((S, LANES), F32), pltpu.VMEM((S, LANES), F32),
            pltpu.VMEM((S, LANES), BF16),
            pltpu.VMEM((S + max_dil * LANES, LANES), BF16),
            pltpu.VMEM((S + max_dil * LANES, 2 * LANES), BF16),
            pltpu.VMEM((S, LANES), F32), pltpu.VMEM((S, LANES), F32), pltpu.VMEM((S, LANES), F32),
        ],
        compiler_params=_params("arbitrary", "arbitrary"),
        name="mix_a",
    )(proj3, proj3, proj3, *tabs)


def _mixb_body(q_ref, k_ref, v_ref, c_ref, sp_ref, sm_ref, qg_ref, kg_ref, o_ref,
               kt_ref, va_ref, q4_ref, m_ref, acc_ref, *, tk):
    S = k_ref.shape[0]
    tq = q_ref.shape[0]
    qi = pl.program_id(2)
    quarter = HEAD_DIM // 4

    def rope(x, rows):
        return (x * c_ref[rows, :] + pltpu.roll(x, quarter, 1) * sp_ref[rows, :]
                + pltpu.roll(x, LANES - quarter, 1) * sm_ref[rows, :])

    @pl.when(qi == 0)
    def _():
        def kv_chunk(c, carry):
            rows = pl.ds(pl.multiple_of(c * tk, tk), tk)
            k = rope(_rms(k_ref[rows, :].astype(F32), kg_ref[...]), rows)
            kt_ref[c] = k.T.astype(BF16)
            va_ref[rows, 0:LANES] = v_ref[rows, :]
            va_ref[rows, LANES:2 * LANES] = jnp.ones((tk, LANES), BF16)
            return carry
        lax.fori_loop(0, S // tk, kv_chunk, 0)

    rows_q = pl.ds(pl.multiple_of(qi * tq, tq), tq)
    for g in range(B_GROUP):
        q = _rms(q_ref[:, g * LANES:(g + 1) * LANES].astype(F32), qg_ref[...])
        q = rope(q, rows_q) * (HEAD_DIM ** -0.5)
        q4_ref[g * tq:(g + 1) * tq, :] = q.astype(BF16)
    m_ref[...] = jnp.full(m_ref.shape, NEG_BIG, F32)
    acc_ref[...] = jnp.zeros(acc_ref.shape, F32)

    def kv_step(c, carry):
        rows = pl.ds(pl.multiple_of(c * tk, tk), tk)
        s = _dot(q4_ref[...], kt_ref[c])
        m_old = m_ref[...]
        m_new = jnp.maximum(m_old, jnp.max(s, axis=1, keepdims=True))
        alpha = jnp.exp(m_old - m_new)
        p = jnp.exp(s - jnp.concatenate([m_new] * (tk // LANES), axis=1))
        acc_ref[...] = acc_ref[...] * jnp.concatenate([alpha, alpha], axis=1) + _dot(p.astype(BF16), va_ref[rows, :])
        m_ref[...] = m_new
        return carry
    lax.fori_loop(0, S // tk, kv_step, 0)

    o = acc_ref[:, 0:LANES] / acc_ref[:, LANES:2 * LANES]
    for g in range(B_GROUP):
        o_ref[:, g * LANES:(g + 1) * LANES] = o[g * tq:(g + 1) * tq, :].astype(o_ref.dtype)


def _mix_b(proj3, tabs, q_gain, k_gain, tq=1024, tk=512):
    B, S, _ = proj3.shape
    qw = B_GROUP * LANES
    tab = pl.BlockSpec((S, LANES), lambda b, h, i: (0, 0))
    gain = pl.BlockSpec((1, LANES), lambda b, h, i: (0, 0))
    return pl.pallas_call(
        functools.partial(_mixb_body, tk=tk),
        grid=(B, B_KV_HEADS, S // tq),
        in_specs=[
            pl.BlockSpec((None, tq, qw), lambda b, h, i: (b, i, COL_BQ // B_GROUP + h)),
            pl.BlockSpec((None, S, LANES), lambda b, h, i: (b, 0, COL_BK + h)),
            pl.BlockSpec((None, S, LANES), lambda b, h, i: (b, 0, COL_BV + h)),
            tab, tab, tab, gain, gain,
        ],
        out_specs=pl.BlockSpec((None, tq, qw), lambda b, h, i: (b, i, h)),
        out_shape=jax.ShapeDtypeStruct((B, S, B_QW), BF16),
        scratch_shapes=[
            pltpu.VMEM((S // tk, LANES, tk), BF16),
            pltpu.VMEM((S, 2 * LANES), BF16),
            pltpu.VMEM((B_GROUP * tq, LANES), BF16),
            pltpu.VMEM((B_GROUP * tq, LANES), F32),
            pltpu.VMEM((B_GROUP * tq, 2 * LANES), F32),
        ],
        compiler_params=_params("arbitrary", "arbitrary", "arbitrary"),
        name="mix_b",
    )(proj3, proj3, proj3, *tabs, q_gain, k_gain)


def _softplus(x):
    return jnp.maximum(x, 0.0) + jnp.log(1.0 + jnp.exp(-jnp.abs(x)))


def _mixc_body(q_ref, k_ref, v_ref, z_ref, cwq_ref, cwk_ref, cwv_ref, lg_ref, par_ref, gn_ref, o_ref,
               xp, qn, kn, vn, tab, qp_ref, op_ref, p_ref, n_ref):
    S = q_ref.shape[0]
    C = DELTA_CHUNK
    n_chunks = S // C
    pad = 8

    for src, cw, dst, kind in ((q_ref, cwq_ref, qn, "q"), (k_ref, cwk_ref, kn, "k"), (v_ref, cwv_ref, vn, "v")):
        xp[0:pad, :] = jnp.zeros((pad, LANES), F32)
        xp[pad + S:2 * pad + S, :] = jnp.zeros((pad, LANES), F32)

        def load(rows, src=src):
            xp[pl.ds(pl.multiple_of(rows.start + pad, pad), 512), :] = src[rows, :].astype(F32)
        _row_chunks(S, 512, load)
        for c in range(S // 512):
            base = pad - CONV_K // 2 + c * 512
            y = xp[base:base + 512, :] * cw[0:1, :]
            for j in range(1, CONV_K):
                y = y + xp[base + j:base + j + 512, :] * cw[j:j + 1, :]
            y = _silu(y)
            if kind != "v":
                y = y * lax.rsqrt(jnp.sum(y * y, axis=-1, keepdims=True) + NORM_EPS)
            if kind == "q":
                y = y * (HEAD_DIM ** -0.5)
            dst[c * 512:(c + 1) * 512, :] = y

    W = GROUP * C
    n_groups = S // W
    set_chunks = PRE_GROUPS * GROUP
    n_sets = n_chunks // set_chunks
    ri = lax.broadcasted_iota(jnp.int32, (W, W), 0)
    ci = lax.broadcasted_iota(jnp.int32, (W, W), 1)
    eye = ri == ci
    eye_f = jnp.where(eye, 1.0, 0.0)
    same_block = [jnp.right_shift(ri, s) == jnp.right_shift(ci, s) for s in (3, 4, 5, 6)]
    same_chunk = same_block[-1]
    chunk_start = jnp.right_shift(ri, 6) * C
    stack_mask = (jnp.right_shift(lax.broadcasted_iota(jnp.int32, (GROUP * HEAD_DIM, W), 0), 7)
                  == jnp.right_shift(lax.broadcasted_iota(jnp.int32, (GROUP * HEAD_DIM, W), 1), 6))
    lane_w = lax.broadcasted_iota(jnp.int32, (1, W), 1)
    for d in range(2):
        g = -jnp.exp(par_ref[d:d + 1, :]) * _softplus(lg_ref[2 + d] + par_ref[2 + d:3 + d, :])
        cum = same_chunk & ((ri <= ci) if d == 0 else (ri >= ci))
        tab[d] = _dot3(g, jnp.where(cum, 1.0, 0.0))
        tab[2 + d] = _sigmoid(lg_ref[d])

    def groups_pre(chains, ring):
        each = lambda fn, *lists: [fn(*args) for args in zip(*lists)]
        ds = [d for _, d in chains]
        rows = [pl.ds(pl.multiple_of(i * W, W), W) for i, _ in chains]
        q = [qn[r, :] for r in rows]
        k = [kn[r, :] for r in rows]
        v = [vn[r, :] for r in rows]
        gr = [tab[d, pl.ds(i, 1), :] for i, d in chains]
        br = [tab[2 + d, pl.ds(i, 1), :] for i, d in chains]
        to_col = lambda mask, r: jnp.sum(jnp.where(mask, r, 0.0), axis=1, keepdims=True)
        gcol = [to_col(eye, g) for g in gr]
        bcol = [to_col(eye, b) for b in br]
        glast = [to_col(ci == (chunk_start + (C - 1) if d == 0 else chunk_start), g) for d, g in zip(ds, gr)]
        incl = [same_chunk & ((ci <= ri) if d == 0 else (ci >= ri)) for d in ds]
        strict = [same_chunk & ((ci < ri) if d == 0 else (ci > ri)) for d in ds]
        decay = each(lambda m, gc, g: jnp.exp(jnp.where(m, gc - g, NEG_BIG)), incl, gcol, gr)
        kb = each(lambda a, b: a * b, k, bcol)
        k16 = [a.astype(BF16) for a in k]
        kk = each(lambda a, b: _nt(a.astype(BF16), b), kb, k16)
        yield
        a = each(lambda m, p, dc: jnp.where(m, p * dc, 0.0), strict, kk, decay)
        eg = [jnp.exp(g) for g in gcol]
        rhs = each(lambda vv, b, kbb, e: jnp.concatenate([vv * b, kbb * e], axis=1), v, bcol, kb, eg)
        x = [jnp.where(same_block[0], -m, 0.0) for m in a]
        s1 = [eye_f + m for m in x]
        x2 = each(_dotb, x, x)
        yield
        x2s1 = each(_dotb, x2, s1)
        x4 = each(_dotb, x2, x2)
        yield
        s2 = each(lambda p, m: p + m, s1, x2s1)
        t = each(lambda p, m, n: p + _dotb(m, n), s2, x4, s2)
        yield
        for lvl in range(1, len(same_block)):
            off = same_block[lvl] & jnp.logical_not(same_block[lvl - 1])
            y = each(lambda m, tt: _dotb(jnp.where(off, m, 0.0), tt), a, t)
            yield
            t = each(lambda tt, yy: tt - _dotb(tt, yy), t, y)
            yield
        r16 = each(lambda tt, r: _dotb(tt, r).astype(BF16), t, rhs)
        yield
        qk = each(lambda a_, b_, dc: _nt(a_.astype(BF16), b_) * dc, q, k16, decay)
        yield
        stacked = each(lambda kk_, gl, gc: jnp.where(
            stack_mask, jnp.concatenate([(kk_ * jnp.exp(gl - gc)).T] * GROUP, axis=0), 0.0).astype(BF16),
            k, glast, gcol)
        np_ = each(_dot, stacked, r16)
        yield
        qo = each(lambda m, r: _dot(m.astype(BF16), r), qk, r16)
        for j, (i, d) in enumerate(chains):
            qp_ref[d, rows[j], :] = (q[j] * eg[j] - qo[j][:, LANES:2 * LANES]).astype(BF16)
            op_ref[d, rows[j], :] = qo[j][:, 0:LANES]
            chunks = pl.ds(ring + (i * GROUP) % set_chunks, GROUP)
            n_ref[d, chunks] = np_[j][:, 0:LANES].reshape(GROUP, HEAD_DIM, HEAD_DIM)
            p_ref[d, chunks] = np_[j][:, LANES:2 * LANES].astype(BF16).reshape(GROUP, HEAD_DIM, HEAD_DIM)

    def set_chains(j):
        return ([(j * PRE_GROUPS + u, 0) for u in range(PRE_GROUPS)]
                + [(n_groups - 1 - (j * PRE_GROUPS + u), 1) for u in range(PRE_GROUPS)])

    def scan_steps(j, state, out):
        st = list(state)
        for step in range(set_chunks):
            i = j * set_chunks + step
            new = []
            for d in range(2):
                n = i if d == 0 else n_chunks - 1 - i
                rows = pl.ds(pl.multiple_of(n * C, C), C)
                gr = tab[d, pl.ds(n // GROUP, 1), :]
                target = (n % GROUP) * C + (C - 1 if d == 0 else 0)
                g_last = jnp.sum(jnp.where(lane_w == target, gr, 0.0), axis=1, keepdims=True)
                s16 = st[d].astype(BF16)
                slot = (j % 2) * set_chunks + n % set_chunks
                op_ref[d, rows, :] = op_ref[d, rows, :] + _dot(qp_ref[d, rows, :], s16)
                new.append(st[d] * jnp.exp(g_last) - _dot(p_ref[d, slot], s16) + n_ref[d, slot])
            st = new
            yield
        out.extend(st)

    def run_together(*gens):
        live = list(gens)
        while live:
            for g in list(live):
                try:
                    next(g)
                except StopIteration:
                    live.remove(g)

    run_together(groups_pre(set_chains(0), 0))

    def piped(j, state):
        out = []
        run_together(groups_pre(set_chains(j), (j % 2) * set_chunks), scan_steps(j - 1, state, out))
        return tuple(out)
    zero = jnp.zeros((HEAD_DIM, HEAD_DIM), F32)
    state = lax.fori_loop(1, n_sets, piped, (zero, zero))
    run_together(scan_steps(n_sets - 1, state, []))

    def finish(rows):
        o = op_ref[0, rows, :] + op_ref[1, rows, :]
        o_ref[rows, :] = (_rms(o, gn_ref[...]) * _silu(z_ref[rows, :].astype(F32))).astype(o_ref.dtype)
    _row_chunks(S, 512, finish)


def _mix_c(proj3, conv_w, logits_rows, par, gn, layer):
    B, S, _ = proj3.shape
    n_chunks = S // DELTA_CHUNK
    width = GROUP * DELTA_CHUNK
    n_groups = S // width
    head = lambda base: pl.BlockSpec((None, S, LANES), lambda b, h: (b, 0, base + h))
    cw = lambda base: pl.BlockSpec((None, CONV_K, LANES), lambda b, h: (layer, 0, base + h))
    return pl.pallas_call(
        _mixc_body,
        grid=(B, C_HEADS),
        in_specs=[
            head(COL_CQ), head(COL_CK), head(COL_CV), head(COL_CZ),
            cw(0), cw(C_HEADS), cw(2 * C_HEADS),
            pl.BlockSpec((None, 4, None, n_groups, width), lambda b, h: (b, 0, h, 0, 0)),
            pl.BlockSpec((None, 4, width), lambda b, h: (h, 0, 0)),
            pl.BlockSpec((1, LANES), lambda b, h: (0, 0)),
        ],
        out_specs=pl.BlockSpec((None, S, LANES), lambda b, h: (b, 0, h)),
        out_shape=jax.ShapeDtypeStruct((B, S, C_W), BF16),
        scratch_shapes=[
            pltpu.VMEM((S + 16, LANES), F32),
            pltpu.VMEM((S, LANES), F32), pltpu.VMEM((S, LANES), F32), pltpu.VMEM((S, LANES), F32),
            pltpu.VMEM((4, n_groups, width), F32),
            pltpu.VMEM((2, S, LANES), BF16),
            pltpu.VMEM((2, S, LANES), F32),
            pltpu.VMEM((2, 2 * PRE_GROUPS * GROUP, HEAD_DIM, HEAD_DIM), BF16),
            pltpu.VMEM((2, 2 * PRE_GROUPS * GROUP, HEAD_DIM, HEAD_DIM), F32),
        ],
        compiler_params=_params("arbitrary", "arbitrary"),
        name="mix_c",
    )(proj3, proj3, proj3, proj3, conv_w, conv_w, conv_w, logits_rows, par, gn)


def _outproj_body(ya_ref, yb_ref, yc_ref, ga_ref, gb_ref, w_ref, h_ref, o_ref, yn_ref):
    tm = ya_ref.shape[0]

    @pl.when(pl.program_id(1) == 0)
    def _():
        def norm_rows(rows):
            yn_ref[rows, 0:A_W] = _rms(ya_ref[rows, :].astype(F32), ga_ref[...]).astype(BF16)
            yn_ref[rows, A_W:A_W + B_QW] = _rms(yb_ref[rows, :].astype(F32), gb_ref[...]).astype(BF16)
            yn_ref[rows, A_W + B_QW:D_MODEL] = yc_ref[rows, :]
        _row_chunks(tm, 128, norm_rows)

    o_ref[...] = h_ref[...] + _dot(yn_ref[...], w_ref[...].astype(BF16))


def _out_proj(ya, yb, yc, ga, gb, w_out, h2d, layer, tm=2048, tn=512):
    m = h2d.shape[0]
    return pl.pallas_call(
        _outproj_body,
        grid=(m // tm, D_MODEL // tn),
        in_specs=[
            pl.BlockSpec((tm, A_W), lambda i, j: (i, 0), **ROW_RESIDENT),
            pl.BlockSpec((tm, B_QW), lambda i, j: (i, 0), **ROW_RESIDENT),
            pl.BlockSpec((tm, C_W), lambda i, j: (i, 0), **ROW_RESIDENT),
            pl.BlockSpec((1, A_W), lambda i, j: (0, 0)),
            pl.BlockSpec((1, B_QW), lambda i, j: (0, 0)),
            pl.BlockSpec((None, D_MODEL, tn), lambda i, j: (layer, 0, j)),
            pl.BlockSpec((tm, tn), lambda i, j: (i, j)),
        ],
        out_specs=pl.BlockSpec((tm, tn), lambda i, j: (i, j)),
        out_shape=jax.ShapeDtypeStruct((m, D_MODEL), F32),
        scratch_shapes=[pltpu.VMEM((tm, D_MODEL), BF16)],
        compiler_params=_params("arbitrary", "arbitrary"),
        name="out_proj",
    )(ya, yb, yc, ga, gb, w_out, h2d)


def _moepre_body(h_ref, g_ref, wr_ref, hn_ref, aff_ref):
    xn = _rms(h_ref[...], g_ref[...])
    hi, lo = _split(xn)
    hn_ref[...] = hi
    whi, wlo = _split(wr_ref[...])
    logits = _nt(whi, hi) + (_nt(whi, lo) + _nt(wlo, hi))
    e = jnp.exp(logits - jnp.max(logits, axis=0, keepdims=True))
    aff_ref[...] = e / jnp.sum(e, axis=0, keepdims=True)


def _moe_pre(h3, gain, w_router_t, tm=1024):
    B, S, _ = h3.shape
    return pl.pallas_call(
        _moepre_body,
        grid=(B, S // tm),
        in_specs=[
            pl.BlockSpec((None, tm, D_MODEL), lambda b, i: (b, i, 0)),
            pl.BlockSpec((1, D_MODEL), lambda b, i: (0, 0)),
            pl.BlockSpec((N_EXPERTS, D_MODEL), lambda b, i: (0, 0)),
        ],
        out_specs=[
            pl.BlockSpec((None, tm, D_MODEL), lambda b, i: (b, i, 0)),
            pl.BlockSpec((None, N_EXPERTS, tm), lambda b, i: (b, 0, i)),
        ],
        out_shape=[
            jax.ShapeDtypeStruct((B, S, D_MODEL), BF16),
            jax.ShapeDtypeStruct((B, N_EXPERTS, S), F32),
        ],
        compiler_params=_params("arbitrary", "arbitrary"),
        name="moe_pre",
    )(h3, gain, w_router_t)


def _topk_body(aff_ref, rank_ref, gate_ref, offs_ref, *, cap):
    x = aff_ref[...]
    E, S = x.shape
    xb = pltpu.bitcast(x, jnp.int32)
    count = lambda mask: jnp.sum(jnp.where(mask, 1.0, 0.0), axis=1, keepdims=True)

    def value_bit(it, t):
        cand = t | jnp.left_shift(jnp.int32(1), 30 - it)
        return jnp.where(count(xb >= cand) >= cap, cand, t)
    thr = lax.fori_loop(0, 31, value_bit, jnp.zeros((E, 1), jnp.int32))

    above = xb > thr
    tied = xb == thr
    need = cap - count(above)
    idx = lax.broadcasted_iota(jnp.int32, (E, S), 1)

    def index_bit(it, j):
        cand = j | jnp.left_shift(jnp.int32(1), 11 - it)
        return jnp.where(count(tied & (idx < cand)) < need, cand, j)
    assert S == 4096
    jmax = lax.fori_loop(0, 12, index_bit, jnp.zeros((E, 1), jnp.int32))
    sel = above | (tied & (idx <= jmax) & (need > 0.0))
    gate_ref[...] = jnp.where(sel, x, 0.0)

    ri = lax.broadcasted_iota(jnp.int32, (LANES, LANES), 0)
    ci = lax.broadcasted_iota(jnp.int32, (LANES, LANES), 1)
    before = jnp.where(ri < ci, 1.0, 0.0).astype(BF16)
    lane = lax.broadcasted_iota(jnp.int32, (E, LANES), 1)
    off = jnp.zeros((E, 1), F32)
    offs = jnp.zeros((E, LANES), F32)
    self_f = jnp.where(sel, 1.0, 0.0)
    for c in range(S // LANES):
        m_c = self_f[:, c * LANES:(c + 1) * LANES]
        rank_c = _dot(m_c.astype(BF16), before)
        rank_ref[:, c * LANES:(c + 1) * LANES] = jnp.where(m_c > 0.0, rank_c + off, -1.0)
        offs = jnp.where(lane == c, off, offs)
        off = off + jnp.sum(m_c, axis=1, keepdims=True)
    offs = jnp.where(lane == S // LANES, off, offs)
    offs_ref[...] = offs.astype(jnp.int32)


def _topk(aff_t, cap):
    B, E, S = aff_t.shape
    spec = pl.BlockSpec((None, E, S), lambda b: (b, 0, 0))
    return pl.pallas_call(
        functools.partial(_topk_body, cap=cap),
        grid=(B,),
        in_specs=[spec],
        out_specs=[spec, spec, pl.BlockSpec((None, E, LANES), lambda b: (b, 0, 0))],
        out_shape=[
            jax.ShapeDtypeStruct((B, E, S), F32),
            jax.ShapeDtypeStruct((B, E, S), F32),
            jax.ShapeDtypeStruct((B, E, LANES), jnp.int32),
        ],
        compiler_params=_params("arbitrary"),
        name="topk",
    )(aff_t)


def _window_start(lo, step, cap):
    return pl.multiple_of(jnp.minimum((lo // 16) * 16 + step * SLOT_WIN, cap - SLOT_WIN), 16)


def _slot_onehot(row_f, start, first_valid, rk):
    slot = row_f + start.astype(F32)
    return (slot == rk) & (slot >= first_valid.astype(F32))


def _gather_body(offs_ref, hn_ref, rk_ref, gt_ref, xs_ref, gs_ref, *, cap):
    S = hn_ref.shape[0]
    b = pl.program_id(0)
    dq = pl.program_id(1)
    acc_ref = xs_ref
    gacc_ref = gs_ref

    def clear(e, carry):
        acc_ref[e] = jnp.zeros(acc_ref.shape[1:], acc_ref.dtype)

        @pl.when(dq == 0)
        def _():
            gacc_ref[e] = jnp.zeros(gacc_ref.shape[1:], F32)
        return carry
    lax.fori_loop(0, N_EXPERTS, clear, 0)

    row_f = lax.broadcasted_iota(jnp.int32, (SLOT_WIN, LANES), 0).astype(F32)

    def chunk(c, carry):
        h_c = hn_ref[pl.ds(pl.multiple_of(c * LANES, LANES), LANES), :]
        los = [offs_ref[(b * N_EXPERTS + e) * LANES + c] for e in range(N_EXPERTS)]
        his = [offs_ref[(b * N_EXPERTS + e) * LANES + c + 1] for e in range(N_EXPERTS)]
        firsts = [(lo // 16) * 16 for lo in los]
        starts = [_window_start(lo, 0, cap) for lo in los]
        hots = [_slot_onehot(row_f, starts[e], starts[e], rk_ref[e, pl.ds(c, 1), :]) for e in range(N_EXPERTS)]
        stacked = jnp.concatenate([jnp.where(h, 1.0, 0.0).astype(BF16) for h in hots], axis=0)
        res = _dot(stacked, h_c)
        for e in range(N_EXPERTS):
            win = pl.ds(starts[e], SLOT_WIN)
            acc_ref[e, win, :] = acc_ref[e, win, :] + res[e * SLOT_WIN:(e + 1) * SLOT_WIN, :].astype(acc_ref.dtype)

        @pl.when(dq == 0)
        def _():
            for e in range(N_EXPERTS):
                win = pl.ds(starts[e], SLOT_WIN)
                gacc_ref[e, win, :] = gacc_ref[e, win, :] + jnp.sum(
                    jnp.where(hots[e], gt_ref[e, pl.ds(c, 1), :], 0.0), axis=1, keepdims=True)

        overflow = his[0] - firsts[0] > SLOT_WIN
        for e in range(1, N_EXPERTS):
            overflow = overflow | (his[e] - firsts[e] > SLOT_WIN)

        @pl.when(overflow)
        def _():
            for e in range(N_EXPERTS):
                def extra(w, carry2, e=e):
                    start = _window_start(firsts[e], w, cap)
                    hot = _slot_onehot(row_f, start, firsts[e] + w * SLOT_WIN, rk_ref[e, pl.ds(c, 1), :])
                    win2 = pl.ds(start, SLOT_WIN)
                    acc_ref[e, win2, :] = acc_ref[e, win2, :] + _dot(
                        jnp.where(hot, 1.0, 0.0).astype(BF16), h_c).astype(acc_ref.dtype)

                    @pl.when(dq == 0)
                    def _():
                        gacc_ref[e, win2, :] = gacc_ref[e, win2, :] + jnp.sum(
                            jnp.where(hot, gt_ref[e, pl.ds(c, 1), :], 0.0), axis=1, keepdims=True)
                    return carry2
                lax.fori_loop(1, (his[e] - firsts[e] + SLOT_WIN - 1) // SLOT_WIN, extra, 0)
        return carry
    lax.fori_loop(0, S // LANES, chunk, 0)


def _gather(offs_flat, hn3, rank4, gate4, cap, dcols=512):
    B, S, _ = hn3.shape
    n_chunks = S // LANES
    rows = pl.BlockSpec((None, N_EXPERTS, n_chunks, LANES), lambda b, q, offs: (b, 0, 0, 0))
    return pl.pallas_call(
        functools.partial(_gather_body, cap=cap),
        grid_spec=pltpu.PrefetchScalarGridSpec(
            num_scalar_prefetch=1,
            grid=(B, D_MODEL // dcols),
            in_specs=[
                pl.BlockSpec((None, S, dcols), lambda b, q, offs: (b, 0, q)),
                rows, rows,
            ],
            out_specs=[
                pl.BlockSpec((None, N_EXPERTS, cap, dcols), lambda b, q, offs: (b, 0, 0, q)),
                pl.BlockSpec((None, N_EXPERTS, cap, LANES), lambda b, q, offs: (b, 0, 0, 0)),
            ],
        ),
        out_shape=[
            jax.ShapeDtypeStruct((B, N_EXPERTS, cap, D_MODEL), BF16),
            jax.ShapeDtypeStruct((B, N_EXPERTS, cap, LANES), F32),
        ],
        compiler_params=_params("arbitrary", "arbitrary"),
        name="gather",
    )(offs_flat, hn3, rank4, gate4)


def _ffn_body(xs_ref, wg_ref, wu_ref, wd_ref, gs_ref, y_ref, acc_ref):
    f = pl.program_id(1)
    n_f = pl.num_programs(1)
    wg = wg_ref[...].astype(BF16)
    wu = wu_ref[...].astype(BF16)
    wd = wd_ref[...].astype(BF16)
    n_b, cap, _ = xs_ref.shape
    blocks = [(b, slice(r, r + FFN_ROWS)) for b in range(n_b) for r in range(0, cap, FFN_ROWS)]
    xs = [xs_ref[b, rows, :] for b, rows in blocks]
    hg = [_dot(x, wg) for x in xs]
    hu = [_dot(x, wu) for x in xs]
    hid = [(_silu(g) * u).astype(BF16) for g, u in zip(hg, hu)]
    for (b, rows), h in zip(blocks, hid):
        acc_ref[b, rows, :] = jnp.where(f > 0, acc_ref[b, rows, :], 0.0) + _dot(h, wd)

    @pl.when(f == n_f - 1)
    def _():
        for b, rows in blocks:
            y_ref[b, rows, :] = (acc_ref[b, rows, :] * gs_ref[b, rows, :][:, 0:1]).astype(y_ref.dtype)


def _ffn(xs, w_gate, w_up, w_down, gslot, layer, tf=512):
    B, E, cap, _ = xs.shape
    return pl.pallas_call(
        _ffn_body,
        grid=(E, EXPERT_FF // tf),
        in_specs=[
            pl.BlockSpec((B, None, cap, D_MODEL), lambda e, f: (0, e, 0, 0)),
            pl.BlockSpec((None, None, D_MODEL, tf), lambda e, f: (layer, e, 0, f)),
            pl.BlockSpec((None, None, D_MODEL, tf), lambda e, f: (layer, e, 0, f)),
            pl.BlockSpec((None, None, tf, D_MODEL), lambda e, f: (layer, e, f, 0)),
            pl.BlockSpec((B, None, cap, LANES), lambda e, f: (0, e, 0, 0)),
        ],
        out_specs=pl.BlockSpec((B, None, cap, D_MODEL), lambda e, f: (0, e, 0, 0)),
        out_shape=jax.ShapeDtypeStruct((B, E, cap, D_MODEL), BF16),
        scratch_shapes=[pltpu.VMEM((B, cap, D_MODEL), F32)],
        compiler_params=_params("arbitrary", "arbitrary"),
        name="ffn",
    )(xs, w_gate, w_up, w_down, gslot)


def _scatter_body(offs_ref, h_ref, y_ref, rk_ref, o_ref, ycat_ref, *, cap):
    tp = h_ref.shape[0]
    b = pl.program_id(0)
    t = pl.program_id(2)
    row_f = lax.broadcasted_iota(jnp.int32, (SLOT_WIN, LANES), 0).astype(F32)

    for cc in range(tp // LANES):
        c = t * (tp // LANES) + cc
        rows = slice(cc * LANES, (cc + 1) * LANES)
        los = [offs_ref[(b * N_EXPERTS + e) * LANES + c] for e in range(N_EXPERTS)]
        his = [offs_ref[(b * N_EXPERTS + e) * LANES + c + 1] for e in range(N_EXPERTS)]
        firsts = [(lo // 16) * 16 for lo in los]
        starts = [_window_start(lo, 0, cap) for lo in los]
        hots = [_slot_onehot(row_f, starts[e], starts[e], rk_ref[e, pl.ds(c, 1), :]) for e in range(N_EXPERTS)]
        for e in range(N_EXPERTS):
            ycat_ref[cc, e * SLOT_WIN:(e + 1) * SLOT_WIN, :] = y_ref[e, pl.ds(starts[e], SLOT_WIN), :]
        stacked = jnp.concatenate([jnp.where(h, 1.0, 0.0).astype(BF16) for h in hots], axis=0)
        o_ref[rows, :] = h_ref[rows, :] + _tn(stacked, ycat_ref[cc])

        overflow = his[0] - firsts[0] > SLOT_WIN
        for e in range(1, N_EXPERTS):
            overflow = overflow | (his[e] - firsts[e] > SLOT_WIN)

        @pl.when(overflow)
        def _():
            for e in range(N_EXPERTS):
                def extra(w, carry, e=e):
                    start = _window_start(firsts[e], w, cap)
                    hot = _slot_onehot(row_f, start, firsts[e] + w * SLOT_WIN, rk_ref[e, pl.ds(c, 1), :])
                    o_ref[rows, :] = o_ref[rows, :] + _tn(
                        jnp.where(hot, 1.0, 0.0).astype(BF16), y_ref[e, pl.ds(start, SLOT_WIN), :])
                    return carry
                lax.fori_loop(1, (his[e] - firsts[e] + SLOT_WIN - 1) // SLOT_WIN, extra, 0)


def _scatter(offs_flat, h3, y, rank4, cap, tp=1024, dcols=512):
    B, S, _ = h3.shape
    n_chunks = S // LANES
    return pl.pallas_call(
        functools.partial(_scatter_body, cap=cap),
        grid_spec=pltpu.PrefetchScalarGridSpec(
            num_scalar_prefetch=1,
            grid=(B, D_MODEL // dcols, S // tp),
            in_specs=[
                pl.BlockSpec((None, tp, dcols), lambda b, q, t, offs: (b, t, q)),
                pl.BlockSpec((None, N_EXPERTS, cap, dcols), lambda b, q, t, offs: (b, 0, 0, q)),
                pl.BlockSpec((None, N_EXPERTS, n_chunks, LANES), lambda b, q, t, offs: (b, 0, 0, 0)),
            ],
            out_specs=pl.BlockSpec((None, tp, dcols), lambda b, q, t, offs: (b, t, q)),
            scratch_shapes=[pltpu.VMEM((tp // LANES, N_EXPERTS * SLOT_WIN, dcols), BF16)],
        ),
        out_shape=jax.ShapeDtypeStruct((B, S, D_MODEL), F32),
        compiler_params=_params("arbitrary", "arbitrary", "arbitrary"),
        name="scatter",
    )(offs_flat, h3, y, rank4)


def _ple_body(hrow_ref, g_ref, wg_ref, p_ref, wp_ref, hblk_ref, o_ref, hn_ref):
    tm = hrow_ref.shape[0]

    @pl.when(pl.program_id(1) == 0)
    def _():
        def norm_rows(rows):
            hn_ref[rows, :] = _rms(hrow_ref[rows, :], g_ref[...]).astype(BF16)
        _row_chunks(tm, 128, norm_rows)

    gate = _sigmoid(_dot(hn_ref[...], wg_ref[...].astype(BF16)))
    emb = _dot(p_ref[...].astype(BF16), wp_ref[...].astype(BF16))
    o_ref[...] = hblk_ref[...] + gate * emb


def _ple(h2d, gain, w_gate, p2d, w_ple, layer, tm=2048, tn=256):
    m = h2d.shape[0]
    return pl.pallas_call(
        _ple_body,
        grid=(m // tm, D_MODEL // tn),
        in_specs=[
            pl.BlockSpec((tm, D_MODEL), lambda i, j: (i, 0), **ROW_RESIDENT),
            pl.BlockSpec((1, D_MODEL), lambda i, j: (0, 0)),
            pl.BlockSpec((None, D_MODEL, tn), lambda i, j: (layer, 0, j)),
            pl.BlockSpec((None, tm, PLE_DIM), lambda i, j: (layer, i, 0), **ROW_RESIDENT),
            pl.BlockSpec((None, PLE_DIM, tn), lambda i, j: (layer, 0, j)),
            pl.BlockSpec((tm, tn), lambda i, j: (i, j)),
        ],
        out_specs=pl.BlockSpec((tm, tn), lambda i, j: (i, j)),
        out_shape=jax.ShapeDtypeStruct((m, D_MODEL), F32),
        scratch_shapes=[pltpu.VMEM((tm, D_MODEL), BF16)],
        compiler_params=_params("arbitrary", "arbitrary"),
        name="ple",
    )(h2d, gain, w_gate, p2d, w_ple, h2d)


def _final_body(h_ref, g_ref, o_ref):
    o_ref[...] = _rms(h_ref[...], g_ref[...])


def _final_norm(h2d, gain, tm=1024):
    m = h2d.shape[0]
    return pl.pallas_call(
        _final_body,
        grid=(m // tm,),
        in_specs=[pl.BlockSpec((tm, D_MODEL), lambda i: (i, 0)), pl.BlockSpec((1, D_MODEL), lambda i: (0, 0))],
        out_specs=pl.BlockSpec((tm, D_MODEL), lambda i: (i, 0)),
        out_shape=jax.ShapeDtypeStruct((m, D_MODEL), F32),
        compiler_params=_params("arbitrary"),
        name="final_norm",
    )(h2d, gain)


def _rope_cs(pos, dim, theta):
    inv = theta ** (-jnp.arange(0, dim, 2, dtype=F32) / dim)
    ang = pos.astype(F32)[:, None] * inv[None, :]
    return jnp.cos(ang), jnp.sin(ang)


def _tables(S):
    cos, sin = _rope_cs(jnp.arange(S), ROPE_DIM, ROPE_THETA)
    z = jnp.zeros_like(sin)
    rest = HEAD_DIM - ROPE_DIM
    a_c = jnp.concatenate([cos, cos, jnp.ones((S, rest), F32)], axis=1)
    a_sp = jnp.concatenate([z, sin, jnp.zeros((S, rest), F32)], axis=1)
    a_sm = jnp.concatenate([-sin, z, jnp.zeros((S, rest), F32)], axis=1)
    rows = S // GRID_W
    rc, rs = _rope_cs(jnp.repeat(jnp.arange(rows), GRID_W), HEAD_DIM // 2, AXIAL_THETA)
    cc, cs = _rope_cs(jnp.tile(jnp.arange(GRID_W), rows), HEAD_DIM // 2, AXIAL_THETA)
    zz = jnp.zeros_like(rs)
    b_c = jnp.concatenate([rc, rc, cc, cc], axis=1)
    b_sp = jnp.concatenate([zz, rs, zz, cs], axis=1)
    b_sm = jnp.concatenate([-rs, zz, -cs, zz], axis=1)
    return (a_c, a_sp, a_sm), (b_c, b_sp, b_sm)


def kernel(x, p, norm_mix, w_in, conv_w, q_norm, k_norm, out_norm_a, out_norm_b, gdn_a_log, gdn_dt_bias,
           gdn_norm, w_out, norm_moe, w_router, w_gate, w_up, w_down, norm_ple, w_ple, w_ple_gate, norm_final):
    B, S, D = x.shape
    depth = w_in.shape[0]
    assert D == D_MODEL and S % 1024 == 0
    cap = EC_CAPACITY * S // N_EXPERTS
    width = GROUP * DELTA_CHUNK
    tabs_a, tabs_b = _tables(S)
    p2d = p.reshape(depth, B * S, PLE_DIM)
    row = lambda v: v.reshape(1, -1)

    w_in_t = jnp.swapaxes(w_in, 1, 2)
    h = x.reshape(B * S, D)
    for i in range(depth):
        proj, small = _proj(h, row(norm_mix[i]), w_in_t, i)
        proj3 = proj.reshape(B, S, D_MAIN)

        y_a = _mix_a(proj3, tabs_a)
        y_b = _mix_b(proj3, tabs_b, row(q_norm[i]), row(k_norm[i]))

        lg = small[:, :D_SMALL].reshape(B, S // width, width, 4, C_HEADS).transpose(0, 3, 4, 1, 2)
        par = jnp.concatenate([gdn_a_log[i], gdn_dt_bias[i]], axis=0)
        par = jnp.broadcast_to(par.T[:, :, None], (C_HEADS, 4, width))
        y_c = _mix_c(proj3, conv_w, lg, par, row(gdn_norm[i]), i)

        h = _out_proj(y_a.reshape(B * S, A_W), y_b.reshape(B * S, B_QW), y_c.reshape(B * S, C_W),
                      row(out_norm_a[i]), row(out_norm_b[i]), w_out, h, i)

        hn, aff_t = _moe_pre(h.reshape(B, S, D), row(norm_moe[i]), w_router[i].T)
        rank, gates, offs = _topk(aff_t, cap)
        offs_flat = offs.reshape(-1)
        rank4 = rank.reshape(B, N_EXPERTS, S // LANES, LANES)
        gate4 = gates.reshape(B, N_EXPERTS, S // LANES, LANES)
        xs, gslot = _gather(offs_flat, hn, rank4, gate4, cap)
        y = _ffn(xs, w_gate, w_up, w_down, gslot, i)
        h = _scatter(offs_flat, h.reshape(B, S, D), y, rank4, cap).reshape(B * S, D)

        h = _ple(h, row(norm_ple[i]), w_ple_gate, p2d, w_ple, i)
    return _final_norm(h, row(norm_final)).reshape(B, S, D)
```

```python
import functools

import jax
import jax.numpy as jnp
from jax import lax
from jax.experimental import pallas as pl
from jax.experimental.pallas import tpu as pltpu

F32 = jnp.float32
BF16 = jnp.bfloat16

D_MODEL = 2048
HEAD_DIM = 128
A_HEADS = 4
B_HEADS = 8
B_KV_HEADS = 2
B_GROUP = B_HEADS // B_KV_HEADS
C_HEADS = 4
A_W = A_HEADS * HEAD_DIM
B_QW = B_HEADS * HEAD_DIM
B_KVW = B_KV_HEADS * HEAD_DIM
C_W = C_HEADS * HEAD_DIM
DILATED_PATTERNS = ((128, 1), (512, 4), (2048, 16))
ROPE_THETA = 500000.0
ROPE_DIM = HEAD_DIM // 4
AXIAL_THETA = 10000.0
GRID_W = 64
CONV_K = 5
DELTA_CHUNK = 64
N_EXPERTS = 16
EC_CAPACITY = 2
EXPERT_FF = D_MODEL // 2
PLE_DIM = 256
NORM_EPS = 1e-6
D_MAIN = 3 * A_W + B_QW + 2 * B_KVW + 4 * C_W
D_SMALL = 4 * C_HEADS

COL_AQ, COL_AK, COL_AV = 0, 4, 8
COL_BQ, COL_BK, COL_BV = 12, 20, 22
COL_CQ, COL_CK, COL_CV, COL_CZ = 24, 28, 32, 36

LANES = 128
VMEM_LIMIT = 56 * 1024 * 1024
NEG_BIG = -1e30
ROW_RESIDENT = dict(pipeline_mode=pl.Buffered(1))
GROUP = 4
PRE_GROUPS = 4
A_BLOCKS = 8
SLOT_WIN = 48
FFN_ROWS = 512
GS_CHUNKS = 8


def _params(*sem):
    return pltpu.CompilerParams(dimension_semantics=sem, vmem_limit_bytes=VMEM_LIMIT)


def _nt(a, b):
    return lax.dot_general(a, b, (((1,), (1,)), ((), ())), preferred_element_type=F32)


def _tn(a, b):
    return lax.dot_general(a, b, (((0,), (0,)), ((), ())), preferred_element_type=F32)


def _dot(a, b):
    return jnp.dot(a, b, preferred_element_type=F32)


def _split(x):
    hi = x.astype(BF16)
    lo = (x - hi.astype(F32)).astype(BF16)
    return hi, lo


def _dotb(a, b):
    return _dot(a.astype(BF16), b.astype(BF16))


def _dot3(a, b):
    ah, al = _split(a)
    bh, bl = _split(b)
    return _dot(ah, bh) + (_dot(ah, bl) + _dot(al, bh))


def _rms(x, gain):
    return x * lax.rsqrt(jnp.mean(x * x, axis=-1, keepdims=True) + NORM_EPS) * gain


def _sigmoid(x):
    return 1.0 / (1.0 + jnp.exp(-x))


def _silu(x):
    return x * _sigmoid(x)


def _row_chunks(n_rows, chunk, fn):
    def body(c, carry):
        fn(pl.ds(pl.multiple_of(c * chunk, chunk), chunk))
        return carry
    lax.fori_loop(0, n_rows // chunk, body, 0)


def _proj_body(x_ref, g_ref, w_ref, ws_ref, o_ref, os_ref, xn_ref):
    tm = x_ref.shape[0]

    @pl.when(pl.program_id(1) == 0)
    def _():
        def norm_rows(rows):
            xn_ref[rows, :] = _rms(x_ref[rows, :], g_ref[...]).astype(BF16)
        _row_chunks(tm, 128, norm_rows)
        sub = lax.broadcasted_iota(jnp.int32, ws_ref.shape, 0)
        os_ref[...] = _nt(xn_ref[...], jnp.where(sub < D_SMALL, ws_ref[...], 0.0).astype(BF16))

    o_ref[...] = _nt(xn_ref[...], w_ref[...].astype(BF16)).astype(o_ref.dtype)


def _proj(h2d, gain, w_in_t, layer, tm=2048, tn=512):
    m = h2d.shape[0]
    return pl.pallas_call(
        _proj_body,
        grid=(m // tm, D_MAIN // tn),
        in_specs=[
            pl.BlockSpec((tm, D_MODEL), lambda i, j: (i, 0), **ROW_RESIDENT),
            pl.BlockSpec((1, D_MODEL), lambda i, j: (0, 0)),
            pl.BlockSpec((None, tn, D_MODEL), lambda i, j: (layer, j, 0)),
            pl.BlockSpec((None, LANES, D_MODEL), lambda i, j: (layer, D_MAIN // LANES, 0)),
        ],
        out_specs=[
            pl.BlockSpec((tm, tn), lambda i, j: (i, j)),
            pl.BlockSpec((tm, LANES), lambda i, j: (i, 0)),
        ],
        out_shape=[
            jax.ShapeDtypeStruct((m, D_MAIN), BF16),
            jax.ShapeDtypeStruct((m, LANES), F32),
        ],
        scratch_shapes=[pltpu.VMEM((tm, D_MODEL), BF16)],
        compiler_params=_params("arbitrary", "arbitrary"),
        name="proj",
    )(h2d, gain, w_in_t, w_in_t)


def _mixa_body(q_ref, k_ref, v_ref, c_ref, sp_ref, sm_ref, o_ref,
               qf, kf, vf, qd, kp, vp, acc, den, mrun):
    S = q_ref.shape[0]
    half = 64
    blk = 128
    scale = HEAD_DIM ** -0.5

    def prep(rows):
        def rope(x):
            return (x * c_ref[rows, :] + pltpu.roll(x, ROPE_DIM // 2, 1) * sp_ref[rows, :]
                    + pltpu.roll(x, LANES - ROPE_DIM // 2, 1) * sm_ref[rows, :])
        qf[rows, :] = rope(q_ref[rows, :].astype(F32)) * scale
        kf[rows, :] = rope(k_ref[rows, :].astype(F32))
        vf[rows, :] = v_ref[rows, :].astype(F32)
        acc[rows, :] = jnp.zeros((512, LANES), F32)
        den[rows, :] = jnp.zeros((512, LANES), F32)
        mrun[rows, :] = jnp.full((512, LANES), NEG_BIG, F32)
    _row_chunks(S, 512, prep)

    row = lax.broadcasted_iota(jnp.int32, (blk, 2 * blk), 0)
    col = lax.broadcasted_iota(jnp.int32, (blk, 2 * blk), 1)
    in_band = jnp.abs(col - row - half) <= half

    for window, dil in DILATED_PATTERNS:
        assert window // (2 * dil) == half
        L = S // dil
        nblk = L // blk
        stride_k = L + 2 * half
        for j in range(dil):
            for base in (j * stride_k, j * stride_k + half + L):
                kp[base:base + half, :] = jnp.zeros((half, LANES), BF16)
                vp[base:base + half, :] = jnp.zeros((half, 2 * LANES), BF16)

        def fold(r, carry, dil=dil, L=L, stride_k=stride_k):
            def piece(c, carry2):
                src = pl.ds(r + c * (256 * dil), 256, stride=dil) if dil > 1 else pl.ds(
                    pl.multiple_of(c * 256, 256), 256)
                dst = pl.ds(pl.multiple_of(r * stride_k + half + c * 256, half), 256)
                qd[pl.ds(pl.multiple_of(r * L + c * 256, 256), 256), :] = qf[src, :].astype(BF16)
                kp[dst, :] = kf[src, :].astype(BF16)
                vp[dst, 0:LANES] = vf[src, :].astype(BF16)
                vp[dst, LANES:2 * LANES] = jnp.ones((256, LANES), BF16)
                return carry2
            lax.fori_loop(0, L // 256, piece, 0)
            return carry
        lax.fori_loop(0, dil, fold, 0)

        def blocks(it, carry, dil=dil, L=L, nblk=nblk, stride_k=stride_k):
            items = [it * A_BLOCKS + u for u in range(A_BLOCKS)]
            rs = [w // nblk for w in items]
            nbs = [w % nblk for w in items]
            q0 = [pl.multiple_of(r * L + nb * blk, blk) for r, nb in zip(rs, nbs)]
            k0 = [pl.multiple_of(r * stride_k + nb * blk, blk) for r, nb in zip(rs, nbs)]
            s = [_nt(qd[pl.ds(a, blk), :], kp[pl.ds(b, 2 * blk), :]) for a, b in zip(q0, k0)]
            kpos = [nb * blk - half + col for nb in nbs]
            s = [jnp.where(in_band & (kp_ >= 0) & (kp_ < L), s_, NEG_BIG) for s_, kp_ in zip(s, kpos)]
            m_b = [jnp.max(s_, axis=1, keepdims=True) for s_ in s]
            e = [jnp.exp(s_ - m_).astype(BF16) for s_, m_ in zip(s, m_b)]
            od = [_dot(e_, vp[pl.ds(b, 2 * blk), :]) for e_, b in zip(e, k0)]
            for r, nb, m_, od_ in zip(rs, nbs, m_b, od):
                rows = (pl.ds(r + nb * (blk * dil), blk, stride=dil) if dil > 1
                        else pl.ds(pl.multiple_of(nb * blk, blk), blk))
                m_old = mrun[rows, :]
                m_new = jnp.maximum(m_old, m_)
                a_old = jnp.exp(m_old - m_new)
                a_new = jnp.exp(m_ - m_new)
                acc[rows, :] = acc[rows, :] * a_old + od_[:, 0:LANES] * a_new
                den[rows, :] = den[rows, :] * a_old + od_[:, LANES:2 * LANES] * a_new
                mrun[rows, :] = m_new
            return carry
        lax.fori_loop(0, (dil * nblk) // A_BLOCKS, blocks, 0)

    def finish(rows):
        o_ref[rows, :] = (acc[rows, :] / den[rows, :]).astype(o_ref.dtype)
    _row_chunks(S, 512, finish)


def _mix_a(proj3, tabs):
    B, S, _ = proj3.shape
    max_dil = max(d for _, d in DILATED_PATTERNS)
    head = lambda base: pl.BlockSpec((None, S, LANES), lambda b, h: (b, 0, base + h))
    tab = pl.BlockSpec((S, LANES), lambda b, h: (0, 0))
    return pl.pallas_call(
        _mixa_body,
        grid=(B, A_HEADS),
        in_specs=[head(COL_AQ), head(COL_AK), head(COL_AV), tab, tab, tab],
        out_specs=pl.BlockSpec((None, S, LANES), lambda b, h: (b, 0, h)),
        out_shape=jax.ShapeDtypeStruct((B, S, A_W), BF16),
        scratch_shapes=[
            pltpu.VMEM((S, LANES), F32), pltpu.VMEM((S, LANES), F32), pltpu.VMEM((S, LANES), F32),
            pltpu.VMEM((S, LANES), BF16),
            pltpu.VMEM((S + max_dil * LANES, LANES), BF16),
            pltpu.VMEM((S + max_dil * LANES, 2 * LANES), BF16),
            pltpu.VMEM((S, LANES), F32), pltpu.VMEM((S, LANES), F32), pltpu.VMEM((S, LANES), F32),
        ],
        compiler_params=_params("arbitrary", "arbitrary"),
        name="mix_a",
    )(proj3, proj3, proj3, *tabs)


def _mixb_body(q_ref, k_ref, v_ref, c_ref, sp_ref, sm_ref, qg_ref, kg_ref, o_ref,
               kt_ref, va_ref, q4_ref, m_ref, acc_ref, *, tk):
    S = k_ref.shape[0]
    tq = q_ref.shape[0]
    qi = pl.program_id(2)
    quarter = HEAD_DIM // 4

    def rope(x, rows):
        return (x * c_ref[rows, :] + pltpu.roll(x, quarter, 1) * sp_ref[rows, :]
                + pltpu.roll(x, LANES - quarter, 1) * sm_ref[rows, :])

    @pl.when(qi == 0)
    def _():
        def kv_chunk(c, carry):
            rows = pl.ds(pl.multiple_of(c * tk, tk), tk)
            k = rope(_rms(k_ref[rows, :].astype(F32), kg_ref[...]), rows)
            kt_ref[c] = k.T.astype(BF16)
            va_ref[rows, 0:LANES] = v_ref[rows, :]
            va_ref[rows, LANES:2 * LANES] = jnp.ones((tk, LANES), BF16)
            return carry
        lax.fori_loop(0, S // tk, kv_chunk, 0)

    rows_q = pl.ds(pl.multiple_of(qi * tq, tq), tq)
    for g in range(B_GROUP):
        q = _rms(q_ref[:, g * LANES:(g + 1) * LANES].astype(F32), qg_ref[...])
        q = rope(q, rows_q) * (HEAD_DIM ** -0.5)
        q4_ref[g * tq:(g + 1) * tq, :] = q.astype(BF16)
    m_ref[...] = jnp.full(m_ref.shape, NEG_BIG, F32)
    acc_ref[...] = jnp.zeros(acc_ref.shape, F32)

    def kv_step(c, carry):
        rows = pl.ds(pl.multiple_of(c * tk, tk), tk)
        s = _dot(q4_ref[...], kt_ref[c])
        m_old = m_ref[...]
        m_new = jnp.maximum(m_old, jnp.max(s, axis=1, keepdims=True))
        alpha = jnp.exp(m_old - m_new)
        p = jnp.exp(s - jnp.concatenate([m_new] * (tk // LANES), axis=1))
        acc_ref[...] = acc_ref[...] * jnp.concatenate([alpha, alpha], axis=1) + _dot(p.astype(BF16), va_ref[rows, :])
        m_ref[...] = m_new
        return carry
    lax.fori_loop(0, S // tk, kv_step, 0)

    o = acc_ref[:, 0:LANES] / acc_ref[:, LANES:2 * LANES]
    for g in range(B_GROUP):
        o_ref[:, g * LANES:(g + 1) * LANES] = o[g * tq:(g + 1) * tq, :].astype(o_ref.dtype)


def _mix_b(proj3, tabs, q_gain, k_gain, tq=1024, tk=512):
    B, S, _ = proj3.shape
    qw = B_GROUP * LANES
    tab = pl.BlockSpec((S, LANES), lambda b, h, i: (0, 0))
    gain = pl.BlockSpec((1, LANES), lambda b, h, i: (0, 0))
    return pl.pallas_call(
        functools.partial(_mixb_body, tk=tk),
        grid=(B, B_KV_HEADS, S // tq),
        in_specs=[
            pl.BlockSpec((None, tq, qw), lambda b, h, i: (b, i, COL_BQ // B_GROUP + h)),
            pl.BlockSpec((None, S, LANES), lambda b, h, i: (b, 0, COL_BK + h)),
            pl.BlockSpec((None, S, LANES), lambda b, h, i: (b, 0, COL_BV + h)),
            tab, tab, tab, gain, gain,
        ],
        out_specs=pl.BlockSpec((None, tq, qw), lambda b, h, i: (b, i, h)),
        out_shape=jax.ShapeDtypeStruct((B, S, B_QW), BF16),
        scratch_shapes=[
            pltpu.VMEM((S // tk, LANES, tk), BF16),
            pltpu.VMEM((S, 2 * LANES), BF16),
            pltpu.VMEM((B_GROUP * tq, LANES), BF16),
            pltpu.VMEM((B_GROUP * tq, LANES), F32),
            pltpu.VMEM((B_GROUP * tq, 2 * LANES), F32),
        ],
        compiler_params=_params("arbitrary", "arbitrary", "arbitrary"),
        name="mix_b",
    )(proj3, proj3, proj3, *tabs, q_gain, k_gain)


def _softplus(x):
    return jnp.maximum(x, 0.0) + jnp.log(1.0 + jnp.exp(-jnp.abs(x)))


def _mixc_body(q_ref, k_ref, v_ref, z_ref, cwq_ref, cwk_ref, cwv_ref, lg_ref, par_ref, gn_ref, o_ref,
               xp, qn, kn, vn, tab, qp_ref, op_ref, p_ref, n_ref):
    S = q_ref.shape[0]
    C = DELTA_CHUNK
    n_chunks = S // C
    pad = 8

    for src, cw, dst, kind in ((q_ref, cwq_ref, qn, "q"), (k_ref, cwk_ref, kn, "k"), (v_ref, cwv_ref, vn, "v")):
        xp[0:pad, :] = jnp.zeros((pad, LANES), F32)
        xp[pad + S:2 * pad + S, :] = jnp.zeros((pad, LANES), F32)

        def load(rows, src=src):
            xp[pl.ds(pl.multiple_of(rows.start + pad, pad), 512), :] = src[rows, :].astype(F32)
        _row_chunks(S, 512, load)
        for c in range(S // 512):
            base = pad - CONV_K // 2 + c * 512
            y = xp[base:base + 512, :] * cw[0:1, :]
            for j in range(1, CONV_K):
                y = y + xp[base + j:base + j + 512, :] * cw[j:j + 1, :]
            y = _silu(y)
            if kind != "v":
                y = y * lax.rsqrt(jnp.sum(y * y, axis=-1, keepdims=True) + NORM_EPS)
            if kind == "q":
                y = y * (HEAD_DIM ** -0.5)
            dst[c * 512:(c + 1) * 512, :] = y

    W = GROUP * C
    n_groups = S // W
    set_chunks = PRE_GROUPS * GROUP
    n_sets = n_chunks // set_chunks
    ri = lax.broadcasted_iota(jnp.int32, (W, W), 0)
    ci = lax.broadcasted_iota(jnp.int32, (W, W), 1)
    eye = ri == ci
    eye_f = jnp.where(eye, 1.0, 0.0)
    same_block = [jnp.right_shift(ri, s) == jnp.right_shift(ci, s) for s in (3, 4, 5, 6)]
    same_chunk = same_block[-1]
    chunk_start = jnp.right_shift(ri, 6) * C
    stack_mask = (jnp.right_shift(lax.broadcasted_iota(jnp.int32, (GROUP * HEAD_DIM, W), 0), 7)
                  == jnp.right_shift(lax.broadcasted_iota(jnp.int32, (GROUP * HEAD_DIM, W), 1), 6))
    lane_w = lax.broadcasted_iota(jnp.int32, (1, W), 1)
    for d in range(2):
        g = -jnp.exp(par_ref[d:d + 1, :]) * _softplus(lg_ref[2 + d] + par_ref[2 + d:3 + d, :])
        cum = same_chunk & ((ri <= ci) if d == 0 else (ri >= ci))
        tab[d] = _dot3(g, jnp.where(cum, 1.0, 0.0))
        tab[2 + d] = _sigmoid(lg_ref[d])

    def groups_pre(chains, ring):
        each = lambda fn, *lists: [fn(*args) for args in zip(*lists)]
        ds = [d for _, d in chains]
        rows = [pl.ds(pl.multiple_of(i * W, W), W) for i, _ in chains]
        q = [qn[r, :] for r in rows]
        k = [kn[r, :] for r in rows]
        v = [vn[r, :] for r in rows]
        gr = [tab[d, pl.ds(i, 1), :] for i, d in chains]
        br = [tab[2 + d, pl.ds(i, 1), :] for i, d in chains]
        to_col = lambda mask, r: jnp.sum(jnp.where(mask, r, 0.0), axis=1, keepdims=True)
        gcol = [to_col(eye, g) for g in gr]
        bcol = [to_col(eye, b) for b in br]
        glast = [to_col(ci == (chunk_start + (C - 1) if d == 0 else chunk_start), g) for d, g in zip(ds, gr)]
        incl = [same_chunk & ((ci <= ri) if d == 0 else (ci >= ri)) for d in ds]
        strict = [same_chunk & ((ci < ri) if d == 0 else (ci > ri)) for d in ds]
        decay = each(lambda m, gc, g: jnp.exp(jnp.where(m, gc - g, NEG_BIG)), incl, gcol, gr)
        kb = each(lambda a, b: a * b, k, bcol)
        k16 = [a.astype(BF16) for a in k]
        kk = each(lambda a, b: _nt(a.astype(BF16), b), kb, k16)
        yield
        a = each(lambda m, p, dc: jnp.where(m, p * dc, 0.0), strict, kk, decay)
        eg = [jnp.exp(g) for g in gcol]
        rhs = each(lambda vv, b, kbb, e: jnp.concatenate([vv * b, kbb * e], axis=1), v, bcol, kb, eg)
        x = [jnp.where(same_block[0], -m, 0.0) for m in a]
        s1 = [eye_f + m for m in x]
        x2 = each(_dotb, x, x)
        yield
        x2s1 = each(_dotb, x2, s1)
        x4 = each(_dotb, x2, x2)
        yield
        s2 = each(lambda p, m: p + m, s1, x2s1)
        t = each(lambda p, m, n: p + _dotb(m, n), s2, x4, s2)
        yield
        for lvl in range(1, len(same_block)):
            off = same_block[lvl] & jnp.logical_not(same_block[lvl - 1])
            y = each(lambda m, tt: _dotb(jnp.where(off, m, 0.0), tt), a, t)
            yield
            t = each(lambda tt, yy: tt - _dotb(tt, yy), t, y)
            yield
        r16 = each(lambda tt, r: _dotb(tt, r).astype(BF16), t, rhs)
        yield
        qk = each(lambda a_, b_, dc: _nt(a_.astype(BF16), b_) * dc, q, k16, decay)
        yield
        stacked = each(lambda kk_, gl, gc: jnp.where(
            stack_mask, jnp.concatenate([(kk_ * jnp.exp(gl - gc)).T] * GROUP, axis=0), 0.0).astype(BF16),
            k, glast, gcol)
        np_ = each(_dot, stacked, r16)
        yield
        qo = each(lambda m, r: _dot(m.astype(BF16), r), qk, r16)
        for j, (i, d) in enumerate(chains):
            qp_ref[d, rows[j], :] = (q[j] * eg[j] - qo[j][:, LANES:2 * LANES]).astype(BF16)
            op_ref[d, rows[j], :] = qo[j][:, 0:LANES]
            chunks = pl.ds(ring + (i * GROUP) % set_chunks, GROUP)
            n_ref[d, chunks] = np_[j][:, 0:LANES].reshape(GROUP, HEAD_DIM, HEAD_DIM)
            p_ref[d, chunks] = np_[j][:, LANES:2 * LANES].astype(BF16).reshape(GROUP, HEAD_DIM, HEAD_DIM)

    def set_chains(j):
        return ([(j * PRE_GROUPS + u, 0) for u in range(PRE_GROUPS)]
                + [(n_groups - 1 - (j * PRE_GROUPS + u), 1) for u in range(PRE_GROUPS)])

    def scan_steps(j, state, out):
        st = list(state)
        for step in range(set_chunks):
            i = j * set_chunks + step
            new = []
            for d in range(2):
                n = i if d == 0 else n_chunks - 1 - i
                rows = pl.ds(pl.multiple_of(n * C, C), C)
                gr = tab[d, pl.ds(n // GROUP, 1), :]
                target = (n % GROUP) * C + (C - 1 if d == 0 else 0)
                g_last = jnp.sum(jnp.where(lane_w == target, gr, 0.0), axis=1, keepdims=True)
                s16 = st[d].astype(BF16)
                slot = (j % 2) * set_chunks + n % set_chunks
                op_ref[d, rows, :] = op_ref[d, rows, :] + _dot(qp_ref[d, rows, :], s16)
                new.append(st[d] * jnp.exp(g_last) - _dot(p_ref[d, slot], s16) + n_ref[d, slot])
            st = new
            yield
        out.extend(st)

    def run_together(*gens):
        live = list(gens)
        while live:
            for g in list(live):
                try:
                    next(g)
                except StopIteration:
                    live.remove(g)

    run_together(groups_pre(set_chains(0), 0))

    def piped(j, state):
        out = []
        run_together(groups_pre(set_chains(j), (j % 2) * set_chunks), scan_steps(j - 1, state, out))
        return tuple(out)
    zero = jnp.zeros((HEAD_DIM, HEAD_DIM), F32)
    state = lax.fori_loop(1, n_sets, piped, (zero, zero))
    run_together(scan_steps(n_sets - 1, state, []))

    def finish(rows):
        o = op_ref[0, rows, :] + op_ref[1, rows, :]
        o_ref[rows, :] = (_rms(o, gn_ref[...]) * _silu(z_ref[rows, :].astype(F32))).astype(o_ref.dtype)
    _row_chunks(S, 512, finish)


def _mix_c(proj3, conv_w, logits_rows, par, gn, layer):
    B, S, _ = proj3.shape
    n_chunks = S // DELTA_CHUNK
    width = GROUP * DELTA_CHUNK
    n_groups = S // width
    head = lambda base: pl.BlockSpec((None, S, LANES), lambda b, h: (b, 0, base + h))
    cw = lambda base: pl.BlockSpec((None, CONV_K, LANES), lambda b, h: (layer, 0, base + h))
    return pl.pallas_call(
        _mixc_body,
        grid=(B, C_HEADS),
        in_specs=[
            head(COL_CQ), head(COL_CK), head(COL_CV), head(COL_CZ),
            cw(0), cw(C_HEADS), cw(2 * C_HEADS),
            pl.BlockSpec((None, 4, None, n_groups, width), lambda b, h: (b, 0, h, 0, 0)),
            pl.BlockSpec((None, 4, width), lambda b, h: (h, 0, 0)),
            pl.BlockSpec((1, LANES), lambda b, h: (0, 0)),
        ],
        out_specs=pl.BlockSpec((None, S, LANES), lambda b, h: (b, 0, h)),
        out_shape=jax.ShapeDtypeStruct((B, S, C_W), BF16),
        scratch_shapes=[
            pltpu.VMEM((S + 16, LANES), F32),
            pltpu.VMEM((S, LANES), F32), pltpu.VMEM((S, LANES), F32), pltpu.VMEM((S, LANES), F32),
            pltpu.VMEM((4, n_groups, width), F32),
            pltpu.VMEM((2, S, LANES), BF16),
            pltpu.VMEM((2, S, LANES), F32),
            pltpu.VMEM((2, 2 * PRE_GROUPS * GROUP, HEAD_DIM, HEAD_DIM), BF16),
            pltpu.VMEM((2, 2 * PRE_GROUPS * GROUP, HEAD_DIM, HEAD_DIM), F32),
        ],
        compiler_params=_params("arbitrary", "arbitrary"),
        name="mix_c",
    )(proj3, proj3, proj3, proj3, conv_w, conv_w, conv_w, logits_rows, par, gn)


def _outproj_body(ya_ref, yb_ref, yc_ref, ga_ref, gb_ref, w_ref, h_ref, o_ref, yn_ref):
    tm = ya_ref.shape[0]

    @pl.when(pl.program_id(1) == 0)
    def _():
        def norm_rows(rows):
            yn_ref[rows, 0:A_W] = _rms(ya_ref[rows, :].astype(F32), ga_ref[...]).astype(BF16)
            yn_ref[rows, A_W:A_W + B_QW] = _rms(yb_ref[rows, :].astype(F32), gb_ref[...]).astype(BF16)
            yn_ref[rows, A_W + B_QW:D_MODEL] = yc_ref[rows, :]
        _row_chunks(tm, 128, norm_rows)

    o_ref[...] = h_ref[...] + _dot(yn_ref[...], w_ref[...].astype(BF16))


def _out_proj(ya, yb, yc, ga, gb, w_out, h2d, layer, tm=2048, tn=512):
    m = h2d.shape[0]
    return pl.pallas_call(
        _outproj_body,
        grid=(m // tm, D_MODEL // tn),
        in_specs=[
            pl.BlockSpec((tm, A_W), lambda i, j: (i, 0), **ROW_RESIDENT),
            pl.BlockSpec((tm, B_QW), lambda i, j: (i, 0), **ROW_RESIDENT),
            pl.BlockSpec((tm, C_W), lambda i, j: (i, 0), **ROW_RESIDENT),
            pl.BlockSpec((1, A_W), lambda i, j: (0, 0)),
            pl.BlockSpec((1, B_QW), lambda i, j: (0, 0)),
            pl.BlockSpec((None, D_MODEL, tn), lambda i, j: (layer, 0, j)),
            pl.BlockSpec((tm, tn), lambda i, j: (i, j)),
        ],
        out_specs=pl.BlockSpec((tm, tn), lambda i, j: (i, j)),
        out_shape=jax.ShapeDtypeStruct((m, D_MODEL), F32),
        scratch_shapes=[pltpu.VMEM((tm, D_MODEL), BF16)],
        compiler_params=_params("arbitrary", "arbitrary"),
        name="out_proj",
    )(ya, yb, yc, ga, gb, w_out, h2d)


def _moepre_body(h_ref, g_ref, wr_ref, hn_ref, aff_ref):
    xn = _rms(h_ref[...], g_ref[...])
    hi, lo = _split(xn)
    hn_ref[...] = hi
    whi, wlo = _split(wr_ref[...])
    logits = _nt(whi, hi) + (_nt(whi, lo) + _nt(wlo, hi))
    e = jnp.exp(logits - jnp.max(logits, axis=0, keepdims=True))
    aff_ref[...] = e / jnp.sum(e, axis=0, keepdims=True)


def _moe_pre(h3, gain, w_router_t, tm=1024):
    B, S, _ = h3.shape
    return pl.pallas_call(
        _moepre_body,
        grid=(B, S // tm),
        in_specs=[
            pl.BlockSpec((None, tm, D_MODEL), lambda b, i: (b, i, 0)),
            pl.BlockSpec((1, D_MODEL), lambda b, i: (0, 0)),
            pl.BlockSpec((N_EXPERTS, D_MODEL), lambda b, i: (0, 0)),
        ],
        out_specs=[
            pl.BlockSpec((None, tm, D_MODEL), lambda b, i: (b, i, 0)),
            pl.BlockSpec((None, N_EXPERTS, tm), lambda b, i: (b, 0, i)),
        ],
        out_shape=[
            jax.ShapeDtypeStruct((B, S, D_MODEL), BF16),
            jax.ShapeDtypeStruct((B, N_EXPERTS, S), F32),
        ],
        compiler_params=_params("arbitrary", "arbitrary"),
        name="moe_pre",
    )(h3, gain, w_router_t)


def _topk_body(aff_ref, rank_ref, gate_ref, offs_ref, *, cap):
    x = aff_ref[...]
    E, S = x.shape
    xb = pltpu.bitcast(x, jnp.int32)
    count = lambda mask: jnp.sum(jnp.where(mask, 1.0, 0.0), axis=1, keepdims=True)

    def value_bit(it, t):
        cand = t | jnp.left_shift(jnp.int32(1), 30 - it)
        return jnp.where(count(xb >= cand) >= cap, cand, t)
    thr = lax.fori_loop(0, 31, value_bit, jnp.zeros((E, 1), jnp.int32))

    above = xb > thr
    tied = xb == thr
    need = cap - count(above)
    idx = lax.broadcasted_iota(jnp.int32, (E, S), 1)

    def index_bit(it, j):
        cand = j | jnp.left_shift(jnp.int32(1), 11 - it)
        return jnp.where(count(tied & (idx < cand)) < need, cand, j)
    assert S == 4096
    jmax = lax.fori_loop(0, 12, index_bit, jnp.zeros((E, 1), jnp.int32))
    sel = above | (tied & (idx <= jmax) & (need > 0.0))
    gate_ref[...] = jnp.where(sel, x, 0.0)

    ri = lax.broadcasted_iota(jnp.int32, (LANES, LANES), 0)
    ci = lax.broadcasted_iota(jnp.int32, (LANES, LANES), 1)
    before = jnp.where(ri < ci, 1.0, 0.0).astype(BF16)
    lane = lax.broadcasted_iota(jnp.int32, (E, LANES), 1)
    off = jnp.zeros((E, 1), F32)
    offs = jnp.zeros((E, LANES), F32)
    self_f = jnp.where(sel, 1.0, 0.0)
    for c in range(S // LANES):
        m_c = self_f[:, c * LANES:(c + 1) * LANES]
        rank_c = _dot(m_c.astype(BF16), before)
        rank_ref[:, c * LANES:(c + 1) * LANES] = jnp.where(m_c > 0.0, rank_c + off, -1.0)
        offs = jnp.where(lane == c, off, offs)
        off = off + jnp.sum(m_c, axis=1, keepdims=True)
    offs = jnp.where(lane == S // LANES, off, offs)
    offs_ref[...] = offs.astype(jnp.int32)


def _topk(aff_t, cap):
    B, E, S = aff_t.shape
    spec = pl.BlockSpec((None, E, S), lambda b: (b, 0, 0))
    return pl.pallas_call(
        functools.partial(_topk_body, cap=cap),
        grid=(B,),
        in_specs=[spec],
        out_specs=[spec, spec, pl.BlockSpec((None, E, LANES), lambda b: (b, 0, 0))],
        out_shape=[
            jax.ShapeDtypeStruct((B, E, S), F32),
            jax.ShapeDtypeStruct((B, E, S), F32),
            jax.ShapeDtypeStruct((B, E, LANES), jnp.int32),
        ],
        compiler_params=_params("arbitrary"),
        name="topk",
    )(aff_t)


def _window_start(lo, step, cap):
    return pl.multiple_of(jnp.minimum((lo // 16) * 16 + step * SLOT_WIN, cap - SLOT_WIN), 16)


def _slot_onehot(row_f, start, first_valid, rk):
    slot = row_f + start.astype(F32)
    return (slot == rk) & (slot >= first_valid.astype(F32))


def _gather_body(offs_ref, hn_ref, rk_ref, gt_ref, xs_ref, gs_ref, *, cap):
    S = hn_ref.shape[0]
    b = pl.program_id(0)
    dq = pl.program_id(1)
    acc_ref = xs_ref
    gacc_ref = gs_ref

    def clear(e, carry):
        acc_ref[e] = jnp.zeros(acc_ref.shape[1:], acc_ref.dtype)

        @pl.when(dq == 0)
        def _():
            gacc_ref[e] = jnp.zeros(gacc_ref.shape[1:], F32)
        return carry
    lax.fori_loop(0, N_EXPERTS, clear, 0)

    row_f = lax.broadcasted_iota(jnp.int32, (SLOT_WIN, LANES), 0).astype(F32)

    experts = range(N_EXPERTS)

    def chunks(it, carry):
        cs = [it * GS_CHUNKS + u for u in range(GS_CHUNKS)]
        h_c = [hn_ref[pl.ds(pl.multiple_of(c * LANES, LANES), LANES), :] for c in cs]
        los = [[offs_ref[(b * N_EXPERTS + e) * LANES + c] for e in experts] for c in cs]
        his = [[offs_ref[(b * N_EXPERTS + e) * LANES + c + 1] for e in experts] for c in cs]
        firsts = [[(lo // 16) * 16 for lo in row] for row in los]
        starts = [[_window_start(lo, 0, cap) for lo in row] for row in los]
        hots = [[_slot_onehot(row_f, starts[u][e], starts[u][e], rk_ref[e, pl.ds(c, 1), :]) for e in experts]
                for u, c in enumerate(cs)]
        stacked = [jnp.concatenate([jnp.where(h, 1.0, 0.0).astype(BF16) for h in row], axis=0) for row in hots]
        res = [_dot(st, h) for st, h in zip(stacked, h_c)]
        for u in range(GS_CHUNKS):
            for e in experts:
                win = pl.ds(starts[u][e], SLOT_WIN)
                acc_ref[e, win, :] = acc_ref[e, win, :] + res[u][e * SLOT_WIN:(e + 1) * SLOT_WIN, :].astype(
                    acc_ref.dtype)

        @pl.when(dq == 0)
        def _():
            for u, c in enumerate(cs):
                for e in experts:
                    win = pl.ds(starts[u][e], SLOT_WIN)
                    gacc_ref[e, win, :] = gacc_ref[e, win, :] + jnp.sum(
                        jnp.where(hots[u][e], gt_ref[e, pl.ds(c, 1), :], 0.0), axis=1, keepdims=True)

        overflow = functools.reduce(
            lambda x, y: x | y, [his[u][e] - firsts[u][e] > SLOT_WIN for u in range(GS_CHUNKS) for e in experts])

        @pl.when(overflow)
        def _():
            for u, c in enumerate(cs):
                for e in experts:
                    def extra(w, carry2, u=u, c=c, e=e):
                        start = _window_start(firsts[u][e], w, cap)
                        hot = _slot_onehot(row_f, start, firsts[u][e] + w * SLOT_WIN, rk_ref[e, pl.ds(c, 1), :])
                        win2 = pl.ds(start, SLOT_WIN)
                        acc_ref[e, win2, :] = acc_ref[e, win2, :] + _dot(
                            jnp.where(hot, 1.0, 0.0).astype(BF16), h_c[u]).astype(acc_ref.dtype)

                        @pl.when(dq == 0)
                        def _():
                            gacc_ref[e, win2, :] = gacc_ref[e, win2, :] + jnp.sum(
                                jnp.where(hot, gt_ref[e, pl.ds(c, 1), :], 0.0), axis=1, keepdims=True)
                        return carry2
                    lax.fori_loop(1, (his[u][e] - firsts[u][e] + SLOT_WIN - 1) // SLOT_WIN, extra, 0)
        return carry
    lax.fori_loop(0, S // (LANES * GS_CHUNKS), chunks, 0)


def _gather(offs_flat, hn3, rank4, gate4, cap, dcols=512):
    B, S, _ = hn3.shape
    n_chunks = S // LANES
    rows = pl.BlockSpec((None, N_EXPERTS, n_chunks, LANES), lambda b, q, offs: (b, 0, 0, 0))
    return pl.pallas_call(
        functools.partial(_gather_body, cap=cap),
        grid_spec=pltpu.PrefetchScalarGridSpec(
            num_scalar_prefetch=1,
            grid=(B, D_MODEL // dcols),
            in_specs=[
                pl.BlockSpec((None, S, dcols), lambda b, q, offs: (b, 0, q)),
                rows, rows,
            ],
            out_specs=[
                pl.BlockSpec((None, N_EXPERTS, cap, dcols), lambda b, q, offs: (b, 0, 0, q)),
                pl.BlockSpec((None, N_EXPERTS, cap, LANES), lambda b, q, offs: (b, 0, 0, 0)),
            ],
        ),
        out_shape=[
            jax.ShapeDtypeStruct((B, N_EXPERTS, cap, D_MODEL), BF16),
            jax.ShapeDtypeStruct((B, N_EXPERTS, cap, LANES), F32),
        ],
        compiler_params=_params("arbitrary", "arbitrary"),
        name="gather",
    )(offs_flat, hn3, rank4, gate4)


def _ffn_body(xs_ref, wg_ref, wu_ref, wd_ref, gs_ref, y_ref, acc_ref):
    f = pl.program_id(1)
    n_f = pl.num_programs(1)
    wg = wg_ref[...].astype(BF16)
    wu = wu_ref[...].astype(BF16)
    wd = wd_ref[...].astype(BF16)
    n_b, cap, _ = xs_ref.shape
    blocks = [(b, slice(r, r + FFN_ROWS)) for b in range(n_b) for r in range(0, cap, FFN_ROWS)]
    xs = [xs_ref[b, rows, :] for b, rows in blocks]
    hg = [_dot(x, wg) for x in xs]
    hu = [_dot(x, wu) for x in xs]
    hid = [(_silu(g) * u).astype(BF16) for g, u in zip(hg, hu)]
    for (b, rows), h in zip(blocks, hid):
        acc_ref[b, rows, :] = jnp.where(f > 0, acc_ref[b, rows, :], 0.0) + _dot(h, wd)

    @pl.when(f == n_f - 1)
    def _():
        for b, rows in blocks:
            y_ref[b, rows, :] = (acc_ref[b, rows, :] * gs_ref[b, rows, :][:, 0:1]).astype(y_ref.dtype)


def _ffn(xs, w_gate, w_up, w_down, gslot, layer, tf=512):
    B, E, cap, _ = xs.shape
    return pl.pallas_call(
        _ffn_body,
        grid=(E, EXPERT_FF // tf),
        in_specs=[
            pl.BlockSpec((B, None, cap, D_MODEL), lambda e, f: (0, e, 0, 0)),
            pl.BlockSpec((None, None, D_MODEL, tf), lambda e, f: (layer, e, 0, f)),
            pl.BlockSpec((None, None, D_MODEL, tf), lambda e, f: (layer, e, 0, f)),
            pl.BlockSpec((None, None, tf, D_MODEL), lambda e, f: (layer, e, f, 0)),
            pl.BlockSpec((B, None, cap, LANES), lambda e, f: (0, e, 0, 0)),
        ],
        out_specs=pl.BlockSpec((B, None, cap, D_MODEL), lambda e, f: (0, e, 0, 0)),
        out_shape=jax.ShapeDtypeStruct((B, E, cap, D_MODEL), BF16),
        scratch_shapes=[pltpu.VMEM((B, cap, D_MODEL), F32)],
        compiler_params=_params("arbitrary", "arbitrary"),
        name="ffn",
    )(xs, w_gate, w_up, w_down, gslot)


def _scatter_body(offs_ref, h_ref, y_ref, rk_ref, o_ref, ycat_ref, *, cap):
    tp = h_ref.shape[0]
    b = pl.program_id(0)
    t = pl.program_id(2)
    row_f = lax.broadcasted_iota(jnp.int32, (SLOT_WIN, LANES), 0).astype(F32)

    experts = range(N_EXPERTS)
    n_cc = tp // LANES
    cs = [t * n_cc + cc for cc in range(n_cc)]
    rows = [slice(cc * LANES, (cc + 1) * LANES) for cc in range(n_cc)]
    los = [[offs_ref[(b * N_EXPERTS + e) * LANES + c] for e in experts] for c in cs]
    his = [[offs_ref[(b * N_EXPERTS + e) * LANES + c + 1] for e in experts] for c in cs]
    firsts = [[(lo // 16) * 16 for lo in row] for row in los]
    starts = [[_window_start(lo, 0, cap) for lo in row] for row in los]
    hots = [[_slot_onehot(row_f, starts[cc][e], starts[cc][e], rk_ref[e, pl.ds(c, 1), :]) for e in experts]
            for cc, c in enumerate(cs)]
    for cc in range(n_cc):
        for e in experts:
            ycat_ref[cc, e * SLOT_WIN:(e + 1) * SLOT_WIN, :] = y_ref[e, pl.ds(starts[cc][e], SLOT_WIN), :]
    stacked = [jnp.concatenate([jnp.where(h, 1.0, 0.0).astype(BF16) for h in row], axis=0) for row in hots]
    for cc in range(n_cc):
        o_ref[rows[cc], :] = h_ref[rows[cc], :] + _tn(stacked[cc], ycat_ref[cc])

    overflow = functools.reduce(
        lambda x, y: x | y, [his[cc][e] - firsts[cc][e] > SLOT_WIN for cc in range(n_cc) for e in experts])

    @pl.when(overflow)
    def _():
        for cc, c in enumerate(cs):
            for e in experts:
                def extra(w, carry, cc=cc, c=c, e=e):
                    start = _window_start(firsts[cc][e], w, cap)
                    hot = _slot_onehot(row_f, start, firsts[cc][e] + w * SLOT_WIN, rk_ref[e, pl.ds(c, 1), :])
                    o_ref[rows[cc], :] = o_ref[rows[cc], :] + _tn(
                        jnp.where(hot, 1.0, 0.0).astype(BF16), y_ref[e, pl.ds(start, SLOT_WIN), :])
                    return carry
                lax.fori_loop(1, (his[cc][e] - firsts[cc][e] + SLOT_WIN - 1) // SLOT_WIN, extra, 0)


def _scatter(offs_flat, h3, y, rank4, cap, tp=1024, dcols=512):
    B, S, _ = h3.shape
    n_chunks = S // LANES
    return pl.pallas_call(
        functools.partial(_scatter_body, cap=cap),
        grid_spec=pltpu.PrefetchScalarGridSpec(
            num_scalar_prefetch=1,
            grid=(B, D_MODEL // dcols, S // tp),
            in_specs=[
                pl.BlockSpec((None, tp, dcols), lambda b, q, t, offs: (b, t, q)),
                pl.BlockSpec((None, N_EXPERTS, cap, dcols), lambda b, q, t, offs: (b, 0, 0, q)),
                pl.BlockSpec((None, N_EXPERTS, n_chunks, LANES), lambda b, q, t, offs: (b, 0, 0, 0)),
            ],
            out_specs=pl.BlockSpec((None, tp, dcols), lambda b, q, t, offs: (b, t, q)),
            scratch_shapes=[pltpu.VMEM((tp // LANES, N_EXPERTS * SLOT_WIN, dcols), BF16)],
        ),
        out_shape=jax.ShapeDtypeStruct((B, S, D_MODEL), F32),
        compiler_params=_params("arbitrary", "arbitrary", "arbitrary"),
        name="scatter",
    )(offs_flat, h3, y, rank4)


def _ple_body(hrow_ref, g_ref, wg_ref, p_ref, wp_ref, hblk_ref, o_ref, hn_ref):
    tm = hrow_ref.shape[0]

    @pl.when(pl.program_id(1) == 0)
    def _():
        def norm_rows(rows):
            hn_ref[rows, :] = _rms(hrow_ref[rows, :], g_ref[...]).astype(BF16)
        _row_chunks(tm, 128, norm_rows)

    gate = _sigmoid(_dot(hn_ref[...], wg_ref[...].astype(BF16)))
    emb = _dot(p_ref[...].astype(BF16), wp_ref[...].astype(BF16))
    o_ref[...] = hblk_ref[...] + gate * emb


def _ple(h2d, gain, w_gate, p2d, w_ple, layer, tm=2048, tn=256):
    m = h2d.shape[0]
    return pl.pallas_call(
        _ple_body,
        grid=(m // tm, D_MODEL // tn),
        in_specs=[
            pl.BlockSpec((tm, D_MODEL), lambda i, j: (i, 0), **ROW_RESIDENT),
            pl.BlockSpec((1, D_MODEL), lambda i, j: (0, 0)),
            pl.BlockSpec((None, D_MODEL, tn), lambda i, j: (layer, 0, j)),
            pl.BlockSpec((None, tm, PLE_DIM), lambda i, j: (layer, i, 0), **ROW_RESIDENT),
            pl.BlockSpec((None, PLE_DIM, tn), lambda i, j: (layer, 0, j)),
            pl.BlockSpec((tm, tn), lambda i, j: (i, j)),
        ],
        out_specs=pl.BlockSpec((tm, tn), lambda i, j: (i, j)),
        out_shape=jax.ShapeDtypeStruct((m, D_MODEL), F32),
        scratch_shapes=[pltpu.VMEM((tm, D_MODEL), BF16)],
        compiler_params=_params("arbitrary", "arbitrary"),
        name="ple",
    )(h2d, gain, w_gate, p2d, w_ple, h2d)


def _final_body(h_ref, g_ref, o_ref):
    o_ref[...] = _rms(h_ref[...], g_ref[...])


def _final_norm(h2d, gain, tm=1024):
    m = h2d.shape[0]
    return pl.pallas_call(
        _final_body,
        grid=(m // tm,),
        in_specs=[pl.BlockSpec((tm, D_MODEL), lambda i: (i, 0)), pl.BlockSpec((1, D_MODEL), lambda i: (0, 0))],
        out_specs=pl.BlockSpec((tm, D_MODEL), lambda i: (i, 0)),
        out_shape=jax.ShapeDtypeStruct((m, D_MODEL), F32),
        compiler_params=_params("arbitrary"),
        name="final_norm",
    )(h2d, gain)


def _rope_cs(pos, dim, theta):
    inv = theta ** (-jnp.arange(0, dim, 2, dtype=F32) / dim)
    ang = pos.astype(F32)[:, None] * inv[None, :]
    return jnp.cos(ang), jnp.sin(ang)


def _tables(S):
    cos, sin = _rope_cs(jnp.arange(S), ROPE_DIM, ROPE_THETA)
    z = jnp.zeros_like(sin)
    rest = HEAD_DIM - ROPE_DIM
    a_c = jnp.concatenate([cos, cos, jnp.ones((S, rest), F32)], axis=1)
    a_sp = jnp.concatenate([z, sin, jnp.zeros((S, rest), F32)], axis=1)
    a_sm = jnp.concatenate([-sin, z, jnp.zeros((S, rest), F32)], axis=1)
    rows = S // GRID_W
    rc, rs = _rope_cs(jnp.repeat(jnp.arange(rows), GRID_W), HEAD_DIM // 2, AXIAL_THETA)
    cc, cs = _rope_cs(jnp.tile(jnp.arange(GRID_W), rows), HEAD_DIM // 2, AXIAL_THETA)
    zz = jnp.zeros_like(rs)
    b_c = jnp.concatenate([rc, rc, cc, cc], axis=1)
    b_sp = jnp.concatenate([zz, rs, zz, cs], axis=1)
    b_sm = jnp.concatenate([-rs, zz, -cs, zz], axis=1)
    return (a_c, a_sp, a_sm), (b_c, b_sp, b_sm)


def kernel(x, p, norm_mix, w_in, conv_w, q_norm, k_norm, out_norm_a, out_norm_b, gdn_a_log, gdn_dt_bias,
           gdn_norm, w_out, norm_moe, w_router, w_gate, w_up, w_down, norm_ple, w_ple, w_ple_gate, norm_final):
    B, S, D = x.shape
    depth = w_in.shape[0]
    assert D == D_MODEL and S % 1024 == 0
    cap = EC_CAPACITY * S // N_EXPERTS
    width = GROUP * DELTA_CHUNK
    tabs_a, tabs_b = _tables(S)
    p2d = p.reshape(depth, B * S, PLE_DIM)
    row = lambda v: v.reshape(1, -1)

    w_in_t = jnp.swapaxes(w_in, 1, 2)
    h = x.reshape(B * S, D)
    for i in range(depth):
        proj, small = _proj(h, row(norm_mix[i]), w_in_t, i)
        proj3 = proj.reshape(B, S, D_MAIN)

        y_a = _mix_a(proj3, tabs_a)
        y_b = _mix_b(proj3, tabs_b, row(q_norm[i]), row(k_norm[i]))

        lg = small[:, :D_SMALL].reshape(B, S // width, width, 4, C_HEADS).transpose(0, 3, 4, 1, 2)
        par = jnp.concatenate([gdn_a_log[i], gdn_dt_bias[i]], axis=0)
        par = jnp.broadcast_to(par.T[:, :, None], (C_HEADS, 4, width))
        y_c = _mix_c(proj3, conv_w, lg, par, row(gdn_norm[i]), i)

        h = _out_proj(y_a.reshape(B * S, A_W), y_b.reshape(B * S, B_QW), y_c.reshape(B * S, C_W),
                      row(out_norm_a[i]), row(out_norm_b[i]), w_out, h, i)

        hn, aff_t = _moe_pre(h.reshape(B, S, D), row(norm_moe[i]), w_router[i].T)
        rank, gates, offs = _topk(aff_t, cap)
        offs_flat = offs.reshape(-1)
        rank4 = rank.reshape(B, N_EXPERTS, S // LANES, LANES)
        gate4 = gates.reshape(B, N_EXPERTS, S // LANES, LANES)
        xs, gslot = _gather(offs_flat, hn, rank4, gate4, cap)
        y = _ffn(xs, w_gate, w_up, w_down, gslot, i)
        h = _scatter(offs_flat, h.reshape(B, S, D), y, rank4, cap).reshape(B * S, D)

        h = _ple(h, row(norm_ple[i]), w_ple_gate, p2d, w_ple, i)
    return _final_norm(h, row(norm_final)).reshape(B, S, D)
```

```python
import functools

import jax
import jax.numpy as jnp
from jax import lax
from jax.experimental import pallas as pl
from jax.experimental.pallas import tpu as pltpu

F32 = jnp.float32
BF16 = jnp.bfloat16

D_MODEL = 2048
HEAD_DIM = 128
A_HEADS = 4
B_HEADS = 8
B_KV_HEADS = 2
B_GROUP = B_HEADS // B_KV_HEADS
C_HEADS = 4
A_W = A_HEADS * HEAD_DIM
B_QW = B_HEADS * HEAD_DIM
B_KVW = B_KV_HEADS * HEAD_DIM
C_W = C_HEADS * HEAD_DIM
DILATED_PATTERNS = ((128, 1), (512, 4), (2048, 16))
ROPE_THETA = 500000.0
ROPE_DIM = HEAD_DIM // 4
AXIAL_THETA = 10000.0
GRID_W = 64
CONV_K = 5
DELTA_CHUNK = 64
N_EXPERTS = 16
EC_CAPACITY = 2
EXPERT_FF = D_MODEL // 2
PLE_DIM = 256
NORM_EPS = 1e-6
D_MAIN = 3 * A_W + B_QW + 2 * B_KVW + 4 * C_W
D_SMALL = 4 * C_HEADS

COL_AQ, COL_AK, COL_AV = 0, 4, 8
COL_BQ, COL_BK, COL_BV = 12, 20, 22
COL_CQ, COL_CK, COL_CV, COL_CZ = 24, 28, 32, 36

LANES = 128
VMEM_LIMIT = 56 * 1024 * 1024
NEG_BIG = -1e30
ROW_RESIDENT = dict(pipeline_mode=pl.Buffered(1))
GROUP = 4
PRE_GROUPS = 4
A_BLOCKS = 8
SLOT_WIN = 48
FFN_ROWS = 512
GS_CHUNKS = 8


def _params(*sem):
    return pltpu.CompilerParams(dimension_semantics=sem, vmem_limit_bytes=VMEM_LIMIT)


def _nt(a, b):
    return lax.dot_general(a, b, (((1,), (1,)), ((), ())), preferred_element_type=F32)


def _tn(a, b):
    return lax.dot_general(a, b, (((0,), (0,)), ((), ())), preferred_element_type=F32)


def _dot(a, b):
    return jnp.dot(a, b, preferred_element_type=F32)


def _split(x):
    hi = x.astype(BF16)
    lo = (x - hi.astype(F32)).astype(BF16)
    return hi, lo


def _dotb(a, b):
    return _dot(a.astype(BF16), b.astype(BF16))


def _dot3(a, b):
    ah, al = _split(a)
    bh, bl = _split(b)
    return _dot(ah, bh) + (_dot(ah, bl) + _dot(al, bh))


def _rms(x, gain):
    return x * lax.rsqrt(jnp.mean(x * x, axis=-1, keepdims=True) + NORM_EPS) * gain


def _sigmoid(x):
    return 1.0 / (1.0 + jnp.exp(-x))


def _silu(x):
    return x * _sigmoid(x)


def _row_chunks(n_rows, chunk, fn):
    def body(c, carry):
        fn(pl.ds(pl.multiple_of(c * chunk, chunk), chunk))
        return carry
    lax.fori_loop(0, n_rows // chunk, body, 0)


def _proj_body(x_ref, g_ref, w_ref, ws_ref, o_ref, os_ref, xn_ref):
    tm = x_ref.shape[0]

    @pl.when(pl.program_id(1) == 0)
    def _():
        def norm_rows(rows):
            xn_ref[rows, :] = _rms(x_ref[rows, :], g_ref[...]).astype(BF16)
        _row_chunks(tm, 128, norm_rows)
        sub = lax.broadcasted_iota(jnp.int32, ws_ref.shape, 0)
        os_ref[...] = _nt(xn_ref[...], jnp.where(sub < D_SMALL, ws_ref[...], 0.0).astype(BF16))

    o_ref[...] = _nt(xn_ref[...], w_ref[...].astype(BF16)).astype(o_ref.dtype)


def _proj(h2d, gain, w_in_t, layer, tm=2048, tn=512):
    m = h2d.shape[0]
    return pl.pallas_call(
        _proj_body,
        grid=(m // tm, D_MAIN // tn),
        in_specs=[
            pl.BlockSpec((tm, D_MODEL), lambda i, j: (i, 0), **ROW_RESIDENT),
            pl.BlockSpec((1, D_MODEL), lambda i, j: (0, 0)),
            pl.BlockSpec((None, tn, D_MODEL), lambda i, j: (layer, j, 0)),
            pl.BlockSpec((None, LANES, D_MODEL), lambda i, j: (layer, D_MAIN // LANES, 0)),
        ],
        out_specs=[
            pl.BlockSpec((tm, tn), lambda i, j: (i, j)),
            pl.BlockSpec((tm, LANES), lambda i, j: (i, 0)),
        ],
        out_shape=[
            jax.ShapeDtypeStruct((m, D_MAIN), BF16),
            jax.ShapeDtypeStruct((m, LANES), F32),
        ],
        scratch_shapes=[pltpu.VMEM((tm, D_MODEL), BF16)],
        compiler_params=_params("arbitrary", "arbitrary"),
        name="proj",
    )(h2d, gain, w_in_t, w_in_t)


def _mixa_body(q_ref, k_ref, v_ref, c_ref, sp_ref, sm_ref, o_ref,
               qf, kf, vf, qd, kp, vp, acc, den, mrun):
    S = q_ref.shape[0]
    half = 64
    blk = 128
    scale = HEAD_DIM ** -0.5

    def prep(rows):
        def rope(x):
            return (x * c_ref[rows, :] + pltpu.roll(x, ROPE_DIM // 2, 1) * sp_ref[rows, :]
                    + pltpu.roll(x, LANES - ROPE_DIM // 2, 1) * sm_ref[rows, :])
        qf[rows, :] = rope(q_ref[rows, :].astype(F32)) * scale
        kf[rows, :] = rope(k_ref[rows, :].astype(F32))
        vf[rows, :] = v_ref[rows, :].astype(F32)
        acc[rows, :] = jnp.zeros((512, LANES), F32)
        den[rows, :] = jnp.zeros((512, LANES), F32)
        mrun[rows, :] = jnp.full((512, LANES), NEG_BIG, F32)
    _row_chunks(S, 512, prep)

    row = lax.broadcasted_iota(jnp.int32, (blk, 2 * blk), 0)
    col = lax.broadcasted_iota(jnp.int32, (blk, 2 * blk), 1)
    in_band = jnp.abs(col - row - half) <= half

    for window, dil in DILATED_PATTERNS:
        assert window // (2 * dil) == half
        L = S // dil
        nblk = L // blk
        stride_k = L + 2 * half
        for j in range(dil):
            for base in (j * stride_k, j * stride_k + half + L):
                kp[base:base + half, :] = jnp.zeros((half, LANES), BF16)
                vp[base:base + half, :] = jnp.zeros((half, 2 * LANES), BF16)

        def fold(r, carry, dil=dil, L=L, stride_k=stride_k):
            def piece(c, carry2):
                src = pl.ds(r + c * (256 * dil), 256, stride=dil) if dil > 1 else pl.ds(
                    pl.multiple_of(c * 256, 256), 256)
                dst = pl.ds(pl.multiple_of(r * stride_k + half + c * 256, half), 256)
                qd[pl.ds(pl.multiple_of(r * L + c * 256, 256), 256), :] = qf[src, :].astype(BF16)
                kp[dst, :] = kf[src, :].astype(BF16)
                vp[dst, 0:LANES] = vf[src, :].astype(BF16)
                vp[dst, LANES:2 * LANES] = jnp.ones((256, LANES), BF16)
                return carry2
            lax.fori_loop(0, L // 256, piece, 0)
            return carry
        lax.fori_loop(0, dil, fold, 0)

        def blocks(it, carry, dil=dil, L=L, nblk=nblk, stride_k=stride_k):
            items = [it * A_BLOCKS + u for u in range(A_BLOCKS)]
            rs = [w // nblk for w in items]
            nbs = [w % nblk for w in items]
            q0 = [pl.multiple_of(r * L + nb * blk, blk) for r, nb in zip(rs, nbs)]
            k0 = [pl.multiple_of(r * stride_k + nb * blk, blk) for r, nb in zip(rs, nbs)]
            s = [_nt(qd[pl.ds(a, blk), :], kp[pl.ds(b, 2 * blk), :]) for a, b in zip(q0, k0)]
            kpos = [nb * blk - half + col for nb in nbs]
            s = [jnp.where(in_band & (kp_ >= 0) & (kp_ < L), s_, NEG_BIG) for s_, kp_ in zip(s, kpos)]
            m_b = [jnp.max(s_, axis=1, keepdims=True) for s_ in s]
            e = [jnp.exp(s_ - m_).astype(BF16) for s_, m_ in zip(s, m_b)]
            od = [_dot(e_, vp[pl.ds(b, 2 * blk), :]) for e_, b in zip(e, k0)]
            for r, nb, m_, od_ in zip(rs, nbs, m_b, od):
                rows = (pl.ds(r + nb * (blk * dil), blk, stride=dil) if dil > 1
                        else pl.ds(pl.multiple_of(nb * blk, blk), blk))
                m_old = mrun[rows, :]
                m_new = jnp.maximum(m_old, m_)
                a_old = jnp.exp(m_old - m_new)
                a_new = jnp.exp(m_ - m_new)
                acc[rows, :] = acc[rows, :] * a_old + od_[:, 0:LANES] * a_new
                den[rows, :] = den[rows, :] * a_old + od_[:, LANES:2 * LANES] * a_new
                mrun[rows, :] = m_new
            return carry
        lax.fori_loop(0, (dil * nblk) // A_BLOCKS, blocks, 0)

    def finish(rows):
        o_ref[rows, :] = (acc[rows, :] / den[rows, :]).astype(o_ref.dtype)
    _row_chunks(S, 512, finish)


def _mix_a(proj3, tabs):
    B, S, _ = proj3.shape
    max_dil = max(d for _, d in DILATED_PATTERNS)
    head = lambda base: pl.BlockSpec((None, S, LANES), lambda b, h: (b, 0, base + h))
    tab = pl.BlockSpec((S, LANES), lambda b, h: (0, 0))
    return pl.pallas_call(
        _mixa_body,
        grid=(B, A_HEADS),
        in_specs=[head(COL_AQ), head(COL_AK), head(COL_AV), tab, tab, tab],
        out_specs=pl.BlockSpec((None, S, LANES), lambda b, h: (b, 0, h)),
        out_shape=jax.ShapeDtypeStruct((B, S, A_W), BF16),
        scratch_shapes=[
            pltpu.VMEM((S, LANES), F32), pltpu.VMEM((S, LANES), F32), pltpu.VMEM((S, LANES), F32),
            pltpu.VMEM((S, LANES), BF16),
            pltpu.VMEM((S + max_dil * LANES, LANES), BF16),
            pltpu.VMEM((S + max_dil * LANES, 2 * LANES), BF16),
            pltpu.VMEM((S, LANES), F32), pltpu.VMEM((S, LANES), F32), pltpu.VMEM((S, LANES), F32),
        ],
        compiler_params=_params("arbitrary", "arbitrary"),
        name="mix_a",
    )(proj3, proj3, proj3, *tabs)


def _mixb_body(q_ref, k_ref, v_ref, c_ref, sp_ref, sm_ref, qg_ref, kg_ref, o_ref,
               kt_ref, va_ref, q4_ref, m_ref, acc_ref, *, tk):
    S = k_ref.shape[0]
    tq = q_ref.shape[0]
    qi = pl.program_id(2)
    quarter = HEAD_DIM // 4

    def rope(x, rows):
        return (x * c_ref[rows, :] + pltpu.roll(x, quarter, 1) * sp_ref[rows, :]
                + pltpu.roll(x, LANES - quarter, 1) * sm_ref[rows, :])

    @pl.when(qi == 0)
    def _():
        def kv_chunk(c, carry):
            rows = pl.ds(pl.multiple_of(c * tk, tk), tk)
            k = rope(_rms(k_ref[rows, :].astype(F32), kg_ref[...]), rows)
            kt_ref[c] = k.T.astype(BF16)
            va_ref[rows, 0:LANES] = v_ref[rows, :]
            va_ref[rows, LANES:2 * LANES] = jnp.ones((tk, LANES), BF16)
            return carry
        lax.fori_loop(0, S // tk, kv_chunk, 0)

    rows_q = pl.ds(pl.multiple_of(qi * tq, tq), tq)
    for g in range(B_GROUP):
        q = _rms(q_ref[:, g * LANES:(g + 1) * LANES].astype(F32), qg_ref[...])
        q = rope(q, rows_q) * (HEAD_DIM ** -0.5)
        q4_ref[g * tq:(g + 1) * tq, :] = q.astype(BF16)
    m_ref[...] = jnp.full(m_ref.shape, NEG_BIG, F32)
    acc_ref[...] = jnp.zeros(acc_ref.shape, F32)

    def kv_step(c, carry):
        rows = pl.ds(pl.multiple_of(c * tk, tk), tk)
        s = _dot(q4_ref[...], kt_ref[c])
        m_old = m_ref[...]
        m_new = jnp.maximum(m_old, jnp.max(s, axis=1, keepdims=True))
        alpha = jnp.exp(m_old - m_new)
        p = jnp.exp(s - jnp.concatenate([m_new] * (tk // LANES), axis=1))
        acc_ref[...] = acc_ref[...] * jnp.concatenate([alpha, alpha], axis=1) + _dot(p.astype(BF16), va_ref[rows, :])
        m_ref[...] = m_new
        return carry
    lax.fori_loop(0, S // tk, kv_step, 0)

    o = acc_ref[:, 0:LANES] / acc_ref[:, LANES:2 * LANES]
    for g in range(B_GROUP):
        o_ref[:, g * LANES:(g + 1) * LANES] = o[g * tq:(g + 1) * tq, :].astype(o_ref.dtype)


def _mix_b(proj3, tabs, q_gain, k_gain, tq=1024, tk=512):
    B, S, _ = proj3.shape
    qw = B_GROUP * LANES
    tab = pl.BlockSpec((S, LANES), lambda b, h, i: (0, 0))
    gain = pl.BlockSpec((1, LANES), lambda b, h, i: (0, 0))
    return pl.pallas_call(
        functools.partial(_mixb_body, tk=tk),
        grid=(B, B_KV_HEADS, S // tq),
        in_specs=[
            pl.BlockSpec((None, tq, qw), lambda b, h, i: (b, i, COL_BQ // B_GROUP + h)),
            pl.BlockSpec((None, S, LANES), lambda b, h, i: (b, 0, COL_BK + h)),
            pl.BlockSpec((None, S, LANES), lambda b, h, i: (b, 0, COL_BV + h)),
            tab, tab, tab, gain, gain,
        ],
        out_specs=pl.BlockSpec((None, tq, qw), lambda b, h, i: (b, i, h)),
        out_shape=jax.ShapeDtypeStruct((B, S, B_QW), BF16),
        scratch_shapes=[
            pltpu.VMEM((S // tk, LANES, tk), BF16),
            pltpu.VMEM((S, 2 * LANES), BF16),
            pltpu.VMEM((B_GROUP * tq, LANES), BF16),
            pltpu.VMEM((B_GROUP * tq, LANES), F32),
            pltpu.VMEM((B_GROUP * tq, 2 * LANES), F32),
        ],
        compiler_params=_params("arbitrary", "arbitrary", "arbitrary"),
        name="mix_b",
    )(proj3, proj3, proj3, *tabs, q_gain, k_gain)


def _softplus(x):
    return jnp.maximum(x, 0.0) + jnp.log(1.0 + jnp.exp(-jnp.abs(x)))


def _mixc_body(q_ref, k_ref, v_ref, z_ref, cwq_ref, cwk_ref, cwv_ref, lg_ref, par_ref, gn_ref, o_ref,
               xp, qn, kn, vn, tab, qp_ref, op_ref, p_ref, n_ref):
    S = q_ref.shape[0]
    C = DELTA_CHUNK
    n_chunks = S // C
    pad = 8

    for src, cw, dst, kind in ((q_ref, cwq_ref, qn, "q"), (k_ref, cwk_ref, kn, "k"), (v_ref, cwv_ref, vn, "v")):
        xp[0:pad, :] = jnp.zeros((pad, LANES), F32)
        xp[pad + S:2 * pad + S, :] = jnp.zeros((pad, LANES), F32)

        def load(rows, src=src):
            xp[pl.ds(pl.multiple_of(rows.start + pad, pad), 512), :] = src[rows, :].astype(F32)
        _row_chunks(S, 512, load)
        for c in range(S // 512):
            base = pad - CONV_K // 2 + c * 512
            y = xp[base:base + 512, :] * cw[0:1, :]
            for j in range(1, CONV_K):
                y = y + xp[base + j:base + j + 512, :] * cw[j:j + 1, :]
            y = _silu(y)
            if kind != "v":
                y = y * lax.rsqrt(jnp.sum(y * y, axis=-1, keepdims=True) + NORM_EPS)
            if kind == "q":
                y = y * (HEAD_DIM ** -0.5)
            dst[c * 512:(c + 1) * 512, :] = y

    W = GROUP * C
    n_groups = S // W
    set_chunks = PRE_GROUPS * GROUP
    n_sets = n_chunks // set_chunks
    ri = lax.broadcasted_iota(jnp.int32, (W, W), 0)
    ci = lax.broadcasted_iota(jnp.int32, (W, W), 1)
    eye = ri == ci
    eye_f = jnp.where(eye, 1.0, 0.0)
    same_block = [jnp.right_shift(ri, s) == jnp.right_shift(ci, s) for s in (3, 4, 5, 6)]
    same_chunk = same_block[-1]
    chunk_start = jnp.right_shift(ri, 6) * C
    stack_mask = (jnp.right_shift(lax.broadcasted_iota(jnp.int32, (GROUP * HEAD_DIM, W), 0), 7)
                  == jnp.right_shift(lax.broadcasted_iota(jnp.int32, (GROUP * HEAD_DIM, W), 1), 6))
    lane_w = lax.broadcasted_iota(jnp.int32, (1, W), 1)
    for d in range(2):
        g = -jnp.exp(par_ref[d:d + 1, :]) * _softplus(lg_ref[2 + d] + par_ref[2 + d:3 + d, :])
        cum = same_chunk & ((ri <= ci) if d == 0 else (ri >= ci))
        tab[d] = _dot3(g, jnp.where(cum, 1.0, 0.0))
        tab[2 + d] = _sigmoid(lg_ref[d])

    def groups_pre(chains, ring):
        each = lambda fn, *lists: [fn(*args) for args in zip(*lists)]
        ds = [d for _, d in chains]
        rows = [pl.ds(pl.multiple_of(i * W, W), W) for i, _ in chains]
        q = [qn[r, :] for r in rows]
        k = [kn[r, :] for r in rows]
        v = [vn[r, :] for r in rows]
        gr = [tab[d, pl.ds(i, 1), :] for i, d in chains]
        br = [tab[2 + d, pl.ds(i, 1), :] for i, d in chains]
        to_col = lambda mask, r: jnp.sum(jnp.where(mask, r, 0.0), axis=1, keepdims=True)
        gcol = [to_col(eye, g) for g in gr]
        bcol = [to_col(eye, b) for b in br]
        glast = [to_col(ci == (chunk_start + (C - 1) if d == 0 else chunk_start), g) for d, g in zip(ds, gr)]
        incl = [same_chunk & ((ci <= ri) if d == 0 else (ci >= ri)) for d in ds]
        strict = [same_chunk & ((ci < ri) if d == 0 else (ci > ri)) for d in ds]
        decay = each(lambda m, gc, g: jnp.exp(jnp.where(m, gc - g, NEG_BIG)), incl, gcol, gr)
        kb = each(lambda a, b: a * b, k, bcol)
        k16 = [a.astype(BF16) for a in k]
        kk = each(lambda a, b: _nt(a.astype(BF16), b), kb, k16)
        yield
        a = each(lambda m, p, dc: jnp.where(m, p * dc, 0.0), strict, kk, decay)
        eg = [jnp.exp(g) for g in gcol]
        rhs = each(lambda vv, b, kbb, e: jnp.concatenate([vv * b, kbb * e], axis=1), v, bcol, kb, eg)
        x = [jnp.where(same_block[0], -m, 0.0) for m in a]
        s1 = [eye_f + m for m in x]
        x2 = each(_dotb, x, x)
        yield
        x2s1 = each(_dotb, x2, s1)
        x4 = each(_dotb, x2, x2)
        yield
        s2 = each(lambda p, m: p + m, s1, x2s1)
        t = each(lambda p, m, n: p + _dotb(m, n), s2, x4, s2)
        yield
        for lvl in range(1, len(same_block)):
            off = same_block[lvl] & jnp.logical_not(same_block[lvl - 1])
            y = each(lambda m, tt: _dotb(jnp.where(off, m, 0.0), tt), a, t)
            yield
            t = each(lambda tt, yy: tt - _dotb(tt, yy), t, y)
            yield
        r16 = each(lambda tt, r: _dotb(tt, r).astype(BF16), t, rhs)
        yield
        qk = each(lambda a_, b_, dc: _nt(a_.astype(BF16), b_) * dc, q, k16, decay)
        yield
        stacked = each(lambda kk_, gl, gc: jnp.where(
            stack_mask, jnp.concatenate([(kk_ * jnp.exp(gl - gc)).T] * GROUP, axis=0), 0.0).astype(BF16),
            k, glast, gcol)
        np_ = each(_dot, stacked, r16)
        yield
        qo = each(lambda m, r: _dot(m.astype(BF16), r), qk, r16)
        for j, (i, d) in enumerate(chains):
            qp_ref[d, rows[j], :] = (q[j] * eg[j] - qo[j][:, LANES:2 * LANES]).astype(BF16)
            op_ref[d, rows[j], :] = qo[j][:, 0:LANES]
            chunks = pl.ds(ring + (i * GROUP) % set_chunks, GROUP)
            n_ref[d, chunks] = np_[j][:, 0:LANES].reshape(GROUP, HEAD_DIM, HEAD_DIM)
            p_ref[d, chunks] = np_[j][:, LANES:2 * LANES].astype(BF16).reshape(GROUP, HEAD_DIM, HEAD_DIM)

    def set_chains(j):
        return ([(j * PRE_GROUPS + u, 0) for u in range(PRE_GROUPS)]
                + [(n_groups - 1 - (j * PRE_GROUPS + u), 1) for u in range(PRE_GROUPS)])

    def scan_steps(j, state, out):
        st = list(state)
        for step in range(set_chunks):
            i = j * set_chunks + step
            new = []
            for d in range(2):
                n = i if d == 0 else n_chunks - 1 - i
                rows = pl.ds(pl.multiple_of(n * C, C), C)
                gr = tab[d, pl.ds(n // GROUP, 1), :]
                target = (n % GROUP) * C + (C - 1 if d == 0 else 0)
                g_last = jnp.sum(jnp.where(lane_w == target, gr, 0.0), axis=1, keepdims=True)
                s16 = st[d].astype(BF16)
                slot = (j % 2) * set_chunks + n % set_chunks
                op_ref[d, rows, :] = op_ref[d, rows, :] + _dot(qp_ref[d, rows, :], s16)
                new.append(st[d] * jnp.exp(g_last) - _dot(p_ref[d, slot], s16) + n_ref[d, slot])
            st = new
            yield
        out.extend(st)

    def run_together(*gens):
        live = list(gens)
        while live:
            for g in list(live):
                try:
                    next(g)
                except StopIteration:
                    live.remove(g)

    run_together(groups_pre(set_chains(0), 0))

    def piped(j, state):
        out = []
        run_together(groups_pre(set_chains(j), (j % 2) * set_chunks), scan_steps(j - 1, state, out))
        return tuple(out)
    zero = jnp.zeros((HEAD_DIM, HEAD_DIM), F32)
    state = lax.fori_loop(1, n_sets, piped, (zero, zero))
    run_together(scan_steps(n_sets - 1, state, []))

    def finish(rows):
        o = op_ref[0, rows, :] + op_ref[1, rows, :]
        o_ref[rows, :] = (_rms(o, gn_ref[...]) * _silu(z_ref[rows, :].astype(F32))).astype(o_ref.dtype)
    _row_chunks(S, 512, finish)


def _mix_c(proj3, conv_w, logits_rows, par, gn, layer):
    B, S, _ = proj3.shape
    n_chunks = S // DELTA_CHUNK
    width = GROUP * DELTA_CHUNK
    n_groups = S // width
    head = lambda base: pl.BlockSpec((None, S, LANES), lambda b, h: (b, 0, base + h))
    cw = lambda base: pl.BlockSpec((None, CONV_K, LANES), lambda b, h: (layer, 0, base + h))
    return pl.pallas_call(
        _mixc_body,
        grid=(B, C_HEADS),
        in_specs=[
            head(COL_CQ), head(COL_CK), head(COL_CV), head(COL_CZ),
            cw(0), cw(C_HEADS), cw(2 * C_HEADS),
            pl.BlockSpec((None, 4, None, n_groups, width), lambda b, h: (b, 0, h, 0, 0)),
            pl.BlockSpec((None, 4, width), lambda b, h: (h, 0, 0)),
            pl.BlockSpec((1, LANES), lambda b, h: (0, 0)),
        ],
        out_specs=pl.BlockSpec((None, S, LANES), lambda b, h: (b, 0, h)),
        out_shape=jax.ShapeDtypeStruct((B, S, C_W), BF16),
        scratch_shapes=[
            pltpu.VMEM((S + 16, LANES), F32),
            pltpu.VMEM((S, LANES), F32), pltpu.VMEM((S, LANES), F32), pltpu.VMEM((S, LANES), F32),
            pltpu.VMEM((4, n_groups, width), F32),
            pltpu.VMEM((2, S, LANES), BF16),
            pltpu.VMEM((2, S, LANES), F32),
            pltpu.VMEM((2, 2 * PRE_GROUPS * GROUP, HEAD_DIM, HEAD_DIM), BF16),
            pltpu.VMEM((2, 2 * PRE_GROUPS * GROUP, HEAD_DIM, HEAD_DIM), F32),
        ],
        compiler_params=_params("arbitrary", "arbitrary"),
        name="mix_c",
    )(proj3, proj3, proj3, proj3, conv_w, conv_w, conv_w, logits_rows, par, gn)


def _outproj_body(ya_ref, yb_ref, yc_ref, ga_ref, gb_ref, w_ref, h_ref, o_ref, yn_ref):
    tm = ya_ref.shape[0]

    @pl.when(pl.program_id(1) == 0)
    def _():
        def norm_rows(rows):
            yn_ref[rows, 0:A_W] = _rms(ya_ref[rows, :].astype(F32), ga_ref[...]).astype(BF16)
            yn_ref[rows, A_W:A_W + B_QW] = _rms(yb_ref[rows, :].astype(F32), gb_ref[...]).astype(BF16)
            yn_ref[rows, A_W + B_QW:D_MODEL] = yc_ref[rows, :]
        _row_chunks(tm, 128, norm_rows)

    o_ref[...] = h_ref[...] + _dot(yn_ref[...], w_ref[...].astype(BF16))


def _out_proj(ya, yb, yc, ga, gb, w_out, h2d, layer, tm=2048, tn=512):
    m = h2d.shape[0]
    return pl.pallas_call(
        _outproj_body,
        grid=(m // tm, D_MODEL // tn),
        in_specs=[
            pl.BlockSpec((tm, A_W), lambda i, j: (i, 0), **ROW_RESIDENT),
            pl.BlockSpec((tm, B_QW), lambda i, j: (i, 0), **ROW_RESIDENT),
            pl.BlockSpec((tm, C_W), lambda i, j: (i, 0), **ROW_RESIDENT),
            pl.BlockSpec((1, A_W), lambda i, j: (0, 0)),
            pl.BlockSpec((1, B_QW), lambda i, j: (0, 0)),
            pl.BlockSpec((None, D_MODEL, tn), lambda i, j: (layer, 0, j)),
            pl.BlockSpec((tm, tn), lambda i, j: (i, j)),
        ],
        out_specs=pl.BlockSpec((tm, tn), lambda i, j: (i, j)),
        out_shape=jax.ShapeDtypeStruct((m, D_MODEL), F32),
        scratch_shapes=[pltpu.VMEM((tm, D_MODEL), BF16)],
        compiler_params=_params("arbitrary", "arbitrary"),
        name="out_proj",
    )(ya, yb, yc, ga, gb, w_out, h2d)


def _moepre_body(h_ref, g_ref, wr_ref, hn_ref, aff_ref):
    xn = _rms(h_ref[...], g_ref[...])
    hi, lo = _split(xn)
    hn_ref[...] = hi
    whi, wlo = _split(wr_ref[...])
    logits = _nt(whi, hi) + (_nt(whi, lo) + _nt(wlo, hi))
    e = jnp.exp(logits - jnp.max(logits, axis=0, keepdims=True))
    aff_ref[...] = e / jnp.sum(e, axis=0, keepdims=True)


def _moe_pre(h3, gain, w_router_t, tm=1024):
    B, S, _ = h3.shape
    return pl.pallas_call(
        _moepre_body,
        grid=(B, S // tm),
        in_specs=[
            pl.BlockSpec((None, tm, D_MODEL), lambda b, i: (b, i, 0)),
            pl.BlockSpec((1, D_MODEL), lambda b, i: (0, 0)),
            pl.BlockSpec((N_EXPERTS, D_MODEL), lambda b, i: (0, 0)),
        ],
        out_specs=[
            pl.BlockSpec((None, tm, D_MODEL), lambda b, i: (b, i, 0)),
            pl.BlockSpec((None, N_EXPERTS, tm), lambda b, i: (b, 0, i)),
        ],
        out_shape=[
            jax.ShapeDtypeStruct((B, S, D_MODEL), BF16),
            jax.ShapeDtypeStruct((B, N_EXPERTS, S), F32),
        ],
        compiler_params=_params("arbitrary", "arbitrary"),
        name="moe_pre",
    )(h3, gain, w_router_t)


def _topk_body(aff_ref, rank_ref, gate_ref, offs_ref, *, cap):
    x = aff_ref[...]
    E, S = x.shape
    xb = pltpu.bitcast(x, jnp.int32)
    count = lambda mask: jnp.sum(jnp.where(mask, 1.0, 0.0), axis=1, keepdims=True)

    def value_bit(it, t):
        cand = t | jnp.left_shift(jnp.int32(1), 30 - it)
        return jnp.where(count(xb >= cand) >= cap, cand, t)
    thr = lax.fori_loop(0, 31, value_bit, jnp.zeros((E, 1), jnp.int32))

    above = xb > thr
    tied = xb == thr
    need = cap - count(above)
    idx = lax.broadcasted_iota(jnp.int32, (E, S), 1)

    index_bits = (S - 1).bit_length()

    def index_bit(it, j):
        cand = j | jnp.left_shift(jnp.int32(1), index_bits - 1 - it)
        return jnp.where(count(tied & (idx < cand)) < need, cand, j)
    jmax = lax.fori_loop(0, index_bits, index_bit, jnp.zeros((E, 1), jnp.int32))
    sel = above | (tied & (idx <= jmax) & (need > 0.0))
    gate_ref[...] = jnp.where(sel, x, 0.0)

    ri = lax.broadcasted_iota(jnp.int32, (LANES, LANES), 0)
    ci = lax.broadcasted_iota(jnp.int32, (LANES, LANES), 1)
    before = jnp.where(ri < ci, 1.0, 0.0).astype(BF16)
    lane = lax.broadcasted_iota(jnp.int32, (E, LANES), 1)
    off = jnp.zeros((E, 1), F32)
    offs = jnp.zeros((E, LANES), F32)
    self_f = jnp.where(sel, 1.0, 0.0)
    for c in range(S // LANES):
        m_c = self_f[:, c * LANES:(c + 1) * LANES]
        rank_c = _dot(m_c.astype(BF16), before)
        rank_ref[:, c * LANES:(c + 1) * LANES] = jnp.where(m_c > 0.0, rank_c + off, -1.0)
        offs = jnp.where(lane == c, off, offs)
        off = off + jnp.sum(m_c, axis=1, keepdims=True)
    offs = jnp.where(lane == S // LANES, off, offs)
    offs_ref[...] = offs.astype(jnp.int32)


def _topk(aff_t, cap):
    B, E, S = aff_t.shape
    spec = pl.BlockSpec((None, E, S), lambda b: (b, 0, 0))
    return pl.pallas_call(
        functools.partial(_topk_body, cap=cap),
        grid=(B,),
        in_specs=[spec],
        out_specs=[spec, spec, pl.BlockSpec((None, E, LANES), lambda b: (b, 0, 0))],
        out_shape=[
            jax.ShapeDtypeStruct((B, E, S), F32),
            jax.ShapeDtypeStruct((B, E, S), F32),
            jax.ShapeDtypeStruct((B, E, LANES), jnp.int32),
        ],
        compiler_params=_params("arbitrary"),
        name="topk",
    )(aff_t)


def _window_start(lo, step, cap):
    return pl.multiple_of(jnp.minimum((lo // 16) * 16 + step * SLOT_WIN, cap - SLOT_WIN), 16)


def _slot_onehot(row_f, start, first_valid, rk):
    slot = row_f + start.astype(F32)
    return (slot == rk) & (slot >= first_valid.astype(F32))


def _gather_body(offs_ref, hn_ref, rk_ref, gt_ref, xs_ref, gs_ref, *, cap):
    S = hn_ref.shape[0]
    b = pl.program_id(0)
    dq = pl.program_id(1)
    acc_ref = xs_ref
    gacc_ref = gs_ref

    def clear(e, carry):
        acc_ref[e] = jnp.zeros(acc_ref.shape[1:], acc_ref.dtype)

        @pl.when(dq == 0)
        def _():
            gacc_ref[e] = jnp.zeros(gacc_ref.shape[1:], F32)
        return carry
    lax.fori_loop(0, N_EXPERTS, clear, 0)

    row_f = lax.broadcasted_iota(jnp.int32, (SLOT_WIN, LANES), 0).astype(F32)

    experts = range(N_EXPERTS)

    def chunks(it, carry):
        cs = [it * GS_CHUNKS + u for u in range(GS_CHUNKS)]
        h_c = [hn_ref[pl.ds(pl.multiple_of(c * LANES, LANES), LANES), :] for c in cs]
        los = [[offs_ref[(b * N_EXPERTS + e) * LANES + c] for e in experts] for c in cs]
        his = [[offs_ref[(b * N_EXPERTS + e) * LANES + c + 1] for e in experts] for c in cs]
        firsts = [[(lo // 16) * 16 for lo in row] for row in los]
        starts = [[_window_start(lo, 0, cap) for lo in row] for row in los]
        hots = [[_slot_onehot(row_f, starts[u][e], starts[u][e], rk_ref[e, pl.ds(c, 1), :]) for e in experts]
                for u, c in enumerate(cs)]
        stacked = [jnp.concatenate([jnp.where(h, 1.0, 0.0).astype(BF16) for h in row], axis=0) for row in hots]
        res = [_dot(st, h) for st, h in zip(stacked, h_c)]
        for u in range(GS_CHUNKS):
            for e in experts:
                win = pl.ds(starts[u][e], SLOT_WIN)
                acc_ref[e, win, :] = acc_ref[e, win, :] + res[u][e * SLOT_WIN:(e + 1) * SLOT_WIN, :].astype(
                    acc_ref.dtype)

        @pl.when(dq == 0)
        def _():
            for u, c in enumerate(cs):
                for e in experts:
                    win = pl.ds(starts[u][e], SLOT_WIN)
                    gacc_ref[e, win, :] = gacc_ref[e, win, :] + jnp.sum(
                        jnp.where(hots[u][e], gt_ref[e, pl.ds(c, 1), :], 0.0), axis=1, keepdims=True)

        overflow = functools.reduce(
            lambda x, y: x | y, [his[u][e] - firsts[u][e] > SLOT_WIN for u in range(GS_CHUNKS) for e in experts])

        @pl.when(overflow)
        def _():
            for u, c in enumerate(cs):
                for e in experts:
                    def extra(w, carry2, u=u, c=c, e=e):
                        start = _window_start(firsts[u][e], w, cap)
                        hot = _slot_onehot(row_f, start, firsts[u][e] + w * SLOT_WIN, rk_ref[e, pl.ds(c, 1), :])
                        win2 = pl.ds(start, SLOT_WIN)
                        acc_ref[e, win2, :] = acc_ref[e, win2, :] + _dot(
                            jnp.where(hot, 1.0, 0.0).astype(BF16), h_c[u]).astype(acc_ref.dtype)

                        @pl.when(dq == 0)
                        def _():
                            gacc_ref[e, win2, :] = gacc_ref[e, win2, :] + jnp.sum(
                                jnp.where(hot, gt_ref[e, pl.ds(c, 1), :], 0.0), axis=1, keepdims=True)
                        return carry2
                    lax.fori_loop(1, (his[u][e] - firsts[u][e] + SLOT_WIN - 1) // SLOT_WIN, extra, 0)
        return carry
    lax.fori_loop(0, S // (LANES * GS_CHUNKS), chunks, 0)


def _gather(offs_flat, hn3, rank4, gate4, cap, dcols=512):
    B, S, _ = hn3.shape
    n_chunks = S // LANES
    rows = pl.BlockSpec((None, N_EXPERTS, n_chunks, LANES), lambda b, q, offs: (b, 0, 0, 0))
    return pl.pallas_call(
        functools.partial(_gather_body, cap=cap),
        grid_spec=pltpu.PrefetchScalarGridSpec(
            num_scalar_prefetch=1,
            grid=(B, D_MODEL // dcols),
            in_specs=[
                pl.BlockSpec((None, S, dcols), lambda b, q, offs: (b, 0, q)),
                rows, rows,
            ],
            out_specs=[
                pl.BlockSpec((None, N_EXPERTS, cap, dcols), lambda b, q, offs: (b, 0, 0, q)),
                pl.BlockSpec((None, N_EXPERTS, cap, LANES), lambda b, q, offs: (b, 0, 0, 0)),
            ],
        ),
        out_shape=[
            jax.ShapeDtypeStruct((B, N_EXPERTS, cap, D_MODEL), BF16),
            jax.ShapeDtypeStruct((B, N_EXPERTS, cap, LANES), F32),
        ],
        compiler_params=_params("arbitrary", "arbitrary"),
        name="gather",
    )(offs_flat, hn3, rank4, gate4)


def _ffn_body(xs_ref, wg_ref, wu_ref, wd_ref, gs_ref, y_ref, acc_ref):
    f = pl.program_id(1)
    n_f = pl.num_programs(1)
    wg = wg_ref[...].astype(BF16)
    wu = wu_ref[...].astype(BF16)
    wd = wd_ref[...].astype(BF16)
    n_b, cap, _ = xs_ref.shape
    blocks = [(b, slice(r, r + FFN_ROWS)) for b in range(n_b) for r in range(0, cap, FFN_ROWS)]
    xs = [xs_ref[b, rows, :] for b, rows in blocks]
    hg = [_dot(x, wg) for x in xs]
    hu = [_dot(x, wu) for x in xs]
    hid = [(_silu(g) * u).astype(BF16) for g, u in zip(hg, hu)]
    for (b, rows), h in zip(blocks, hid):
        acc_ref[b, rows, :] = jnp.where(f > 0, acc_ref[b, rows, :], 0.0) + _dot(h, wd)

    @pl.when(f == n_f - 1)
    def _():
        for b, rows in blocks:
            y_ref[b, rows, :] = (acc_ref[b, rows, :] * gs_ref[b, rows, :][:, 0:1]).astype(y_ref.dtype)


def _ffn(xs, w_gate, w_up, w_down, gslot, layer, tf=512):
    B, E, cap, _ = xs.shape
    return pl.pallas_call(
        _ffn_body,
        grid=(E, EXPERT_FF // tf),
        in_specs=[
            pl.BlockSpec((B, None, cap, D_MODEL), lambda e, f: (0, e, 0, 0)),
            pl.BlockSpec((None, None, D_MODEL, tf), lambda e, f: (layer, e, 0, f)),
            pl.BlockSpec((None, None, D_MODEL, tf), lambda e, f: (layer, e, 0, f)),
            pl.BlockSpec((None, None, tf, D_MODEL), lambda e, f: (layer, e, f, 0)),
            pl.BlockSpec((B, None, cap, LANES), lambda e, f: (0, e, 0, 0)),
        ],
        out_specs=pl.BlockSpec((B, None, cap, D_MODEL), lambda e, f: (0, e, 0, 0)),
        out_shape=jax.ShapeDtypeStruct((B, E, cap, D_MODEL), BF16),
        scratch_shapes=[pltpu.VMEM((B, cap, D_MODEL), F32)],
        compiler_params=_params("arbitrary", "arbitrary"),
        name="ffn",
    )(xs, w_gate, w_up, w_down, gslot)


def _scatter_body(offs_ref, h_ref, y_ref, rk_ref, o_ref, ycat_ref, *, cap):
    tp = h_ref.shape[0]
    b = pl.program_id(0)
    t = pl.program_id(2)
    row_f = lax.broadcasted_iota(jnp.int32, (SLOT_WIN, LANES), 0).astype(F32)

    experts = range(N_EXPERTS)
    n_cc = tp // LANES
    cs = [t * n_cc + cc for cc in range(n_cc)]
    rows = [slice(cc * LANES, (cc + 1) * LANES) for cc in range(n_cc)]
    los = [[offs_ref[(b * N_EXPERTS + e) * LANES + c] for e in experts] for c in cs]
    his = [[offs_ref[(b * N_EXPERTS + e) * LANES + c + 1] for e in experts] for c in cs]
    firsts = [[(lo // 16) * 16 for lo in row] for row in los]
    starts = [[_window_start(lo, 0, cap) for lo in row] for row in los]
    hots = [[_slot_onehot(row_f, starts[cc][e], starts[cc][e], rk_ref[e, pl.ds(c, 1), :]) for e in experts]
            for cc, c in enumerate(cs)]
    for cc in range(n_cc):
        for e in experts:
            ycat_ref[cc, e * SLOT_WIN:(e + 1) * SLOT_WIN, :] = y_ref[e, pl.ds(starts[cc][e], SLOT_WIN), :]
    stacked = [jnp.concatenate([jnp.where(h, 1.0, 0.0).astype(BF16) for h in row], axis=0) for row in hots]
    for cc in range(n_cc):
        o_ref[rows[cc], :] = h_ref[rows[cc], :] + _tn(stacked[cc], ycat_ref[cc])

    overflow = functools.reduce(
        lambda x, y: x | y, [his[cc][e] - firsts[cc][e] > SLOT_WIN for cc in range(n_cc) for e in experts])

    @pl.when(overflow)
    def _():
        for cc, c in enumerate(cs):
            for e in experts:
                def extra(w, carry, cc=cc, c=c, e=e):
                    start = _window_start(firsts[cc][e], w, cap)
                    hot = _slot_onehot(row_f, start, firsts[cc][e] + w * SLOT_WIN, rk_ref[e, pl.ds(c, 1), :])
                    o_ref[rows[cc], :] = o_ref[rows[cc], :] + _tn(
                        jnp.where(hot, 1.0, 0.0).astype(BF16), y_ref[e, pl.ds(start, SLOT_WIN), :])
                    return carry
                lax.fori_loop(1, (his[cc][e] - firsts[cc][e] + SLOT_WIN - 1) // SLOT_WIN, extra, 0)


def _scatter(offs_flat, h3, y, rank4, cap, tp=1024, dcols=512):
    B, S, _ = h3.shape
    n_chunks = S // LANES
    return pl.pallas_call(
        functools.partial(_scatter_body, cap=cap),
        grid_spec=pltpu.PrefetchScalarGridSpec(
            num_scalar_prefetch=1,
            grid=(B, D_MODEL // dcols, S // tp),
            in_specs=[
                pl.BlockSpec((None, tp, dcols), lambda b, q, t, offs: (b, t, q)),
                pl.BlockSpec((None, N_EXPERTS, cap, dcols), lambda b, q, t, offs: (b, 0, 0, q)),
                pl.BlockSpec((None, N_EXPERTS, n_chunks, LANES), lambda b, q, t, offs: (b, 0, 0, 0)),
            ],
            out_specs=pl.BlockSpec((None, tp, dcols), lambda b, q, t, offs: (b, t, q)),
            scratch_shapes=[pltpu.VMEM((tp // LANES, N_EXPERTS * SLOT_WIN, dcols), BF16)],
        ),
        out_shape=jax.ShapeDtypeStruct((B, S, D_MODEL), F32),
        compiler_params=_params("arbitrary", "arbitrary", "arbitrary"),
        name="scatter",
    )(offs_flat, h3, y, rank4)


def _ple_body(hrow_ref, g_ref, wg_ref, p_ref, wp_ref, hblk_ref, o_ref, hn_ref):
    tm = hrow_ref.shape[0]

    @pl.when(pl.program_id(1) == 0)
    def _():
        def norm_rows(rows):
            hn_ref[rows, :] = _rms(hrow_ref[rows, :], g_ref[...]).astype(BF16)
        _row_chunks(tm, 128, norm_rows)

    gate = _sigmoid(_dot(hn_ref[...], wg_ref[...].astype(BF16)))
    emb = _dot(p_ref[...].astype(BF16), wp_ref[...].astype(BF16))
    o_ref[...] = hblk_ref[...] + gate * emb


def _ple(h2d, gain, w_gate, p2d, w_ple, layer, tm=2048, tn=256):
    m = h2d.shape[0]
    return pl.pallas_call(
        _ple_body,
        grid=(m // tm, D_MODEL // tn),
        in_specs=[
            pl.BlockSpec((tm, D_MODEL), lambda i, j: (i, 0), **ROW_RESIDENT),
            pl.BlockSpec((1, D_MODEL), lambda i, j: (0, 0)),
            pl.BlockSpec((None, D_MODEL, tn), lambda i, j: (layer, 0, j)),
            pl.BlockSpec((None, tm, PLE_DIM), lambda i, j: (layer, i, 0), **ROW_RESIDENT),
            pl.BlockSpec((None, PLE_DIM, tn), lambda i, j: (layer, 0, j)),
            pl.BlockSpec((tm, tn), lambda i, j: (i, j)),
        ],
        out_specs=pl.BlockSpec((tm, tn), lambda i, j: (i, j)),
        out_shape=jax.ShapeDtypeStruct((m, D_MODEL), F32),
        scratch_shapes=[pltpu.VMEM((tm, D_MODEL), BF16)],
        compiler_params=_params("arbitrary", "arbitrary"),
        name="ple",
    )(h2d, gain, w_gate, p2d, w_ple, h2d)


def _final_body(h_ref, g_ref, o_ref):
    o_ref[...] = _rms(h_ref[...], g_ref[...])


def _final_norm(h2d, gain, tm=1024):
    m = h2d.shape[0]
    return pl.pallas_call(
        _final_body,
        grid=(m // tm,),
        in_specs=[pl.BlockSpec((tm, D_MODEL), lambda i: (i, 0)), pl.BlockSpec((1, D_MODEL), lambda i: (0, 0))],
        out_specs=pl.BlockSpec((tm, D_MODEL), lambda i: (i, 0)),
        out_shape=jax.ShapeDtypeStruct((m, D_MODEL), F32),
        compiler_params=_params("arbitrary"),
        name="final_norm",
    )(h2d, gain)


def _rope_cs(pos, dim, theta):
    inv = theta ** (-jnp.arange(0, dim, 2, dtype=F32) / dim)
    ang = pos.astype(F32)[:, None] * inv[None, :]
    return jnp.cos(ang), jnp.sin(ang)


def _tables(S):
    cos, sin = _rope_cs(jnp.arange(S), ROPE_DIM, ROPE_THETA)
    z = jnp.zeros_like(sin)
    rest = HEAD_DIM - ROPE_DIM
    a_c = jnp.concatenate([cos, cos, jnp.ones((S, rest), F32)], axis=1)
    a_sp = jnp.concatenate([z, sin, jnp.zeros((S, rest), F32)], axis=1)
    a_sm = jnp.concatenate([-sin, z, jnp.zeros((S, rest), F32)], axis=1)
    rows = S // GRID_W
    rc, rs = _rope_cs(jnp.repeat(jnp.arange(rows), GRID_W), HEAD_DIM // 2, AXIAL_THETA)
    cc, cs = _rope_cs(jnp.tile(jnp.arange(GRID_W), rows), HEAD_DIM // 2, AXIAL_THETA)
    zz = jnp.zeros_like(rs)
    b_c = jnp.concatenate([rc, rc, cc, cc], axis=1)
    b_sp = jnp.concatenate([zz, rs, zz, cs], axis=1)
    b_sm = jnp.concatenate([-rs, zz, -cs, zz], axis=1)
    return (a_c, a_sp, a_sm), (b_c, b_sp, b_sm)


def kernel(x, p, norm_mix, w_in, conv_w, q_norm, k_norm, out_norm_a, out_norm_b, gdn_a_log, gdn_dt_bias,
           gdn_norm, w_out, norm_moe, w_router, w_gate, w_up, w_down, norm_ple, w_ple, w_ple_gate, norm_final):
    B, S, D = x.shape
    depth = w_in.shape[0]
    assert D == D_MODEL and S % 1024 == 0
    cap = EC_CAPACITY * S // N_EXPERTS
    width = GROUP * DELTA_CHUNK
    tabs_a, tabs_b = _tables(S)
    p2d = p.reshape(depth, B * S, PLE_DIM)
    row = lambda v: v.reshape(1, -1)

    w_in_t = jnp.swapaxes(w_in, 1, 2)
    h = x.reshape(B * S, D)
    for i in range(depth):
        proj, small = _proj(h, row(norm_mix[i]), w_in_t, i)
        proj3 = proj.reshape(B, S, D_MAIN)

        y_a = _mix_a(proj3, tabs_a)
        y_b = _mix_b(proj3, tabs_b, row(q_norm[i]), row(k_norm[i]))

        lg = small[:, :D_SMALL].reshape(B, S // width, width, 4, C_HEADS).transpose(0, 3, 4, 1, 2)
        par = jnp.concatenate([gdn_a_log[i], gdn_dt_bias[i]], axis=0)
        par = jnp.broadcast_to(par.T[:, :, None], (C_HEADS, 4, width))
        y_c = _mix_c(proj3, conv_w, lg, par, row(gdn_norm[i]), i)

        h = _out_proj(y_a.reshape(B * S, A_W), y_b.reshape(B * S, B_QW), y_c.reshape(B * S, C_W),
                      row(out_norm_a[i]), row(out_norm_b[i]), w_out, h, i)

        hn, aff_t = _moe_pre(h.reshape(B, S, D), row(norm_moe[i]), w_router[i].T)
        rank, gates, offs = _topk(aff_t, cap)
        offs_flat = offs.reshape(-1)
        rank4 = rank.reshape(B, N_EXPERTS, S // LANES, LANES)
        gate4 = gates.reshape(B, N_EXPERTS, S // LANES, LANES)
        xs, gslot = _gather(offs_flat, hn, rank4, gate4, cap)
        y = _ffn(xs, w_gate, w_up, w_down, gslot, i)
        h = _scatter(offs_flat, h.reshape(B, S, D), y, rank4, cap).reshape(B * S, D)

        h = _ple(h, row(norm_ple[i]), w_ple_gate, p2d, w_ple, i)
    return _final_norm(h, row(norm_final)).reshape(B, S, D)
```

```python
import functools

import jax
import jax.numpy as jnp
from jax import lax
from jax.experimental import pallas as pl
from jax.experimental.pallas import tpu as pltpu

F32 = jnp.float32
BF16 = jnp.bfloat16

D_MODEL = 2048
HEAD_DIM = 128
A_HEADS = 4
B_HEADS = 8
B_KV_HEADS = 2
B_GROUP = B_HEADS // B_KV_HEADS
C_HEADS = 4
A_W = A_HEADS * HEAD_DIM
B_QW = B_HEADS * HEAD_DIM
B_KVW = B_KV_HEADS * HEAD_DIM
C_W = C_HEADS * HEAD_DIM
DILATED_PATTERNS = ((128, 1), (512, 4), (2048, 16))
ROPE_THETA = 500000.0
ROPE_DIM = HEAD_DIM // 4
AXIAL_THETA = 10000.0
GRID_W = 64
CONV_K = 5
DELTA_CHUNK = 64
N_EXPERTS = 16
EC_CAPACITY = 2
EXPERT_FF = D_MODEL // 2
PLE_DIM = 256
NORM_EPS = 1e-6
D_MAIN = 3 * A_W + B_QW + 2 * B_KVW + 4 * C_W
D_SMALL = 4 * C_HEADS

COL_AQ, COL_AK, COL_AV = 0, 4, 8
COL_BQ, COL_BK, COL_BV = 12, 20, 22
COL_CQ, COL_CK, COL_CV, COL_CZ = 24, 28, 32, 36

LANES = 128
VMEM_LIMIT = 56 * 1024 * 1024
NEG_BIG = -1e30
ROW_RESIDENT = dict(pipeline_mode=pl.Buffered(1))
GROUP = 4
PRE_GROUPS = 4
A_BLOCKS = 8
SLOT_WIN = 48
FFN_ROWS = 512
GS_CHUNKS = 8


def _params(*sem):
    return pltpu.CompilerParams(dimension_semantics=sem, vmem_limit_bytes=VMEM_LIMIT)


def _nt(a, b):
    return lax.dot_general(a, b, (((1,), (1,)), ((), ())), preferred_element_type=F32)


def _tn(a, b):
    return lax.dot_general(a, b, (((0,), (0,)), ((), ())), preferred_element_type=F32)


def _dot(a, b):
    return jnp.dot(a, b, preferred_element_type=F32)


def _split(x):
    hi = x.astype(BF16)
    lo = (x - hi.astype(F32)).astype(BF16)
    return hi, lo


def _dotb(a, b):
    return _dot(a.astype(BF16), b.astype(BF16))


def _dot3(a, b):
    ah, al = _split(a)
    bh, bl = _split(b)
    return _dot(ah, bh) + (_dot(ah, bl) + _dot(al, bh))


def _rms(x, gain):
    return x * lax.rsqrt(jnp.mean(x * x, axis=-1, keepdims=True) + NORM_EPS) * gain


def _sigmoid(x):
    return 1.0 / (1.0 + jnp.exp(-x))


def _silu(x):
    return x * _sigmoid(x)


def _row_chunks(n_rows, chunk, fn):
    def body(c, carry):
        fn(pl.ds(pl.multiple_of(c * chunk, chunk), chunk))
        return carry
    lax.fori_loop(0, n_rows // chunk, body, 0)


def _norm_row_block(x_hbm, xbuf, sem, g_ref, dst_ref):
    tm = xbuf.shape[0]
    i = pl.program_id(0)

    def copy(blk):
        return pltpu.make_async_copy(x_hbm.at[pl.ds(pl.multiple_of(blk * tm, tm), tm), :], xbuf, sem)

    @pl.when(i == 0)
    def _():
        copy(0).start()
    copy(i).wait()

    def norm_rows(rows):
        dst_ref[rows, :] = _rms(xbuf[rows, :], g_ref[...]).astype(BF16)
    _row_chunks(tm, 128, norm_rows)

    @pl.when(i + 1 < pl.num_programs(0))
    def _():
        copy(i + 1).start()


def _proj_body(x_hbm, g_ref, w_ref, ws_ref, o_ref, os_ref, xn_ref, xbuf, sem):
    @pl.when(pl.program_id(1) == 0)
    def _():
        _norm_row_block(x_hbm, xbuf, sem, g_ref, xn_ref)
        sub = lax.broadcasted_iota(jnp.int32, ws_ref.shape, 0)
        os_ref[...] = _nt(xn_ref[...], jnp.where(sub < D_SMALL, ws_ref[...], 0.0).astype(BF16))

    o_ref[...] = _nt(xn_ref[...], w_ref[...].astype(BF16)).astype(o_ref.dtype)


def _proj(h2d, gain, w_in_t, layer, tm=2048, tn=512):
    m = h2d.shape[0]
    return pl.pallas_call(
        _proj_body,
        grid=(m // tm, D_MAIN // tn),
        in_specs=[
            pl.BlockSpec(memory_space=pl.ANY),
            pl.BlockSpec((1, D_MODEL), lambda i, j: (0, 0)),
            pl.BlockSpec((None, tn, D_MODEL), lambda i, j: (layer, j, 0)),
            pl.BlockSpec((None, LANES, D_MODEL), lambda i, j: (layer, D_MAIN // LANES, 0)),
        ],
        out_specs=[
            pl.BlockSpec((tm, tn), lambda i, j: (i, j)),
            pl.BlockSpec((tm, LANES), lambda i, j: (i, 0)),
        ],
        out_shape=[
            jax.ShapeDtypeStruct((m, D_MAIN), BF16),
            jax.ShapeDtypeStruct((m, LANES), F32),
        ],
        scratch_shapes=[pltpu.VMEM((tm, D_MODEL), BF16), pltpu.VMEM((tm, D_MODEL), F32),
                        pltpu.SemaphoreType.DMA(())],
        compiler_params=_params("arbitrary", "arbitrary"),
        name="proj",
    )(h2d, gain, w_in_t, w_in_t)


def _mixa_body(q_ref, k_ref, v_ref, c_ref, sp_ref, sm_ref, o_ref,
               qf, kf, vf, qd, kp, vp, acc, den, mrun):
    S = q_ref.shape[0]
    half = 64
    blk = 128
    scale = HEAD_DIM ** -0.5

    def prep(rows):
        def rope(x):
            return (x * c_ref[rows, :] + pltpu.roll(x, ROPE_DIM // 2, 1) * sp_ref[rows, :]
                    + pltpu.roll(x, LANES - ROPE_DIM // 2, 1) * sm_ref[rows, :])
        qf[rows, :] = rope(q_ref[rows, :].astype(F32)) * scale
        kf[rows, :] = rope(k_ref[rows, :].astype(F32))
        vf[rows, :] = v_ref[rows, :].astype(F32)
        acc[rows, :] = jnp.zeros((512, LANES), F32)
        den[rows, :] = jnp.zeros((512, LANES), F32)
        mrun[rows, :] = jnp.full((512, LANES), NEG_BIG, F32)
    _row_chunks(S, 512, prep)

    row = lax.broadcasted_iota(jnp.int32, (blk, 2 * blk), 0)
    col = lax.broadcasted_iota(jnp.int32, (blk, 2 * blk), 1)
    in_band = jnp.abs(col - row - half) <= half

    for window, dil in DILATED_PATTERNS:
        assert window // (2 * dil) == half
        L = S // dil
        nblk = L // blk
        stride_k = L + 2 * half
        for j in range(dil):
            for base in (j * stride_k, j * stride_k + half + L):
                kp[base:base + half, :] = jnp.zeros((half, LANES), BF16)
                vp[base:base + half, :] = jnp.zeros((half, 2 * LANES), BF16)

        def fold(r, carry, dil=dil, L=L, stride_k=stride_k):
            def piece(c, carry2):
                src = pl.ds(r + c * (256 * dil), 256, stride=dil) if dil > 1 else pl.ds(
                    pl.multiple_of(c * 256, 256), 256)
                dst = pl.ds(pl.multiple_of(r * stride_k + half + c * 256, half), 256)
                qd[pl.ds(pl.multiple_of(r * L + c * 256, 256), 256), :] = qf[src, :].astype(BF16)
                kp[dst, :] = kf[src, :].astype(BF16)
                vp[dst, 0:LANES] = vf[src, :].astype(BF16)
                vp[dst, LANES:2 * LANES] = jnp.ones((256, LANES), BF16)
                return carry2
            lax.fori_loop(0, L // 256, piece, 0)
            return carry
        lax.fori_loop(0, dil, fold, 0)

        def blocks(it, carry, dil=dil, L=L, nblk=nblk, stride_k=stride_k):
            items = [it * A_BLOCKS + u for u in range(A_BLOCKS)]
            rs = [w // nblk for w in items]
            nbs = [w % nblk for w in items]
            q0 = [pl.multiple_of(r * L + nb * blk, blk) for r, nb in zip(rs, nbs)]
            k0 = [pl.multiple_of(r * stride_k + nb * blk, blk) for r, nb in zip(rs, nbs)]
            s = [_nt(qd[pl.ds(a, blk), :], kp[pl.ds(b, 2 * blk), :]) for a, b in zip(q0, k0)]
            kpos = [nb * blk - half + col for nb in nbs]
            s = [jnp.where(in_band & (kp_ >= 0) & (kp_ < L), s_, NEG_BIG) for s_, kp_ in zip(s, kpos)]
            m_b = [jnp.max(s_, axis=1, keepdims=True) for s_ in s]
            e = [jnp.exp(s_ - m_).astype(BF16) for s_, m_ in zip(s, m_b)]
            od = [_dot(e_, vp[pl.ds(b, 2 * blk), :]) for e_, b in zip(e, k0)]
            for r, nb, m_, od_ in zip(rs, nbs, m_b, od):
                rows = (pl.ds(r + nb * (blk * dil), blk, stride=dil) if dil > 1
                        else pl.ds(pl.multiple_of(nb * blk, blk), blk))
                m_old = mrun[rows, :]
                m_new = jnp.maximum(m_old, m_)
                a_old = jnp.exp(m_old - m_new)
                a_new = jnp.exp(m_ - m_new)
                acc[rows, :] = acc[rows, :] * a_old + od_[:, 0:LANES] * a_new
                den[rows, :] = den[rows, :] * a_old + od_[:, LANES:2 * LANES] * a_new
                mrun[rows, :] = m_new
            return carry
        lax.fori_loop(0, (dil * nblk) // A_BLOCKS, blocks, 0)

    def finish(rows):
        o_ref[rows, :] = (acc[rows, :] / den[rows, :]).astype(o_ref.dtype)
    _row_chunks(S, 512, finish)


def _mix_a(proj3, tabs):
    B, S, _ = proj3.shape
    max_dil = max(d for _, d in DILATED_PATTERNS)
    head = lambda base: pl.BlockSpec((None, S, LANES), lambda b, h: (b, 0, base + h))
    tab = pl.BlockSpec((S, LANES), lambda b, h: (0, 0))
    return pl.pallas_call(
        _mixa_body,
        grid=(B, A_HEADS),
        in_specs=[head(COL_AQ), head(COL_AK), head(COL_AV), tab, tab, tab],
        out_specs=pl.BlockSpec((None, S, LANES), lambda b, h: (b, 0, h)),
        out_shape=jax.ShapeDtypeStruct((B, S, A_W), BF16),
        scratch_shapes=[
            pltpu.VMEM((S, LANES), F32), pltpu.VMEM((S, LANES), F32), pltpu.VMEM((S, LANES), F32),
            pltpu.VMEM((S, LANES), BF16),
            pltpu.VMEM((S + max_dil * LANES, LANES), BF16),
            pltpu.VMEM((S + max_dil * LANES, 2 * LANES), BF16),
            pltpu.VMEM((S, LANES), F32), pltpu.VMEM((S, LANES), F32), pltpu.VMEM((S, LANES), F32),
        ],
        compiler_params=_params("arbitrary", "arbitrary"),
        name="mix_a",
    )(proj3, proj3, proj3, *tabs)


def _mixb_body(q_ref, k_ref, v_ref, c_ref, sp_ref, sm_ref, qg_ref, kg_ref, o_ref,
               kt_ref, va_ref, q4_ref, m_ref, acc_ref, *, tk):
    S = k_ref.shape[0]
    tq = q_ref.shape[0]
    qi = pl.program_id(2)
    quarter = HEAD_DIM // 4

    def rope(x, rows):
        return (x * c_ref[rows, :] + pltpu.roll(x, quarter, 1) * sp_ref[rows, :]
                + pltpu.roll(x, LANES - quarter, 1) * sm_ref[rows, :])

    @pl.when(qi == 0)
    def _():
        def kv_chunk(c, carry):
            rows = pl.ds(pl.multiple_of(c * tk, tk), tk)
            k = rope(_rms(k_ref[rows, :].astype(F32), kg_ref[...]), rows)
            kt_ref[c] = k.T.astype(BF16)
            va_ref[rows, 0:LANES] = v_ref[rows, :]
            va_ref[rows, LANES:2 * LANES] = jnp.ones((tk, LANES), BF16)
            return carry
        lax.fori_loop(0, S // tk, kv_chunk, 0)

    rows_q = pl.ds(pl.multiple_of(qi * tq, tq), tq)
    for g in range(B_GROUP):
        q = _rms(q_ref[:, g * LANES:(g + 1) * LANES].astype(F32), qg_ref[...])
        q = rope(q, rows_q) * (HEAD_DIM ** -0.5)
        q4_ref[g * tq:(g + 1) * tq, :] = q.astype(BF16)
    m_ref[...] = jnp.full(m_ref.shape, NEG_BIG, F32)
    acc_ref[...] = jnp.zeros(acc_ref.shape, F32)

    def kv_step(c, carry):
        rows = pl.ds(pl.multiple_of(c * tk, tk), tk)
        s = _dot(q4_ref[...], kt_ref[c])
        m_old = m_ref[...]
        m_new = jnp.maximum(m_old, jnp.max(s, axis=1, keepdims=True))
        alpha = jnp.exp(m_old - m_new)
        p = jnp.exp(s - jnp.concatenate([m_new] * (tk // LANES), axis=1))
        acc_ref[...] = acc_ref[...] * jnp.concatenate([alpha, alpha], axis=1) + _dot(p.astype(BF16), va_ref[rows, :])
        m_ref[...] = m_new
        return carry
    lax.fori_loop(0, S // tk, kv_step, 0)

    o = acc_ref[:, 0:LANES] / acc_ref[:, LANES:2 * LANES]
    for g in range(B_GROUP):
        o_ref[:, g * LANES:(g + 1) * LANES] = o[g * tq:(g + 1) * tq, :].astype(o_ref.dtype)


def _mix_b(proj3, tabs, q_gain, k_gain, tq=1024, tk=512):
    B, S, _ = proj3.shape
    qw = B_GROUP * LANES
    tab = pl.BlockSpec((S, LANES), lambda b, h, i: (0, 0))
    gain = pl.BlockSpec((1, LANES), lambda b, h, i: (0, 0))
    return pl.pallas_call(
        functools.partial(_mixb_body, tk=tk),
        grid=(B, B_KV_HEADS, S // tq),
        in_specs=[
            pl.BlockSpec((None, tq, qw), lambda b, h, i: (b, i, COL_BQ // B_GROUP + h)),
            pl.BlockSpec((None, S, LANES), lambda b, h, i: (b, 0, COL_BK + h)),
            pl.BlockSpec((None, S, LANES), lambda b, h, i: (b, 0, COL_BV + h)),
            tab, tab, tab, gain, gain,
        ],
        out_specs=pl.BlockSpec((None, tq, qw), lambda b, h, i: (b, i, h)),
        out_shape=jax.ShapeDtypeStruct((B, S, B_QW), BF16),
        scratch_shapes=[
            pltpu.VMEM((S // tk, LANES, tk), BF16),
            pltpu.VMEM((S, 2 * LANES), BF16),
            pltpu.VMEM((B_GROUP * tq, LANES), BF16),
            pltpu.VMEM((B_GROUP * tq, LANES), F32),
            pltpu.VMEM((B_GROUP * tq, 2 * LANES), F32),
        ],
        compiler_params=_params("arbitrary", "arbitrary", "arbitrary"),
        name="mix_b",
    )(proj3, proj3, proj3, *tabs, q_gain, k_gain)


def _softplus(x):
    return jnp.maximum(x, 0.0) + jnp.log(1.0 + jnp.exp(-jnp.abs(x)))


def _mixc_body(q_ref, k_ref, v_ref, z_ref, cwq_ref, cwk_ref, cwv_ref, lg_ref, par_ref, gn_ref, o_ref,
               xp, qn, kn, vn, tab, qp_ref, op_ref, p_ref, n_ref):
    S = q_ref.shape[0]
    C = DELTA_CHUNK
    n_chunks = S // C
    pad = 8

    for src, cw, dst, kind in ((q_ref, cwq_ref, qn, "q"), (k_ref, cwk_ref, kn, "k"), (v_ref, cwv_ref, vn, "v")):
        xp[0:pad, :] = jnp.zeros((pad, LANES), F32)
        xp[pad + S:2 * pad + S, :] = jnp.zeros((pad, LANES), F32)

        def load(rows, src=src):
            xp[pl.ds(pl.multiple_of(rows.start + pad, pad), 512), :] = src[rows, :].astype(F32)
        _row_chunks(S, 512, load)
        for c in range(S // 512):
            base = pad - CONV_K // 2 + c * 512
            y = xp[base:base + 512, :] * cw[0:1, :]
            for j in range(1, CONV_K):
                y = y + xp[base + j:base + j + 512, :] * cw[j:j + 1, :]
            y = _silu(y)
            if kind != "v":
                y = y * lax.rsqrt(jnp.sum(y * y, axis=-1, keepdims=True) + NORM_EPS)
            if kind == "q":
                y = y * (HEAD_DIM ** -0.5)
            dst[c * 512:(c + 1) * 512, :] = y

    W = GROUP * C
    n_groups = S // W
    set_chunks = PRE_GROUPS * GROUP
    n_sets = n_chunks // set_chunks
    ri = lax.broadcasted_iota(jnp.int32, (W, W), 0)
    ci = lax.broadcasted_iota(jnp.int32, (W, W), 1)
    eye = ri == ci
    eye_f = jnp.where(eye, 1.0, 0.0)
    same_block = [jnp.right_shift(ri, s) == jnp.right_shift(ci, s) for s in (3, 4, 5, 6)]
    same_chunk = same_block[-1]
    chunk_start = jnp.right_shift(ri, 6) * C
    stack_mask = (jnp.right_shift(lax.broadcasted_iota(jnp.int32, (GROUP * HEAD_DIM, W), 0), 7)
                  == jnp.right_shift(lax.broadcasted_iota(jnp.int32, (GROUP * HEAD_DIM, W), 1), 6))
    lane_w = lax.broadcasted_iota(jnp.int32, (1, W), 1)
    for d in range(2):
        g = -jnp.exp(par_ref[d:d + 1, :]) * _softplus(lg_ref[2 + d] + par_ref[2 + d:3 + d, :])
        cum = same_chunk & ((ri <= ci) if d == 0 else (ri >= ci))
        tab[d] = _dot3(g, jnp.where(cum, 1.0, 0.0))
        tab[2 + d] = _sigmoid(lg_ref[d])

    def groups_pre(chains, ring):
        each = lambda fn, *lists: [fn(*args) for args in zip(*lists)]
        ds = [d for _, d in chains]
        rows = [pl.ds(pl.multiple_of(i * W, W), W) for i, _ in chains]
        q = [qn[r, :] for r in rows]
        k = [kn[r, :] for r in rows]
        v = [vn[r, :] for r in rows]
        gr = [tab[d, pl.ds(i, 1), :] for i, d in chains]
        br = [tab[2 + d, pl.ds(i, 1), :] for i, d in chains]
        to_col = lambda mask, r: jnp.sum(jnp.where(mask, r, 0.0), axis=1, keepdims=True)
        gcol = [to_col(eye, g) for g in gr]
        bcol = [to_col(eye, b) for b in br]
        glast = [to_col(ci == (chunk_start + (C - 1) if d == 0 else chunk_start), g) for d, g in zip(ds, gr)]
        incl = [same_chunk & ((ci <= ri) if d == 0 else (ci >= ri)) for d in ds]
        strict = [same_chunk & ((ci < ri) if d == 0 else (ci > ri)) for d in ds]
        decay = each(lambda m, gc, g: jnp.exp(jnp.where(m, gc - g, NEG_BIG)), incl, gcol, gr)
        kb = each(lambda a, b: a * b, k, bcol)
        k16 = [a.astype(BF16) for a in k]
        kk = each(lambda a, b: _nt(a.astype(BF16), b), kb, k16)
        yield
        a = each(lambda m, p, dc: jnp.where(m, p * dc, 0.0), strict, kk, decay)
        eg = [jnp.exp(g) for g in gcol]
        rhs = each(lambda vv, b, kbb, e: jnp.concatenate([vv * b, kbb * e], axis=1), v, bcol, kb, eg)
        x = [jnp.where(same_block[0], -m, 0.0) for m in a]
        s1 = [eye_f + m for m in x]
        x2 = each(_dotb, x, x)
        yield
        x2s1 = each(_dotb, x2, s1)
        x4 = each(_dotb, x2, x2)
        yield
        s2 = each(lambda p, m: p + m, s1, x2s1)
        t = each(lambda p, m, n: p + _dotb(m, n), s2, x4, s2)
        yield
        for lvl in range(1, len(same_block)):
            off = same_block[lvl] & jnp.logical_not(same_block[lvl - 1])
            y = each(lambda m, tt: _dotb(jnp.where(off, m, 0.0), tt), a, t)
            yield
            t = each(lambda tt, yy: tt - _dotb(tt, yy), t, y)
            yield
        r16 = each(lambda tt, r: _dotb(tt, r).astype(BF16), t, rhs)
        yield
        qk = each(lambda a_, b_, dc: _nt(a_.astype(BF16), b_) * dc, q, k16, decay)
        yield
        stacked = each(lambda kk_, gl, gc: jnp.where(
            stack_mask, jnp.concatenate([(kk_ * jnp.exp(gl - gc)).T] * GROUP, axis=0), 0.0).astype(BF16),
            k, glast, gcol)
        np_ = each(_dot, stacked, r16)
        yield
        qo = each(lambda m, r: _dot(m.astype(BF16), r), qk, r16)
        for j, (i, d) in enumerate(chains):
            qp_ref[d, rows[j], :] = (q[j] * eg[j] - qo[j][:, LANES:2 * LANES]).astype(BF16)
            op_ref[d, rows[j], :] = qo[j][:, 0:LANES]
            chunks = pl.ds(ring + (i * GROUP) % set_chunks, GROUP)
            n_ref[d, chunks] = np_[j][:, 0:LANES].reshape(GROUP, HEAD_DIM, HEAD_DIM)
            p_ref[d, chunks] = np_[j][:, LANES:2 * LANES].astype(BF16).reshape(GROUP, HEAD_DIM, HEAD_DIM)

    def set_chains(j):
        return ([(j * PRE_GROUPS + u, 0) for u in range(PRE_GROUPS)]
                + [(n_groups - 1 - (j * PRE_GROUPS + u), 1) for u in range(PRE_GROUPS)])

    def scan_steps(j, state, out):
        st = list(state)
        for step in range(set_chunks):
            i = j * set_chunks + step
            new = []
            for d in range(2):
                n = i if d == 0 else n_chunks - 1 - i
                rows = pl.ds(pl.multiple_of(n * C, C), C)
                gr = tab[d, pl.ds(n // GROUP, 1), :]
                target = (n % GROUP) * C + (C - 1 if d == 0 else 0)
                g_last = jnp.sum(jnp.where(lane_w == target, gr, 0.0), axis=1, keepdims=True)
                s16 = st[d].astype(BF16)
                slot = (j % 2) * set_chunks + n % set_chunks
                op_ref[d, rows, :] = op_ref[d, rows, :] + _dot(qp_ref[d, rows, :], s16)
                new.append(st[d] * jnp.exp(g_last) - _dot(p_ref[d, slot], s16) + n_ref[d, slot])
            st = new
            yield
        out.extend(st)

    def run_together(*gens):
        live = list(gens)
        while live:
            for g in list(live):
                try:
                    next(g)
                except StopIteration:
                    live.remove(g)

    run_together(groups_pre(set_chains(0), 0))

    def piped(j, state):
        out = []
        run_together(groups_pre(set_chains(j), (j % 2) * set_chunks), scan_steps(j - 1, state, out))
        return tuple(out)
    zero = jnp.zeros((HEAD_DIM, HEAD_DIM), F32)
    state = lax.fori_loop(1, n_sets, piped, (zero, zero))
    run_together(scan_steps(n_sets - 1, state, []))

    def finish(rows):
        o = op_ref[0, rows, :] + op_ref[1, rows, :]
        o_ref[rows, :] = (_rms(o, gn_ref[...]) * _silu(z_ref[rows, :].astype(F32))).astype(o_ref.dtype)
    _row_chunks(S, 512, finish)


def _mix_c(proj3, conv_w, logits_rows, par, gn, layer):
    B, S, _ = proj3.shape
    n_chunks = S // DELTA_CHUNK
    width = GROUP * DELTA_CHUNK
    n_groups = S // width
    head = lambda base: pl.BlockSpec((None, S, LANES), lambda b, h: (b, 0, base + h))
    cw = lambda base: pl.BlockSpec((None, CONV_K, LANES), lambda b, h: (layer, 0, base + h))
    return pl.pallas_call(
        _mixc_body,
        grid=(B, C_HEADS),
        in_specs=[
            head(COL_CQ), head(COL_CK), head(COL_CV), head(COL_CZ),
            cw(0), cw(C_HEADS), cw(2 * C_HEADS),
            pl.BlockSpec((None, 4, None, n_groups, width), lambda b, h: (b, 0, h, 0, 0)),
            pl.BlockSpec((None, 4, width), lambda b, h: (h, 0, 0)),
            pl.BlockSpec((1, LANES), lambda b, h: (0, 0)),
        ],
        out_specs=pl.BlockSpec((None, S, LANES), lambda b, h: (b, 0, h)),
        out_shape=jax.ShapeDtypeStruct((B, S, C_W), BF16),
        scratch_shapes=[
            pltpu.VMEM((S + 16, LANES), F32),
            pltpu.VMEM((S, LANES), F32), pltpu.VMEM((S, LANES), F32), pltpu.VMEM((S, LANES), F32),
            pltpu.VMEM((4, n_groups, width), F32),
            pltpu.VMEM((2, S, LANES), BF16),
            pltpu.VMEM((2, S, LANES), F32),
            pltpu.VMEM((2, 2 * PRE_GROUPS * GROUP, HEAD_DIM, HEAD_DIM), BF16),
            pltpu.VMEM((2, 2 * PRE_GROUPS * GROUP, HEAD_DIM, HEAD_DIM), F32),
        ],
        compiler_params=_params("arbitrary", "arbitrary"),
        name="mix_c",
    )(proj3, proj3, proj3, proj3, conv_w, conv_w, conv_w, logits_rows, par, gn)


def _outproj_body(ya_ref, yb_ref, yc_ref, ga_ref, gb_ref, w_ref, h_ref, o_ref, yn_ref):
    tm = ya_ref.shape[0]

    @pl.when(pl.program_id(1) == 0)
    def _():
        def norm_rows(rows):
            yn_ref[rows, 0:A_W] = _rms(ya_ref[rows, :].astype(F32), ga_ref[...]).astype(BF16)
            yn_ref[rows, A_W:A_W + B_QW] = _rms(yb_ref[rows, :].astype(F32), gb_ref[...]).astype(BF16)
            yn_ref[rows, A_W + B_QW:D_MODEL] = yc_ref[rows, :]
        _row_chunks(tm, 128, norm_rows)

    o_ref[...] = h_ref[...] + _dot(yn_ref[...], w_ref[...].astype(BF16))


def _out_proj(ya, yb, yc, ga, gb, w_out, h2d, layer, tm=2048, tn=512):
    m = h2d.shape[0]
    return pl.pallas_call(
        _outproj_body,
        grid=(m // tm, D_MODEL // tn),
        in_specs=[
            pl.BlockSpec((tm, A_W), lambda i, j: (i, 0), **ROW_RESIDENT),
            pl.BlockSpec((tm, B_QW), lambda i, j: (i, 0), **ROW_RESIDENT),
            pl.BlockSpec((tm, C_W), lambda i, j: (i, 0), **ROW_RESIDENT),
            pl.BlockSpec((1, A_W), lambda i, j: (0, 0)),
            pl.BlockSpec((1, B_QW), lambda i, j: (0, 0)),
            pl.BlockSpec((None, D_MODEL, tn), lambda i, j: (layer, 0, j)),
            pl.BlockSpec((tm, tn), lambda i, j: (i, j)),
        ],
        out_specs=pl.BlockSpec((tm, tn), lambda i, j: (i, j)),
        out_shape=jax.ShapeDtypeStruct((m, D_MODEL), F32),
        scratch_shapes=[pltpu.VMEM((tm, D_MODEL), BF16)],
        compiler_params=_params("arbitrary", "arbitrary"),
        name="out_proj",
    )(ya, yb, yc, ga, gb, w_out, h2d)


def _moepre_body(h_ref, g_ref, wr_ref, hn_ref, aff_ref):
    xn = _rms(h_ref[...], g_ref[...])
    hi, lo = _split(xn)
    hn_ref[...] = hi
    whi, wlo = _split(wr_ref[...])
    logits = _nt(whi, hi) + (_nt(whi, lo) + _nt(wlo, hi))
    e = jnp.exp(logits - jnp.max(logits, axis=0, keepdims=True))
    aff_ref[...] = e / jnp.sum(e, axis=0, keepdims=True)


def _moe_pre(h3, gain, w_router_t, tm=1024):
    B, S, _ = h3.shape
    return pl.pallas_call(
        _moepre_body,
        grid=(B, S // tm),
        in_specs=[
            pl.BlockSpec((None, tm, D_MODEL), lambda b, i: (b, i, 0)),
            pl.BlockSpec((1, D_MODEL), lambda b, i: (0, 0)),
            pl.BlockSpec((N_EXPERTS, D_MODEL), lambda b, i: (0, 0)),
        ],
        out_specs=[
            pl.BlockSpec((None, tm, D_MODEL), lambda b, i: (b, i, 0)),
            pl.BlockSpec((None, N_EXPERTS, tm), lambda b, i: (b, 0, i)),
        ],
        out_shape=[
            jax.ShapeDtypeStruct((B, S, D_MODEL), BF16),
            jax.ShapeDtypeStruct((B, N_EXPERTS, S), F32),
        ],
        compiler_params=_params("arbitrary", "arbitrary"),
        name="moe_pre",
    )(h3, gain, w_router_t)


def _topk_body(aff_ref, rank_ref, gate_ref, offs_ref, *, cap):
    x = aff_ref[...]
    E, S = x.shape
    xb = pltpu.bitcast(x, jnp.int32)
    count = lambda mask: jnp.sum(jnp.where(mask, 1.0, 0.0), axis=1, keepdims=True)

    def value_bit(it, t):
        cand = t | jnp.left_shift(jnp.int32(1), 30 - it)
        return jnp.where(count(xb >= cand) >= cap, cand, t)
    thr = lax.fori_loop(0, 31, value_bit, jnp.zeros((E, 1), jnp.int32))

    above = xb > thr
    tied = xb == thr
    need = cap - count(above)
    idx = lax.broadcasted_iota(jnp.int32, (E, S), 1)

    index_bits = (S - 1).bit_length()

    def index_bit(it, j):
        cand = j | jnp.left_shift(jnp.int32(1), index_bits - 1 - it)
        return jnp.where(count(tied & (idx < cand)) < need, cand, j)
    jmax = lax.fori_loop(0, index_bits, index_bit, jnp.zeros((E, 1), jnp.int32))
    sel = above | (tied & (idx <= jmax) & (need > 0.0))
    gate_ref[...] = jnp.where(sel, x, 0.0)

    ri = lax.broadcasted_iota(jnp.int32, (LANES, LANES), 0)
    ci = lax.broadcasted_iota(jnp.int32, (LANES, LANES), 1)
    before = jnp.where(ri < ci, 1.0, 0.0).astype(BF16)
    lane = lax.broadcasted_iota(jnp.int32, (E, LANES), 1)
    off = jnp.zeros((E, 1), F32)
    offs = jnp.zeros((E, LANES), F32)
    self_f = jnp.where(sel, 1.0, 0.0)
    for c in range(S // LANES):
        m_c = self_f[:, c * LANES:(c + 1) * LANES]
        rank_c = _dot(m_c.astype(BF16), before)
        rank_ref[:, c * LANES:(c + 1) * LANES] = jnp.where(m_c > 0.0, rank_c + off, -1.0)
        offs = jnp.where(lane == c, off, offs)
        off = off + jnp.sum(m_c, axis=1, keepdims=True)
    offs = jnp.where(lane == S // LANES, off, offs)
    offs_ref[...] = offs.astype(jnp.int32)


def _topk(aff_t, cap):
    B, E, S = aff_t.shape
    spec = pl.BlockSpec((None, E, S), lambda b: (b, 0, 0))
    return pl.pallas_call(
        functools.partial(_topk_body, cap=cap),
        grid=(B,),
        in_specs=[spec],
        out_specs=[spec, spec, pl.BlockSpec((None, E, LANES), lambda b: (b, 0, 0))],
        out_shape=[
            jax.ShapeDtypeStruct((B, E, S), F32),
            jax.ShapeDtypeStruct((B, E, S), F32),
            jax.ShapeDtypeStruct((B, E, LANES), jnp.int32),
        ],
        compiler_params=_params("arbitrary"),
        name="topk",
    )(aff_t)


def _window_start(lo, step, cap):
    return pl.multiple_of(jnp.minimum((lo // 16) * 16 + step * SLOT_WIN, cap - SLOT_WIN), 16)


def _slot_onehot(row_f, start, first_valid, rk):
    slot = row_f + start.astype(F32)
    return (slot == rk) & (slot >= first_valid.astype(F32))


def _gather_body(offs_ref, hn_ref, rk_ref, gt_ref, xs_ref, gs_ref, *, cap):
    S = hn_ref.shape[0]
    b = pl.program_id(0)
    dq = pl.program_id(1)
    acc_ref = xs_ref
    gacc_ref = gs_ref

    def clear(e, carry):
        acc_ref[e] = jnp.zeros(acc_ref.shape[1:], acc_ref.dtype)

        @pl.when(dq == 0)
        def _():
            gacc_ref[e] = jnp.zeros(gacc_ref.shape[1:], F32)
        return carry
    lax.fori_loop(0, N_EXPERTS, clear, 0)

    row_f = lax.broadcasted_iota(jnp.int32, (SLOT_WIN, LANES), 0).astype(F32)

    experts = range(N_EXPERTS)

    def chunks(it, carry):
        cs = [it * GS_CHUNKS + u for u in range(GS_CHUNKS)]
        h_c = [hn_ref[pl.ds(pl.multiple_of(c * LANES, LANES), LANES), :] for c in cs]
        los = [[offs_ref[(b * N_EXPERTS + e) * LANES + c] for e in experts] for c in cs]
        his = [[offs_ref[(b * N_EXPERTS + e) * LANES + c + 1] for e in experts] for c in cs]
        firsts = [[(lo // 16) * 16 for lo in row] for row in los]
        starts = [[_window_start(lo, 0, cap) for lo in row] for row in los]
        hots = [[_slot_onehot(row_f, starts[u][e], starts[u][e], rk_ref[e, pl.ds(c, 1), :]) for e in experts]
                for u, c in enumerate(cs)]
        stacked = [jnp.concatenate([jnp.where(h, 1.0, 0.0).astype(BF16) for h in row], axis=0) for row in hots]
        res = [_dot(st, h) for st, h in zip(stacked, h_c)]
        for u in range(GS_CHUNKS):
            for e in experts:
                win = pl.ds(starts[u][e], SLOT_WIN)
                acc_ref[e, win, :] = acc_ref[e, win, :] + res[u][e * SLOT_WIN:(e + 1) * SLOT_WIN, :].astype(
                    acc_ref.dtype)

        @pl.when(dq == 0)
        def _():
            for u, c in enumerate(cs):
                for e in experts:
                    win = pl.ds(starts[u][e], SLOT_WIN)
                    gacc_ref[e, win, :] = gacc_ref[e, win, :] + jnp.sum(
                        jnp.where(hots[u][e], gt_ref[e, pl.ds(c, 1), :], 0.0), axis=1, keepdims=True)

        overflow = functools.reduce(
            lambda x, y: x | y, [his[u][e] - firsts[u][e] > SLOT_WIN for u in range(GS_CHUNKS) for e in experts])

        @pl.when(overflow)
        def _():
            for u, c in enumerate(cs):
                for e in experts:
                    def extra(w, carry2, u=u, c=c, e=e):
                        start = _window_start(firsts[u][e], w, cap)
                        hot = _slot_onehot(row_f, start, firsts[u][e] + w * SLOT_WIN, rk_ref[e, pl.ds(c, 1), :])
                        win2 = pl.ds(start, SLOT_WIN)
                        acc_ref[e, win2, :] = acc_ref[e, win2, :] + _dot(
                            jnp.where(hot, 1.0, 0.0).astype(BF16), h_c[u]).astype(acc_ref.dtype)

                        @pl.when(dq == 0)
                        def _():
                            gacc_ref[e, win2, :] = gacc_ref[e, win2, :] + jnp.sum(
                                jnp.where(hot, gt_ref[e, pl.ds(c, 1), :], 0.0), axis=1, keepdims=True)
                        return carry2
                    lax.fori_loop(1, (his[u][e] - firsts[u][e] + SLOT_WIN - 1) // SLOT_WIN, extra, 0)
        return carry
    lax.fori_loop(0, S // (LANES * GS_CHUNKS), chunks, 0)


def _gather(offs_flat, hn3, rank4, gate4, cap, dcols=512):
    B, S, _ = hn3.shape
    n_chunks = S // LANES
    rows = pl.BlockSpec((None, N_EXPERTS, n_chunks, LANES), lambda b, q, offs: (b, 0, 0, 0))
    return pl.pallas_call(
        functools.partial(_gather_body, cap=cap),
        grid_spec=pltpu.PrefetchScalarGridSpec(
            num_scalar_prefetch=1,
            grid=(B, D_MODEL // dcols),
            in_specs=[
                pl.BlockSpec((None, S, dcols), lambda b, q, offs: (b, 0, q)),
                rows, rows,
            ],
            out_specs=[
                pl.BlockSpec((None, N_EXPERTS, cap, dcols), lambda b, q, offs: (b, 0, 0, q)),
                pl.BlockSpec((None, N_EXPERTS, cap, LANES), lambda b, q, offs: (b, 0, 0, 0)),
            ],
        ),
        out_shape=[
            jax.ShapeDtypeStruct((B, N_EXPERTS, cap, D_MODEL), BF16),
            jax.ShapeDtypeStruct((B, N_EXPERTS, cap, LANES), F32),
        ],
        compiler_params=_params("arbitrary", "arbitrary"),
        name="gather",
    )(offs_flat, hn3, rank4, gate4)


def _ffn_body(xs_ref, wg_ref, wu_ref, wd_ref, gs_ref, y_ref, acc_ref):
    f = pl.program_id(1)
    n_f = pl.num_programs(1)
    wg = wg_ref[...].astype(BF16)
    wu = wu_ref[...].astype(BF16)
    wd = wd_ref[...].astype(BF16)
    n_b, cap, _ = xs_ref.shape
    blocks = [(b, slice(r, r + FFN_ROWS)) for b in range(n_b) for r in range(0, cap, FFN_ROWS)]
    xs = [xs_ref[b, rows, :] for b, rows in blocks]
    hg = [_dot(x, wg) for x in xs]
    hu = [_dot(x, wu) for x in xs]
    hid = [(_silu(g) * u).astype(BF16) for g, u in zip(hg, hu)]
    for (b, rows), h in zip(blocks, hid):
        acc_ref[b, rows, :] = jnp.where(f > 0, acc_ref[b, rows, :], 0.0) + _dot(h, wd)

    @pl.when(f == n_f - 1)
    def _():
        for b, rows in blocks:
            y_ref[b, rows, :] = (acc_ref[b, rows, :] * gs_ref[b, rows, :][:, 0:1]).astype(y_ref.dtype)


def _ffn(xs, w_gate, w_up, w_down, gslot, layer, tf=512):
    B, E, cap, _ = xs.shape
    return pl.pallas_call(
        _ffn_body,
        grid=(E, EXPERT_FF // tf),
        in_specs=[
            pl.BlockSpec((B, None, cap, D_MODEL), lambda e, f: (0, e, 0, 0)),
            pl.BlockSpec((None, None, D_MODEL, tf), lambda e, f: (layer, e, 0, f)),
            pl.BlockSpec((None, None, D_MODEL, tf), lambda e, f: (layer, e, 0, f)),
            pl.BlockSpec((None, None, tf, D_MODEL), lambda e, f: (layer, e, f, 0)),
            pl.BlockSpec((B, None, cap, LANES), lambda e, f: (0, e, 0, 0)),
        ],
        out_specs=pl.BlockSpec((B, None, cap, D_MODEL), lambda e, f: (0, e, 0, 0)),
        out_shape=jax.ShapeDtypeStruct((B, E, cap, D_MODEL), BF16),
        scratch_shapes=[pltpu.VMEM((B, cap, D_MODEL), F32)],
        compiler_params=_params("arbitrary", "arbitrary"),
        name="ffn",
    )(xs, w_gate, w_up, w_down, gslot)


def _scatter_body(offs_ref, h_ref, y_ref, rk_ref, o_ref, ycat_ref, *, cap):
    tp = h_ref.shape[0]
    b = pl.program_id(0)
    t = pl.program_id(2)
    row_f = lax.broadcasted_iota(jnp.int32, (SLOT_WIN, LANES), 0).astype(F32)

    experts = range(N_EXPERTS)
    n_cc = tp // LANES
    cs = [t * n_cc + cc for cc in range(n_cc)]
    rows = [slice(cc * LANES, (cc + 1) * LANES) for cc in range(n_cc)]
    los = [[offs_ref[(b * N_EXPERTS + e) * LANES + c] for e in experts] for c in cs]
    his = [[offs_ref[(b * N_EXPERTS + e) * LANES + c + 1] for e in experts] for c in cs]
    firsts = [[(lo // 16) * 16 for lo in row] for row in los]
    starts = [[_window_start(lo, 0, cap) for lo in row] for row in los]
    hots = [[_slot_onehot(row_f, starts[cc][e], starts[cc][e], rk_ref[e, pl.ds(c, 1), :]) for e in experts]
            for cc, c in enumerate(cs)]
    for cc in range(n_cc):
        for e in experts:
            ycat_ref[cc, e * SLOT_WIN:(e + 1) * SLOT_WIN, :] = y_ref[e, pl.ds(starts[cc][e], SLOT_WIN), :]
    stacked = [jnp.concatenate([jnp.where(h, 1.0, 0.0).astype(BF16) for h in row], axis=0) for row in hots]
    for cc in range(n_cc):
        o_ref[rows[cc], :] = h_ref[rows[cc], :] + _tn(stacked[cc], ycat_ref[cc])

    overflow = functools.reduce(
        lambda x, y: x | y, [his[cc][e] - firsts[cc][e] > SLOT_WIN for cc in range(n_cc) for e in experts])

    @pl.when(overflow)
    def _():
        for cc, c in enumerate(cs):
            for e in experts:
                def extra(w, carry, cc=cc, c=c, e=e):
                    start = _window_start(firsts[cc][e], w, cap)
                    hot = _slot_onehot(row_f, start, firsts[cc][e] + w * SLOT_WIN, rk_ref[e, pl.ds(c, 1), :])
                    o_ref[rows[cc], :] = o_ref[rows[cc], :] + _tn(
                        jnp.where(hot, 1.0, 0.0).astype(BF16), y_ref[e, pl.ds(start, SLOT_WIN), :])
                    return carry
                lax.fori_loop(1, (his[cc][e] - firsts[cc][e] + SLOT_WIN - 1) // SLOT_WIN, extra, 0)


def _scatter(offs_flat, h3, y, rank4, cap, tp=1024, dcols=512):
    B, S, _ = h3.shape
    n_chunks = S // LANES
    return pl.pallas_call(
        functools.partial(_scatter_body, cap=cap),
        grid_spec=pltpu.PrefetchScalarGridSpec(
            num_scalar_prefetch=1,
            grid=(B, D_MODEL // dcols, S // tp),
            in_specs=[
                pl.BlockSpec((None, tp, dcols), lambda b, q, t, offs: (b, t, q)),
                pl.BlockSpec((None, N_EXPERTS, cap, dcols), lambda b, q, t, offs: (b, 0, 0, q)),
                pl.BlockSpec((None, N_EXPERTS, n_chunks, LANES), lambda b, q, t, offs: (b, 0, 0, 0)),
            ],
            out_specs=pl.BlockSpec((None, tp, dcols), lambda b, q, t, offs: (b, t, q)),
            scratch_shapes=[pltpu.VMEM((tp // LANES, N_EXPERTS * SLOT_WIN, dcols), BF16)],
        ),
        out_shape=jax.ShapeDtypeStruct((B, S, D_MODEL), F32),
        compiler_params=_params("arbitrary", "arbitrary", "arbitrary"),
        name="scatter",
    )(offs_flat, h3, y, rank4)


def _ple_body(h_hbm, g_ref, wg_ref, p_ref, wp_ref, hblk_ref, o_ref, hn_ref, hbuf, sem):
    @pl.when(pl.program_id(1) == 0)
    def _():
        _norm_row_block(h_hbm, hbuf, sem, g_ref, hn_ref)

    gate = _sigmoid(_dot(hn_ref[...], wg_ref[...].astype(BF16)))
    emb = _dot(p_ref[...].astype(BF16), wp_ref[...].astype(BF16))
    o_ref[...] = hblk_ref[...] + gate * emb


def _ple(h2d, gain, w_gate, p2d, w_ple, layer, tm=2048, tn=256):
    m = h2d.shape[0]
    return pl.pallas_call(
        _ple_body,
        grid=(m // tm, D_MODEL // tn),
        in_specs=[
            pl.BlockSpec(memory_space=pl.ANY),
            pl.BlockSpec((1, D_MODEL), lambda i, j: (0, 0)),
            pl.BlockSpec((None, D_MODEL, tn), lambda i, j: (layer, 0, j)),
            pl.BlockSpec((None, tm, PLE_DIM), lambda i, j: (layer, i, 0), **ROW_RESIDENT),
            pl.BlockSpec((None, PLE_DIM, tn), lambda i, j: (layer, 0, j)),
            pl.BlockSpec((tm, tn), lambda i, j: (i, j)),
        ],
        out_specs=pl.BlockSpec((tm, tn), lambda i, j: (i, j)),
        out_shape=jax.ShapeDtypeStruct((m, D_MODEL), F32),
        scratch_shapes=[pltpu.VMEM((tm, D_MODEL), BF16), pltpu.VMEM((tm, D_MODEL), F32),
                        pltpu.SemaphoreType.DMA(())],
        compiler_params=_params("arbitrary", "arbitrary"),
        name="ple",
    )(h2d, gain, w_gate, p2d, w_ple, h2d)


def _final_body(h_ref, g_ref, o_ref):
    o_ref[...] = _rms(h_ref[...], g_ref[...])


def _final_norm(h2d, gain, tm=1024):
    m = h2d.shape[0]
    return pl.pallas_call(
        _final_body,
        grid=(m // tm,),
        in_specs=[pl.BlockSpec((tm, D_MODEL), lambda i: (i, 0)), pl.BlockSpec((1, D_MODEL), lambda i: (0, 0))],
        out_specs=pl.BlockSpec((tm, D_MODEL), lambda i: (i, 0)),
        out_shape=jax.ShapeDtypeStruct((m, D_MODEL), F32),
        compiler_params=_params("arbitrary"),
        name="final_norm",
    )(h2d, gain)


def _rope_cs(pos, dim, theta):
    inv = theta ** (-jnp.arange(0, dim, 2, dtype=F32) / dim)
    ang = pos.astype(F32)[:, None] * inv[None, :]
    return jnp.cos(ang), jnp.sin(ang)


def _tables(S):
    cos, sin = _rope_cs(jnp.arange(S), ROPE_DIM, ROPE_THETA)
    z = jnp.zeros_like(sin)
    rest = HEAD_DIM - ROPE_DIM
    a_c = jnp.concatenate([cos, cos, jnp.ones((S, rest), F32)], axis=1)
    a_sp = jnp.concatenate([z, sin, jnp.zeros((S, rest), F32)], axis=1)
    a_sm = jnp.concatenate([-sin, z, jnp.zeros((S, rest), F32)], axis=1)
    rows = S // GRID_W
    rc, rs = _rope_cs(jnp.repeat(jnp.arange(rows), GRID_W), HEAD_DIM // 2, AXIAL_THETA)
    cc, cs = _rope_cs(jnp.tile(jnp.arange(GRID_W), rows), HEAD_DIM // 2, AXIAL_THETA)
    zz = jnp.zeros_like(rs)
    b_c = jnp.concatenate([rc, rc, cc, cc], axis=1)
    b_sp = jnp.concatenate([zz, rs, zz, cs], axis=1)
    b_sm = jnp.concatenate([-rs, zz, -cs, zz], axis=1)
    return (a_c, a_sp, a_sm), (b_c, b_sp, b_sm)


def kernel(x, p, norm_mix, w_in, conv_w, q_norm, k_norm, out_norm_a, out_norm_b, gdn_a_log, gdn_dt_bias,
           gdn_norm, w_out, norm_moe, w_router, w_gate, w_up, w_down, norm_ple, w_ple, w_ple_gate, norm_final):
    B, S, D = x.shape
    depth = w_in.shape[0]
    assert D == D_MODEL and S % 1024 == 0
    cap = EC_CAPACITY * S // N_EXPERTS
    width = GROUP * DELTA_CHUNK
    tabs_a, tabs_b = _tables(S)
    p2d = p.reshape(depth, B * S, PLE_DIM)
    row = lambda v: v.reshape(1, -1)

    w_in_t = jnp.swapaxes(w_in, 1, 2)
    h = x.reshape(B * S, D)
    for i in range(depth):
        proj, small = _proj(h, row(norm_mix[i]), w_in_t, i)
        proj3 = proj.reshape(B, S, D_MAIN)

        y_a = _mix_a(proj3, tabs_a)
        y_b = _mix_b(proj3, tabs_b, row(q_norm[i]), row(k_norm[i]))

        lg = small[:, :D_SMALL].reshape(B, S // width, width, 4, C_HEADS).transpose(0, 3, 4, 1, 2)
        par = jnp.concatenate([gdn_a_log[i], gdn_dt_bias[i]], axis=0)
        par = jnp.broadcast_to(par.T[:, :, None], (C_HEADS, 4, width))
        y_c = _mix_c(proj3, conv_w, lg, par, row(gdn_norm[i]), i)

        h = _out_proj(y_a.reshape(B * S, A_W), y_b.reshape(B * S, B_QW), y_c.reshape(B * S, C_W),
                      row(out_norm_a[i]), row(out_norm_b[i]), w_out, h, i)

        hn, aff_t = _moe_pre(h.reshape(B, S, D), row(norm_moe[i]), w_router[i].T)
        rank, gates, offs = _topk(aff_t, cap)
        offs_flat = offs.reshape(-1)
        rank4 = rank.reshape(B, N_EXPERTS, S // LANES, LANES)
        gate4 = gates.reshape(B, N_EXPERTS, S // LANES, LANES)
        xs, gslot = _gather(offs_flat, hn, rank4, gate4, cap)
        y = _ffn(xs, w_gate, w_up, w_down, gslot, i)
        h = _scatter(offs_flat, h.reshape(B, S, D), y, rank4, cap).reshape(B * S, D)

        h = _ple(h, row(norm_ple[i]), w_ple_gate, p2d, w_ple, i)
    return _final_norm(h, row(norm_final)).reshape(B, S, D)
```

```python
import functools

import jax
import jax.numpy as jnp
from jax import lax
from jax.experimental import pallas as pl
from jax.experimental.pallas import tpu as pltpu

F32 = jnp.float32
BF16 = jnp.bfloat16

D_MODEL = 2048
HEAD_DIM = 128
A_HEADS = 4
B_HEADS = 8
B_KV_HEADS = 2
B_GROUP = B_HEADS // B_KV_HEADS
C_HEADS = 4
A_W = A_HEADS * HEAD_DIM
B_QW = B_HEADS * HEAD_DIM
B_KVW = B_KV_HEADS * HEAD_DIM
C_W = C_HEADS * HEAD_DIM
DILATED_PATTERNS = ((128, 1), (512, 4), (2048, 16))
ROPE_THETA = 500000.0
ROPE_DIM = HEAD_DIM // 4
AXIAL_THETA = 10000.0
GRID_W = 64
CONV_K = 5
DELTA_CHUNK = 64
N_EXPERTS = 16
EC_CAPACITY = 2
EXPERT_FF = D_MODEL // 2
PLE_DIM = 256
NORM_EPS = 1e-6
D_MAIN = 3 * A_W + B_QW + 2 * B_KVW + 4 * C_W
D_SMALL = 4 * C_HEADS

COL_AQ, COL_AK, COL_AV = 0, 4, 8
COL_BQ, COL_BK, COL_BV = 12, 20, 22
COL_CQ, COL_CK, COL_CV, COL_CZ = 24, 28, 32, 36

LANES = 128
VMEM_LIMIT = 56 * 1024 * 1024
NEG_BIG = -1e30
ROW_RESIDENT = dict(pipeline_mode=pl.Buffered(1))
GROUP = 4
PRE_GROUPS = 4
A_BLOCKS = 8
SLOT_WIN = 48
FFN_ROWS = 512
GS_CHUNKS = 8


def _params(*sem):
    return pltpu.CompilerParams(dimension_semantics=sem, vmem_limit_bytes=VMEM_LIMIT)


def _nt(a, b):
    return lax.dot_general(a, b, (((1,), (1,)), ((), ())), preferred_element_type=F32)


def _tn(a, b):
    return lax.dot_general(a, b, (((0,), (0,)), ((), ())), preferred_element_type=F32)


def _dot(a, b):
    return jnp.dot(a, b, preferred_element_type=F32)


def _split(x):
    hi = x.astype(BF16)
    lo = (x - hi.astype(F32)).astype(BF16)
    return hi, lo


def _dotb(a, b):
    return _dot(a.astype(BF16), b.astype(BF16))


def _dot3(a, b):
    ah, al = _split(a)
    bh, bl = _split(b)
    return _dot(ah, bh) + (_dot(ah, bl) + _dot(al, bh))


def _rms(x, gain):
    return x * lax.rsqrt(jnp.mean(x * x, axis=-1, keepdims=True) + NORM_EPS) * gain


def _sigmoid(x):
    return 1.0 / (1.0 + jnp.exp(-x))


def _silu(x):
    return x * _sigmoid(x)


def _row_chunks(n_rows, chunk, fn):
    def body(c, carry):
        fn(pl.ds(pl.multiple_of(c * chunk, chunk), chunk))
        return carry
    lax.fori_loop(0, n_rows // chunk, body, 0)


def _norm_row_block(x_hbm, xbuf, sem, g_ref, dst_ref):
    tm = xbuf.shape[0]
    i = pl.program_id(0)

    def copy(blk):
        return pltpu.make_async_copy(x_hbm.at[pl.ds(pl.multiple_of(blk * tm, tm), tm), :], xbuf, sem)

    @pl.when(i == 0)
    def _():
        copy(0).start()
    copy(i).wait()

    def norm_rows(rows):
        dst_ref[rows, :] = _rms(xbuf[rows, :], g_ref[...]).astype(BF16)
    _row_chunks(tm, 128, norm_rows)

    @pl.when(i + 1 < pl.num_programs(0))
    def _():
        copy(i + 1).start()


def _proj_body(x_hbm, g_ref, w_ref, ws_ref, o_ref, os_ref, xn_ref, xbuf, sem):
    @pl.when(pl.program_id(1) == 0)
    def _():
        _norm_row_block(x_hbm, xbuf, sem, g_ref, xn_ref)
        sub = lax.broadcasted_iota(jnp.int32, ws_ref.shape, 0)
        os_ref[...] = _nt(xn_ref[...], jnp.where(sub < D_SMALL, ws_ref[...], 0.0).astype(BF16))

    o_ref[...] = _nt(xn_ref[...], w_ref[...].astype(BF16)).astype(o_ref.dtype)


def _proj(h2d, gain, w_in_t, layer, tm=2048, tn=512):
    m = h2d.shape[0]
    return pl.pallas_call(
        _proj_body,
        grid=(m // tm, D_MAIN // tn),
        in_specs=[
            pl.BlockSpec(memory_space=pl.ANY),
            pl.BlockSpec((1, D_MODEL), lambda i, j: (0, 0)),
            pl.BlockSpec((None, tn, D_MODEL), lambda i, j: (layer, j, 0)),
            pl.BlockSpec((None, LANES, D_MODEL), lambda i, j: (layer, D_MAIN // LANES, 0)),
        ],
        out_specs=[
            pl.BlockSpec((tm, tn), lambda i, j: (i, j)),
            pl.BlockSpec((tm, LANES), lambda i, j: (i, 0)),
        ],
        out_shape=[
            jax.ShapeDtypeStruct((m, D_MAIN), BF16),
            jax.ShapeDtypeStruct((m, LANES), F32),
        ],
        scratch_shapes=[pltpu.VMEM((tm, D_MODEL), BF16), pltpu.VMEM((tm, D_MODEL), F32),
                        pltpu.SemaphoreType.DMA(())],
        compiler_params=_params("arbitrary", "arbitrary"),
        name="proj",
    )(h2d, gain, w_in_t, w_in_t)


def _mixa_body(q_ref, k_ref, v_ref, c_ref, sp_ref, sm_ref, o_ref,
               qf, kf, vf, qd, kp, vp, acc, den, mrun):
    S = q_ref.shape[0]
    half = 64
    blk = 128
    scale = HEAD_DIM ** -0.5

    def prep(rows):
        def rope(x):
            return (x * c_ref[rows, :] + pltpu.roll(x, ROPE_DIM // 2, 1) * sp_ref[rows, :]
                    + pltpu.roll(x, LANES - ROPE_DIM // 2, 1) * sm_ref[rows, :])
        qf[rows, :] = rope(q_ref[rows, :].astype(F32)) * scale
        kf[rows, :] = rope(k_ref[rows, :].astype(F32))
        vf[rows, :] = v_ref[rows, :].astype(F32)
        acc[rows, :] = jnp.zeros((512, LANES), F32)
        den[rows, :] = jnp.zeros((512, LANES), F32)
        mrun[rows, :] = jnp.full((512, LANES), NEG_BIG, F32)
    _row_chunks(S, 512, prep)

    row = lax.broadcasted_iota(jnp.int32, (blk, 2 * blk), 0)
    col = lax.broadcasted_iota(jnp.int32, (blk, 2 * blk), 1)
    in_band = jnp.abs(col - row - half) <= half

    for window, dil in DILATED_PATTERNS:
        assert window // (2 * dil) == half
        L = S // dil
        nblk = L // blk
        stride_k = L + 2 * half
        for j in range(dil):
            for base in (j * stride_k, j * stride_k + half + L):
                kp[base:base + half, :] = jnp.zeros((half, LANES), BF16)
                vp[base:base + half, :] = jnp.zeros((half, 2 * LANES), BF16)

        def fold(r, carry, dil=dil, L=L, stride_k=stride_k):
            def piece(c, carry2):
                src = pl.ds(r + c * (256 * dil), 256, stride=dil) if dil > 1 else pl.ds(
                    pl.multiple_of(c * 256, 256), 256)
                dst = pl.ds(pl.multiple_of(r * stride_k + half + c * 256, half), 256)
                qd[pl.ds(pl.multiple_of(r * L + c * 256, 256), 256), :] = qf[src, :].astype(BF16)
                kp[dst, :] = kf[src, :].astype(BF16)
                vp[dst, 0:LANES] = vf[src, :].astype(BF16)
                vp[dst, LANES:2 * LANES] = jnp.ones((256, LANES), BF16)
                return carry2
            lax.fori_loop(0, L // 256, piece, 0)
            return carry
        lax.fori_loop(0, dil, fold, 0)

        def blocks(it, carry, dil=dil, L=L, nblk=nblk, stride_k=stride_k):
            items = [it * A_BLOCKS + u for u in range(A_BLOCKS)]
            rs = [w // nblk for w in items]
            nbs = [w % nblk for w in items]
            q0 = [pl.multiple_of(r * L + nb * blk, blk) for r, nb in zip(rs, nbs)]
            k0 = [pl.multiple_of(r * stride_k + nb * blk, blk) for r, nb in zip(rs, nbs)]
            s = [_nt(qd[pl.ds(a, blk), :], kp[pl.ds(b, 2 * blk), :]) for a, b in zip(q0, k0)]
            kpos = [nb * blk - half + col for nb in nbs]
            s = [jnp.where(in_band & (kp_ >= 0) & (kp_ < L), s_, NEG_BIG) for s_, kp_ in zip(s, kpos)]
            m_b = [jnp.max(s_, axis=1, keepdims=True) for s_ in s]
            e = [jnp.exp(s_ - m_).astype(BF16) for s_, m_ in zip(s, m_b)]
            od = [_dot(e_, vp[pl.ds(b, 2 * blk), :]) for e_, b in zip(e, k0)]
            for r, nb, m_, od_ in zip(rs, nbs, m_b, od):
                rows = (pl.ds(r + nb * (blk * dil), blk, stride=dil) if dil > 1
                        else pl.ds(pl.multiple_of(nb * blk, blk), blk))
                m_old = mrun[rows, :]
                m_new = jnp.maximum(m_old, m_)
                a_old = jnp.exp(m_old - m_new)
                a_new = jnp.exp(m_ - m_new)
                acc[rows, :] = acc[rows, :] * a_old + od_[:, 0:LANES] * a_new
                den[rows, :] = den[rows, :] * a_old + od_[:, LANES:2 * LANES] * a_new
                mrun[rows, :] = m_new
            return carry
        lax.fori_loop(0, (dil * nblk) // A_BLOCKS, blocks, 0)

    def finish(rows):
        o_ref[rows, :] = (acc[rows, :] / den[rows, :]).astype(o_ref.dtype)
    _row_chunks(S, 512, finish)


def _mix_a(proj3, tabs):
    B, S, _ = proj3.shape
    max_dil = max(d for _, d in DILATED_PATTERNS)
    head = lambda base: pl.BlockSpec((None, S, LANES), lambda b, h: (b, 0, base + h))
    tab = pl.BlockSpec((S, LANES), lambda b, h: (0, 0))
    return pl.pallas_call(
        _mixa_body,
        grid=(B, A_HEADS),
        in_specs=[head(COL_AQ), head(COL_AK), head(COL_AV), tab, tab, tab],
        out_specs=pl.BlockSpec((None, S, LANES), lambda b, h: (b, 0, h)),
        out_shape=jax.ShapeDtypeStruct((B, S, A_W), BF16),
        scratch_shapes=[
            pltpu.VMEM((S, LANES), F32), pltpu.VMEM((S, LANES), F32), pltpu.VMEM((S, LANES), F32),
            pltpu.VMEM((S, LANES), BF16),
            pltpu.VMEM((S + max_dil * LANES, LANES), BF16),
            pltpu.VMEM((S + max_dil * LANES, 2 * LANES), BF16),
            pltpu.VMEM((S, LANES), F32), pltpu.VMEM((S, LANES), F32), pltpu.VMEM((S, LANES), F32),
        ],
        compiler_params=_params("arbitrary", "arbitrary"),
        name="mix_a",
    )(proj3, proj3, proj3, *tabs)


def _mixb_body(q_ref, k_ref, v_ref, c_ref, sp_ref, sm_ref, qg_ref, kg_ref, o_ref,
               kt_ref, va_ref, q4_ref, m_ref, acc_ref, *, tk):
    S = k_ref.shape[0]
    tq = q_ref.shape[0]
    qi = pl.program_id(2)
    quarter = HEAD_DIM // 4

    def rope(x, rows):
        return (x * c_ref[rows, :] + pltpu.roll(x, quarter, 1) * sp_ref[rows, :]
                + pltpu.roll(x, LANES - quarter, 1) * sm_ref[rows, :])

    @pl.when(qi == 0)
    def _():
        def kv_chunk(c, carry):
            rows = pl.ds(pl.multiple_of(c * tk, tk), tk)
            k = rope(_rms(k_ref[rows, :].astype(F32), kg_ref[...]), rows)
            kt_ref[c] = k.T.astype(BF16)
            va_ref[rows, 0:LANES] = v_ref[rows, :]
            va_ref[rows, LANES:2 * LANES] = jnp.ones((tk, LANES), BF16)
            return carry
        lax.fori_loop(0, S // tk, kv_chunk, 0)

    rows_q = pl.ds(pl.multiple_of(qi * tq, tq), tq)
    for g in range(B_GROUP):
        q = _rms(q_ref[:, g * LANES:(g + 1) * LANES].astype(F32), qg_ref[...])
        q = rope(q, rows_q) * (HEAD_DIM ** -0.5)
        q4_ref[g * tq:(g + 1) * tq, :] = q.astype(BF16)
    m_ref[...] = jnp.full(m_ref.shape, NEG_BIG, F32)
    acc_ref[...] = jnp.zeros(acc_ref.shape, F32)

    def kv_step(c, carry):
        rows = pl.ds(pl.multiple_of(c * tk, tk), tk)
        s = _dot(q4_ref[...], kt_ref[c])
        m_old = m_ref[...]
        m_new = jnp.maximum(m_old, jnp.max(s, axis=1, keepdims=True))
        alpha = jnp.exp(m_old - m_new)
        p = jnp.exp(s - jnp.concatenate([m_new] * (tk // LANES), axis=1))
        acc_ref[...] = acc_ref[...] * jnp.concatenate([alpha, alpha], axis=1) + _dot(p.astype(BF16), va_ref[rows, :])
        m_ref[...] = m_new
        return carry
    lax.fori_loop(0, S // tk, kv_step, 0)

    o = acc_ref[:, 0:LANES] / acc_ref[:, LANES:2 * LANES]
    for g in range(B_GROUP):
        o_ref[:, g * LANES:(g + 1) * LANES] = o[g * tq:(g + 1) * tq, :].astype(o_ref.dtype)


def _mix_b(proj3, tabs, q_gain, k_gain, tq=1024, tk=512):
    B, S, _ = proj3.shape
    qw = B_GROUP * LANES
    tab = pl.BlockSpec((S, LANES), lambda b, h, i: (0, 0))
    gain = pl.BlockSpec((1, LANES), lambda b, h, i: (0, 0))
    return pl.pallas_call(
        functools.partial(_mixb_body, tk=tk),
        grid=(B, B_KV_HEADS, S // tq),
        in_specs=[
            pl.BlockSpec((None, tq, qw), lambda b, h, i: (b, i, COL_BQ // B_GROUP + h)),
            pl.BlockSpec((None, S, LANES), lambda b, h, i: (b, 0, COL_BK + h)),
            pl.BlockSpec((None, S, LANES), lambda b, h, i: (b, 0, COL_BV + h)),
            tab, tab, tab, gain, gain,
        ],
        out_specs=pl.BlockSpec((None, tq, qw), lambda b, h, i: (b, i, h)),
        out_shape=jax.ShapeDtypeStruct((B, S, B_QW), BF16),
        scratch_shapes=[
            pltpu.VMEM((S // tk, LANES, tk), BF16),
            pltpu.VMEM((S, 2 * LANES), BF16),
            pltpu.VMEM((B_GROUP * tq, LANES), BF16),
            pltpu.VMEM((B_GROUP * tq, LANES), F32),
            pltpu.VMEM((B_GROUP * tq, 2 * LANES), F32),
        ],
        compiler_params=_params("arbitrary", "arbitrary", "arbitrary"),
        name="mix_b",
    )(proj3, proj3, proj3, *tabs, q_gain, k_gain)


def _softplus(x):
    return jnp.maximum(x, 0.0) + jnp.log(1.0 + jnp.exp(-jnp.abs(x)))


def _mixc_body(q_ref, k_ref, v_ref, z_ref, cwq_ref, cwk_ref, cwv_ref, lg_ref, par_ref, gn_ref, o_ref,
               xp, qn, kn, vn, tab, qp_ref, op_ref, p_ref, n_ref):
    S = q_ref.shape[0]
    C = DELTA_CHUNK
    n_chunks = S // C
    pad = 8

    for src, cw, dst, kind in ((q_ref, cwq_ref, qn, "q"), (k_ref, cwk_ref, kn, "k"), (v_ref, cwv_ref, vn, "v")):
        xp[0:pad, :] = jnp.zeros((pad, LANES), F32)
        xp[pad + S:2 * pad + S, :] = jnp.zeros((pad, LANES), F32)

        def load(rows, src=src):
            xp[pl.ds(pl.multiple_of(rows.start + pad, pad), 512), :] = src[rows, :].astype(F32)
        _row_chunks(S, 512, load)
        for c in range(S // 512):
            base = pad - CONV_K // 2 + c * 512
            y = xp[base:base + 512, :] * cw[0:1, :]
            for j in range(1, CONV_K):
                y = y + xp[base + j:base + j + 512, :] * cw[j:j + 1, :]
            y = _silu(y)
            if kind != "v":
                y = y * lax.rsqrt(jnp.sum(y * y, axis=-1, keepdims=True) + NORM_EPS)
            if kind == "q":
                y = y * (HEAD_DIM ** -0.5)
            dst[c * 512:(c + 1) * 512, :] = y

    W = GROUP * C
    n_groups = S // W
    set_chunks = PRE_GROUPS * GROUP
    n_sets = n_chunks // set_chunks
    ri = lax.broadcasted_iota(jnp.int32, (W, W), 0)
    ci = lax.broadcasted_iota(jnp.int32, (W, W), 1)
    eye = ri == ci
    eye_f = jnp.where(eye, 1.0, 0.0)
    same_block = [jnp.right_shift(ri, s) == jnp.right_shift(ci, s) for s in (3, 4, 5, 6)]
    same_chunk = same_block[-1]
    chunk_start = jnp.right_shift(ri, 6) * C
    stack_mask = (jnp.right_shift(lax.broadcasted_iota(jnp.int32, (GROUP * HEAD_DIM, W), 0), 7)
                  == jnp.right_shift(lax.broadcasted_iota(jnp.int32, (GROUP * HEAD_DIM, W), 1), 6))
    lane_w = lax.broadcasted_iota(jnp.int32, (1, W), 1)
    for d in range(2):
        g = -jnp.exp(par_ref[d:d + 1, :]) * _softplus(lg_ref[2 + d] + par_ref[2 + d:3 + d, :])
        cum = same_chunk & ((ri <= ci) if d == 0 else (ri >= ci))
        tab[d] = _dot3(g, jnp.where(cum, 1.0, 0.0))
        tab[2 + d] = _sigmoid(lg_ref[d])

    def groups_pre(chains, ring):
        each = lambda fn, *lists: [fn(*args) for args in zip(*lists)]
        ds = [d for _, d in chains]
        rows = [pl.ds(pl.multiple_of(i * W, W), W) for i, _ in chains]
        q = [qn[r, :] for r in rows]
        k = [kn[r, :] for r in rows]
        v = [vn[r, :] for r in rows]
        gr = [tab[d, pl.ds(i, 1), :] for i, d in chains]
        br = [tab[2 + d, pl.ds(i, 1), :] for i, d in chains]
        to_col = lambda mask, r: jnp.sum(jnp.where(mask, r, 0.0), axis=1, keepdims=True)
        gcol = [to_col(eye, g) for g in gr]
        bcol = [to_col(eye, b) for b in br]
        glast = [to_col(ci == (chunk_start + (C - 1) if d == 0 else chunk_start), g) for d, g in zip(ds, gr)]
        incl = [same_chunk & ((ci <= ri) if d == 0 else (ci >= ri)) for d in ds]
        strict = [same_chunk & ((ci < ri) if d == 0 else (ci > ri)) for d in ds]
        decay = each(lambda m, gc, g: jnp.exp(jnp.where(m, gc - g, NEG_BIG)), incl, gcol, gr)
        kb = each(lambda a, b: a * b, k, bcol)
        k16 = [a.astype(BF16) for a in k]
        kk = each(lambda a, b: _nt(a.astype(BF16), b), kb, k16)
        yield
        a = each(lambda m, p, dc: jnp.where(m, p * dc, 0.0), strict, kk, decay)
        eg = [jnp.exp(g) for g in gcol]
        rhs = each(lambda vv, b, kbb, e: jnp.concatenate([vv * b, kbb * e], axis=1), v, bcol, kb, eg)
        x = [jnp.where(same_block[0], -m, 0.0) for m in a]
        s1 = [eye_f + m for m in x]
        x2 = each(_dotb, x, x)
        yield
        x2s1 = each(_dotb, x2, s1)
        x4 = each(_dotb, x2, x2)
        yield
        s2 = each(lambda p, m: p + m, s1, x2s1)
        t = each(lambda p, m, n: p + _dotb(m, n), s2, x4, s2)
        yield
        for lvl in range(1, len(same_block)):
            off = same_block[lvl] & jnp.logical_not(same_block[lvl - 1])
            y = each(lambda m, tt: _dotb(jnp.where(off, m, 0.0), tt), a, t)
            yield
            t = each(lambda tt, yy: tt - _dotb(tt, yy), t, y)
            yield
        r16 = each(lambda tt, r: _dotb(tt, r).astype(BF16), t, rhs)
        yield
        qk = each(lambda a_, b_, dc: _nt(a_.astype(BF16), b_) * dc, q, k16, decay)
        yield
        stacked = each(lambda kk_, gl, gc: jnp.where(
            stack_mask, jnp.concatenate([(kk_ * jnp.exp(gl - gc)).T] * GROUP, axis=0), 0.0).astype(BF16),
            k, glast, gcol)
        np_ = each(_dot, stacked, r16)
        yield
        qo = each(lambda m, r: _dot(m.astype(BF16), r), qk, r16)
        for j, (i, d) in enumerate(chains):
            qp_ref[d, rows[j], :] = (q[j] * eg[j] - qo[j][:, LANES:2 * LANES]).astype(BF16)
            op_ref[d, rows[j], :] = qo[j][:, 0:LANES]
            chunks = pl.ds(ring + (i * GROUP) % set_chunks, GROUP)
            n_ref[d, chunks] = np_[j][:, 0:LANES].reshape(GROUP, HEAD_DIM, HEAD_DIM)
            p_ref[d, chunks] = np_[j][:, LANES:2 * LANES].astype(BF16).reshape(GROUP, HEAD_DIM, HEAD_DIM)

    def set_chains(j):
        return ([(j * PRE_GROUPS + u, 0) for u in range(PRE_GROUPS)]
                + [(n_groups - 1 - (j * PRE_GROUPS + u), 1) for u in range(PRE_GROUPS)])

    def scan_steps(j, state, out):
        st = list(state)
        for step in range(set_chunks):
            i = j * set_chunks + step
            new = []
            for d in range(2):
                n = i if d == 0 else n_chunks - 1 - i
                rows = pl.ds(pl.multiple_of(n * C, C), C)
                gr = tab[d, pl.ds(n // GROUP, 1), :]
                target = (n % GROUP) * C + (C - 1 if d == 0 else 0)
                g_last = jnp.sum(jnp.where(lane_w == target, gr, 0.0), axis=1, keepdims=True)
                s16 = st[d].astype(BF16)
                slot = (j % 2) * set_chunks + n % set_chunks
                op_ref[d, rows, :] = op_ref[d, rows, :] + _dot(qp_ref[d, rows, :], s16)
                new.append(st[d] * jnp.exp(g_last) - _dot(p_ref[d, slot], s16) + n_ref[d, slot])
            st = new
            yield
        out.extend(st)

    def run_together(*gens):
        live = list(gens)
        while live:
            for g in list(live):
                try:
                    next(g)
                except StopIteration:
                    live.remove(g)

    run_together(groups_pre(set_chains(0), 0))

    def piped(j, state):
        out = []
        run_together(groups_pre(set_chains(j), (j % 2) * set_chunks), scan_steps(j - 1, state, out))
        return tuple(out)
    zero = jnp.zeros((HEAD_DIM, HEAD_DIM), F32)
    state = lax.fori_loop(1, n_sets, piped, (zero, zero))
    run_together(scan_steps(n_sets - 1, state, []))

    def finish(rows):
        o = op_ref[0, rows, :] + op_ref[1, rows, :]
        o_ref[rows, :] = (_rms(o, gn_ref[...]) * _silu(z_ref[rows, :].astype(F32))).astype(o_ref.dtype)
    _row_chunks(S, 512, finish)


def _mix_c(proj3, conv_w, logits_rows, par, gn, layer):
    B, S, _ = proj3.shape
    n_chunks = S // DELTA_CHUNK
    width = GROUP * DELTA_CHUNK
    n_groups = S // width
    head = lambda base: pl.BlockSpec((None, S, LANES), lambda b, h: (b, 0, base + h))
    cw = lambda base: pl.BlockSpec((None, CONV_K, LANES), lambda b, h: (layer, 0, base + h))
    return pl.pallas_call(
        _mixc_body,
        grid=(B, C_HEADS),
        in_specs=[
            head(COL_CQ), head(COL_CK), head(COL_CV), head(COL_CZ),
            cw(0), cw(C_HEADS), cw(2 * C_HEADS),
            pl.BlockSpec((None, 4, None, n_groups, width), lambda b, h: (b, 0, h, 0, 0)),
            pl.BlockSpec((None, 4, width), lambda b, h: (h, 0, 0)),
            pl.BlockSpec((1, LANES), lambda b, h: (0, 0)),
        ],
        out_specs=pl.BlockSpec((None, S, LANES), lambda b, h: (b, 0, h)),
        out_shape=jax.ShapeDtypeStruct((B, S, C_W), BF16),
        scratch_shapes=[
            pltpu.VMEM((S + 16, LANES), F32),
            pltpu.VMEM((S, LANES), F32), pltpu.VMEM((S, LANES), F32), pltpu.VMEM((S, LANES), F32),
            pltpu.VMEM((4, n_groups, width), F32),
            pltpu.VMEM((2, S, LANES), BF16),
            pltpu.VMEM((2, S, LANES), F32),
            pltpu.VMEM((2, 2 * PRE_GROUPS * GROUP, HEAD_DIM, HEAD_DIM), BF16),
            pltpu.VMEM((2, 2 * PRE_GROUPS * GROUP, HEAD_DIM, HEAD_DIM), F32),
        ],
        compiler_params=_params("arbitrary", "arbitrary"),
        name="mix_c",
    )(proj3, proj3, proj3, proj3, conv_w, conv_w, conv_w, logits_rows, par, gn)


def _outproj_body(ya_hbm, yb_hbm, yc_hbm, ga_ref, gb_ref, w_ref, h_ref, o_ref, yn_ref, abuf, bbuf, cbuf, sems):
    tm = abuf.shape[0]
    i = pl.program_id(0)

    def copies(blk):
        rows = pl.ds(pl.multiple_of(blk * tm, tm), tm)
        return [pltpu.make_async_copy(src.at[rows, :], dst, sems.at[k])
                for k, (src, dst) in enumerate(((ya_hbm, abuf), (yb_hbm, bbuf), (yc_hbm, cbuf)))]

    @pl.when(pl.program_id(1) == 0)
    def _():
        @pl.when(i == 0)
        def _():
            for c in copies(0):
                c.start()
        for c in copies(i):
            c.wait()

        def norm_rows(rows):
            yn_ref[rows, 0:A_W] = _rms(abuf[rows, :].astype(F32), ga_ref[...]).astype(BF16)
            yn_ref[rows, A_W:A_W + B_QW] = _rms(bbuf[rows, :].astype(F32), gb_ref[...]).astype(BF16)
            yn_ref[rows, A_W + B_QW:D_MODEL] = cbuf[rows, :]
        _row_chunks(tm, 128, norm_rows)

        @pl.when(i + 1 < pl.num_programs(0))
        def _():
            for c in copies(i + 1):
                c.start()

    o_ref[...] = h_ref[...] + _dot(yn_ref[...], w_ref[...].astype(BF16))


def _out_proj(ya, yb, yc, ga, gb, w_out, h2d, layer, tm=2048, tn=512):
    m = h2d.shape[0]
    any_spec = pl.BlockSpec(memory_space=pl.ANY)
    return pl.pallas_call(
        _outproj_body,
        grid=(m // tm, D_MODEL // tn),
        in_specs=[
            any_spec, any_spec, any_spec,
            pl.BlockSpec((1, A_W), lambda i, j: (0, 0)),
            pl.BlockSpec((1, B_QW), lambda i, j: (0, 0)),
            pl.BlockSpec((None, D_MODEL, tn), lambda i, j: (layer, 0, j)),
            pl.BlockSpec((tm, tn), lambda i, j: (i, j)),
        ],
        out_specs=pl.BlockSpec((tm, tn), lambda i, j: (i, j)),
        out_shape=jax.ShapeDtypeStruct((m, D_MODEL), F32),
        scratch_shapes=[pltpu.VMEM((tm, D_MODEL), BF16), pltpu.VMEM((tm, A_W), BF16), pltpu.VMEM((tm, B_QW), BF16),
                        pltpu.VMEM((tm, C_W), BF16), pltpu.SemaphoreType.DMA((3,))],
        compiler_params=_params("arbitrary", "arbitrary"),
        name="out_proj",
    )(ya, yb, yc, ga, gb, w_out, h2d)


def _moepre_body(h_ref, g_ref, wr_ref, hn_ref, aff_ref):
    xn = _rms(h_ref[...], g_ref[...])
    hi, lo = _split(xn)
    hn_ref[...] = hi
    whi, wlo = _split(wr_ref[...])
    logits = _nt(whi, hi) + (_nt(whi, lo) + _nt(wlo, hi))
    e = jnp.exp(logits - jnp.max(logits, axis=0, keepdims=True))
    aff_ref[...] = e / jnp.sum(e, axis=0, keepdims=True)


def _moe_pre(h3, gain, w_router_t, tm=1024):
    B, S, _ = h3.shape
    return pl.pallas_call(
        _moepre_body,
        grid=(B, S // tm),
        in_specs=[
            pl.BlockSpec((None, tm, D_MODEL), lambda b, i: (b, i, 0)),
            pl.BlockSpec((1, D_MODEL), lambda b, i: (0, 0)),
            pl.BlockSpec((N_EXPERTS, D_MODEL), lambda b, i: (0, 0)),
        ],
        out_specs=[
            pl.BlockSpec((None, tm, D_MODEL), lambda b, i: (b, i, 0)),
            pl.BlockSpec((None, N_EXPERTS, tm), lambda b, i: (b, 0, i)),
        ],
        out_shape=[
            jax.ShapeDtypeStruct((B, S, D_MODEL), BF16),
            jax.ShapeDtypeStruct((B, N_EXPERTS, S), F32),
        ],
        compiler_params=_params("arbitrary", "arbitrary"),
        name="moe_pre",
    )(h3, gain, w_router_t)


def _topk_body(aff_ref, rank_ref, gate_ref, offs_ref, *, cap):
    x = aff_ref[...]
    E, S = x.shape
    xb = pltpu.bitcast(x, jnp.int32)
    count = lambda mask: jnp.sum(jnp.where(mask, 1.0, 0.0), axis=1, keepdims=True)

    def value_bit(it, t):
        cand = t | jnp.left_shift(jnp.int32(1), 30 - it)
        return jnp.where(count(xb >= cand) >= cap, cand, t)
    thr = lax.fori_loop(0, 31, value_bit, jnp.zeros((E, 1), jnp.int32))

    above = xb > thr
    tied = xb == thr
    need = cap - count(above)
    idx = lax.broadcasted_iota(jnp.int32, (E, S), 1)

    index_bits = (S - 1).bit_length()

    def index_bit(it, j):
        cand = j | jnp.left_shift(jnp.int32(1), index_bits - 1 - it)
        return jnp.where(count(tied & (idx < cand)) < need, cand, j)
    jmax = lax.fori_loop(0, index_bits, index_bit, jnp.zeros((E, 1), jnp.int32))
    sel = above | (tied & (idx <= jmax) & (need > 0.0))
    gate_ref[...] = jnp.where(sel, x, 0.0)

    ri = lax.broadcasted_iota(jnp.int32, (LANES, LANES), 0)
    ci = lax.broadcasted_iota(jnp.int32, (LANES, LANES), 1)
    before = jnp.where(ri < ci, 1.0, 0.0).astype(BF16)
    lane = lax.broadcasted_iota(jnp.int32, (E, LANES), 1)
    off = jnp.zeros((E, 1), F32)
    offs = jnp.zeros((E, LANES), F32)
    self_f = jnp.where(sel, 1.0, 0.0)
    for c in range(S // LANES):
        m_c = self_f[:, c * LANES:(c + 1) * LANES]
        rank_c = _dot(m_c.astype(BF16), before)
        rank_ref[:, c * LANES:(c + 1) * LANES] = jnp.where(m_c > 0.0, rank_c + off, -1.0)
        offs = jnp.where(lane == c, off, offs)
        off = off + jnp.sum(m_c, axis=1, keepdims=True)
    offs = jnp.where(lane == S // LANES, off, offs)
    offs_ref[...] = offs.astype(jnp.int32)


def _topk(aff_t, cap):
    B, E, S = aff_t.shape
    spec = pl.BlockSpec((None, E, S), lambda b: (b, 0, 0))
    return pl.pallas_call(
        functools.partial(_topk_body, cap=cap),
        grid=(B,),
        in_specs=[spec],
        out_specs=[spec, spec, pl.BlockSpec((None, E, LANES), lambda b: (b, 0, 0))],
        out_shape=[
            jax.ShapeDtypeStruct((B, E, S), F32),
            jax.ShapeDtypeStruct((B, E, S), F32),
            jax.ShapeDtypeStruct((B, E, LANES), jnp.int32),
        ],
        compiler_params=_params("arbitrary"),
        name="topk",
    )(aff_t)


def _window_start(lo, step, cap):
    return pl.multiple_of(jnp.minimum((lo // 16) * 16 + step * SLOT_WIN, cap - SLOT_WIN), 16)


def _slot_onehot(row_f, start, first_valid, rk):
    slot = row_f + start.astype(F32)
    return (slot == rk) & (slot >= first_valid.astype(F32))


def _gather_body(offs_ref, hn_ref, rk_ref, gt_ref, xs_ref, gs_ref, *, cap):
    S = hn_ref.shape[0]
    b = pl.program_id(0)
    dq = pl.program_id(1)
    acc_ref = xs_ref
    gacc_ref = gs_ref

    def clear(e, carry):
        acc_ref[e] = jnp.zeros(acc_ref.shape[1:], acc_ref.dtype)

        @pl.when(dq == 0)
        def _():
            gacc_ref[e] = jnp.zeros(gacc_ref.shape[1:], F32)
        return carry
    lax.fori_loop(0, N_EXPERTS, clear, 0)

    row_f = lax.broadcasted_iota(jnp.int32, (SLOT_WIN, LANES), 0).astype(F32)

    experts = range(N_EXPERTS)

    def chunks(it, carry):
        cs = [it * GS_CHUNKS + u for u in range(GS_CHUNKS)]
        h_c = [hn_ref[pl.ds(pl.multiple_of(c * LANES, LANES), LANES), :] for c in cs]
        los = [[offs_ref[(b * N_EXPERTS + e) * LANES + c] for e in experts] for c in cs]
        his = [[offs_ref[(b * N_EXPERTS + e) * LANES + c + 1] for e in experts] for c in cs]
        firsts = [[(lo // 16) * 16 for lo in row] for row in los]
        starts = [[_window_start(lo, 0, cap) for lo in row] for row in los]
        hots = [[_slot_onehot(row_f, starts[u][e], starts[u][e], rk_ref[e, pl.ds(c, 1), :]) for e in experts]
                for u, c in enumerate(cs)]
        stacked = [jnp.concatenate([jnp.where(h, 1.0, 0.0).astype(BF16) for h in row], axis=0) for row in hots]
        res = [_dot(st, h) for st, h in zip(stacked, h_c)]
        for u in range(GS_CHUNKS):
            for e in experts:
                win = pl.ds(starts[u][e], SLOT_WIN)
                acc_ref[e, win, :] = acc_ref[e, win, :] + res[u][e * SLOT_WIN:(e + 1) * SLOT_WIN, :].astype(
                    acc_ref.dtype)

        @pl.when(dq == 0)
        def _():
            for u, c in enumerate(cs):
                for e in experts:
                    win = pl.ds(starts[u][e], SLOT_WIN)
                    gacc_ref[e, win, :] = gacc_ref[e, win, :] + jnp.sum(
                        jnp.where(hots[u][e], gt_ref[e, pl.ds(c, 1), :], 0.0), axis=1, keepdims=True)

        overflow = functools.reduce(
            lambda x, y: x | y, [his[u][e] - firsts[u][e] > SLOT_WIN for u in range(GS_CHUNKS) for e in experts])

        @pl.when(overflow)
        def _():
            for u, c in enumerate(cs):
                for e in experts:
                    def extra(w, carry2, u=u, c=c, e=e):
                        start = _window_start(firsts[u][e], w, cap)
                        hot = _slot_onehot(row_f, start, firsts[u][e] + w * SLOT_WIN, rk_ref[e, pl.ds(c, 1), :])
                        win2 = pl.ds(start, SLOT_WIN)
                        acc_ref[e, win2, :] = acc_ref[e, win2, :] + _dot(
                            jnp.where(hot, 1.0, 0.0).astype(BF16), h_c[u]).astype(acc_ref.dtype)

                        @pl.when(dq == 0)
                        def _():
                            gacc_ref[e, win2, :] = gacc_ref[e, win2, :] + jnp.sum(
                                jnp.where(hot, gt_ref[e, pl.ds(c, 1), :], 0.0), axis=1, keepdims=True)
                        return carry2
                    lax.fori_loop(1, (his[u][e] - firsts[u][e] + SLOT_WIN - 1) // SLOT_WIN, extra, 0)
        return carry
    lax.fori_loop(0, S // (LANES * GS_CHUNKS), chunks, 0)


def _gather(offs_flat, hn3, rank4, gate4, cap, dcols=512):
    B, S, _ = hn3.shape
    n_chunks = S // LANES
    rows = pl.BlockSpec((None, N_EXPERTS, n_chunks, LANES), lambda b, q, offs: (b, 0, 0, 0))
    return pl.pallas_call(
        functools.partial(_gather_body, cap=cap),
        grid_spec=pltpu.PrefetchScalarGridSpec(
            num_scalar_prefetch=1,
            grid=(B, D_MODEL // dcols),
            in_specs=[
                pl.BlockSpec((None, S, dcols), lambda b, q, offs: (b, 0, q)),
                rows, rows,
            ],
            out_specs=[
                pl.BlockSpec((None, N_EXPERTS, cap, dcols), lambda b, q, offs: (b, 0, 0, q)),
                pl.BlockSpec((None, N_EXPERTS, cap, LANES), lambda b, q, offs: (b, 0, 0, 0)),
            ],
        ),
        out_shape=[
            jax.ShapeDtypeStruct((B, N_EXPERTS, cap, D_MODEL), BF16),
            jax.ShapeDtypeStruct((B, N_EXPERTS, cap, LANES), F32),
        ],
        compiler_params=_params("arbitrary", "arbitrary"),
        name="gather",
    )(offs_flat, hn3, rank4, gate4)


def _ffn_body(xs_ref, wg_ref, wu_ref, wd_ref, gs_ref, y_ref, acc_ref):
    f = pl.program_id(1)
    n_f = pl.num_programs(1)
    wg = wg_ref[...].astype(BF16)
    wu = wu_ref[...].astype(BF16)
    wd = wd_ref[...].astype(BF16)
    n_b, cap, _ = xs_ref.shape
    blocks = [(b, slice(r, r + FFN_ROWS)) for b in range(n_b) for r in range(0, cap, FFN_ROWS)]
    xs = [xs_ref[b, rows, :] for b, rows in blocks]
    hg = [_dot(x, wg) for x in xs]
    hu = [_dot(x, wu) for x in xs]
    hid = [(_silu(g) * u).astype(BF16) for g, u in zip(hg, hu)]
    for (b, rows), h in zip(blocks, hid):
        acc_ref[b, rows, :] = jnp.where(f > 0, acc_ref[b, rows, :], 0.0) + _dot(h, wd)

    @pl.when(f == n_f - 1)
    def _():
        for b, rows in blocks:
            y_ref[b, rows, :] = (acc_ref[b, rows, :] * gs_ref[b, rows, :][:, 0:1]).astype(y_ref.dtype)


def _ffn(xs, w_gate, w_up, w_down, gslot, layer, tf=512):
    B, E, cap, _ = xs.shape
    return pl.pallas_call(
        _ffn_body,
        grid=(E, EXPERT_FF // tf),
        in_specs=[
            pl.BlockSpec((B, None, cap, D_MODEL), lambda e, f: (0, e, 0, 0)),
            pl.BlockSpec((None, None, D_MODEL, tf), lambda e, f: (layer, e, 0, f)),
            pl.BlockSpec((None, None, D_MODEL, tf), lambda e, f: (layer, e, 0, f)),
            pl.BlockSpec((None, None, tf, D_MODEL), lambda e, f: (layer, e, f, 0)),
            pl.BlockSpec((B, None, cap, LANES), lambda e, f: (0, e, 0, 0)),
        ],
        out_specs=pl.BlockSpec((B, None, cap, D_MODEL), lambda e, f: (0, e, 0, 0)),
        out_shape=jax.ShapeDtypeStruct((B, E, cap, D_MODEL), BF16),
        scratch_shapes=[pltpu.VMEM((B, cap, D_MODEL), F32)],
        compiler_params=_params("arbitrary", "arbitrary"),
        name="ffn",
    )(xs, w_gate, w_up, w_down, gslot)


def _scatter_body(offs_ref, h_ref, y_ref, rk_ref, o_ref, ycat_ref, *, cap):
    tp = h_ref.shape[0]
    b = pl.program_id(0)
    t = pl.program_id(2)
    row_f = lax.broadcasted_iota(jnp.int32, (SLOT_WIN, LANES), 0).astype(F32)

    experts = range(N_EXPERTS)
    n_cc = tp // LANES
    cs = [t * n_cc + cc for cc in range(n_cc)]
    rows = [slice(cc * LANES, (cc + 1) * LANES) for cc in range(n_cc)]
    los = [[offs_ref[(b * N_EXPERTS + e) * LANES + c] for e in experts] for c in cs]
    his = [[offs_ref[(b * N_EXPERTS + e) * LANES + c + 1] for e in experts] for c in cs]
    firsts = [[(lo // 16) * 16 for lo in row] for row in los]
    starts = [[_window_start(lo, 0, cap) for lo in row] for row in los]
    hots = [[_slot_onehot(row_f, starts[cc][e], starts[cc][e], rk_ref[e, pl.ds(c, 1), :]) for e in experts]
            for cc, c in enumerate(cs)]
    for cc in range(n_cc):
        for e in experts:
            ycat_ref[cc, e * SLOT_WIN:(e + 1) * SLOT_WIN, :] = y_ref[e, pl.ds(starts[cc][e], SLOT_WIN), :]
    stacked = [jnp.concatenate([jnp.where(h, 1.0, 0.0).astype(BF16) for h in row], axis=0) for row in hots]
    for cc in range(n_cc):
        o_ref[rows[cc], :] = h_ref[rows[cc], :] + _tn(stacked[cc], ycat_ref[cc])

    overflow = functools.reduce(
        lambda x, y: x | y, [his[cc][e] - firsts[cc][e] > SLOT_WIN for cc in range(n_cc) for e in experts])

    @pl.when(overflow)
    def _():
        for cc, c in enumerate(cs):
            for e in experts:
                def extra(w, carry, cc=cc, c=c, e=e):
                    start = _window_start(firsts[cc][e], w, cap)
                    hot = _slot_onehot(row_f, start, firsts[cc][e] + w * SLOT_WIN, rk_ref[e, pl.ds(c, 1), :])
                    o_ref[rows[cc], :] = o_ref[rows[cc], :] + _tn(
                        jnp.where(hot, 1.0, 0.0).astype(BF16), y_ref[e, pl.ds(start, SLOT_WIN), :])
                    return carry
                lax.fori_loop(1, (his[cc][e] - firsts[cc][e] + SLOT_WIN - 1) // SLOT_WIN, extra, 0)


def _scatter(offs_flat, h3, y, rank4, cap, tp=1024, dcols=512):
    B, S, _ = h3.shape
    n_chunks = S // LANES
    return pl.pallas_call(
        functools.partial(_scatter_body, cap=cap),
        grid_spec=pltpu.PrefetchScalarGridSpec(
            num_scalar_prefetch=1,
            grid=(B, D_MODEL // dcols, S // tp),
            in_specs=[
                pl.BlockSpec((None, tp, dcols), lambda b, q, t, offs: (b, t, q)),
                pl.BlockSpec((None, N_EXPERTS, cap, dcols), lambda b, q, t, offs: (b, 0, 0, q)),
                pl.BlockSpec((None, N_EXPERTS, n_chunks, LANES), lambda b, q, t, offs: (b, 0, 0, 0)),
            ],
            out_specs=pl.BlockSpec((None, tp, dcols), lambda b, q, t, offs: (b, t, q)),
            scratch_shapes=[pltpu.VMEM((tp // LANES, N_EXPERTS * SLOT_WIN, dcols), BF16)],
        ),
        out_shape=jax.ShapeDtypeStruct((B, S, D_MODEL), F32),
        compiler_params=_params("arbitrary", "arbitrary", "arbitrary"),
        name="scatter",
    )(offs_flat, h3, y, rank4)


def _ple_body(h_hbm, g_ref, wg_ref, p_ref, wp_ref, hblk_ref, o_ref, hn_ref, hbuf, sem):
    @pl.when(pl.program_id(1) == 0)
    def _():
        _norm_row_block(h_hbm, hbuf, sem, g_ref, hn_ref)

    gate = _sigmoid(_dot(hn_ref[...], wg_ref[...].astype(BF16)))
    emb = _dot(p_ref[...].astype(BF16), wp_ref[...].astype(BF16))
    o_ref[...] = hblk_ref[...] + gate * emb


def _ple(h2d, gain, w_gate, p2d, w_ple, layer, tm=2048, tn=256):
    m = h2d.shape[0]
    return pl.pallas_call(
        _ple_body,
        grid=(m // tm, D_MODEL // tn),
        in_specs=[
            pl.BlockSpec(memory_space=pl.ANY),
            pl.BlockSpec((1, D_MODEL), lambda i, j: (0, 0)),
            pl.BlockSpec((None, D_MODEL, tn), lambda i, j: (layer, 0, j)),
            pl.BlockSpec((None, tm, PLE_DIM), lambda i, j: (layer, i, 0), **ROW_RESIDENT),
            pl.BlockSpec((None, PLE_DIM, tn), lambda i, j: (layer, 0, j)),
            pl.BlockSpec((tm, tn), lambda i, j: (i, j)),
        ],
        out_specs=pl.BlockSpec((tm, tn), lambda i, j: (i, j)),
        out_shape=jax.ShapeDtypeStruct((m, D_MODEL), F32),
        scratch_shapes=[pltpu.VMEM((tm, D_MODEL), BF16), pltpu.VMEM((tm, D_MODEL), F32),
                        pltpu.SemaphoreType.DMA(())],
        compiler_params=_params("arbitrary", "arbitrary"),
        name="ple",
    )(h2d, gain, w_gate, p2d, w_ple, h2d)


def _final_body(h_ref, g_ref, o_ref):
    o_ref[...] = _rms(h_ref[...], g_ref[...])


def _final_norm(h2d, gain, tm=1024):
    m = h2d.shape[0]
    return pl.pallas_call(
        _final_body,
        grid=(m // tm,),
        in_specs=[pl.BlockSpec((tm, D_MODEL), lambda i: (i, 0)), pl.BlockSpec((1, D_MODEL), lambda i: (0, 0))],
        out_specs=pl.BlockSpec((tm, D_MODEL), lambda i: (i, 0)),
        out_shape=jax.ShapeDtypeStruct((m, D_MODEL), F32),
        compiler_params=_params("arbitrary"),
        name="final_norm",
    )(h2d, gain)


def _rope_cs(pos, dim, theta):
    inv = theta ** (-jnp.arange(0, dim, 2, dtype=F32) / dim)
    ang = pos.astype(F32)[:, None] * inv[None, :]
    return jnp.cos(ang), jnp.sin(ang)


def _tables(S):
    cos, sin = _rope_cs(jnp.arange(S), ROPE_DIM, ROPE_THETA)
    z = jnp.zeros_like(sin)
    rest = HEAD_DIM - ROPE_DIM
    a_c = jnp.concatenate([cos, cos, jnp.ones((S, rest), F32)], axis=1)
    a_sp = jnp.concatenate([z, sin, jnp.zeros((S, rest), F32)], axis=1)
    a_sm = jnp.concatenate([-sin, z, jnp.zeros((S, rest), F32)], axis=1)
    rows = S // GRID_W
    rc, rs = _rope_cs(jnp.repeat(jnp.arange(rows), GRID_W), HEAD_DIM // 2, AXIAL_THETA)
    cc, cs = _rope_cs(jnp.tile(jnp.arange(GRID_W), rows), HEAD_DIM // 2, AXIAL_THETA)
    zz = jnp.zeros_like(rs)
    b_c = jnp.concatenate([rc, rc, cc, cc], axis=1)
    b_sp = jnp.concatenate([zz, rs, zz, cs], axis=1)
    b_sm = jnp.concatenate([-rs, zz, -cs, zz], axis=1)
    return (a_c, a_sp, a_sm), (b_c, b_sp, b_sm)


def kernel(x, p, norm_mix, w_in, conv_w, q_norm, k_norm, out_norm_a, out_norm_b, gdn_a_log, gdn_dt_bias,
           gdn_norm, w_out, norm_moe, w_router, w_gate, w_up, w_down, norm_ple, w_ple, w_ple_gate, norm_final):
    B, S, D = x.shape
    depth = w_in.shape[0]
    assert D == D_MODEL and S % 1024 == 0
    cap = EC_CAPACITY * S // N_EXPERTS
    width = GROUP * DELTA_CHUNK
    tabs_a, tabs_b = _tables(S)
    p2d = p.reshape(depth, B * S, PLE_DIM)
    row = lambda v: v.reshape(1, -1)

    w_in_t = jnp.swapaxes(w_in, 1, 2)
    h = x.reshape(B * S, D)
    for i in range(depth):
        proj, small = _proj(h, row(norm_mix[i]), w_in_t, i)
        proj3 = proj.reshape(B, S, D_MAIN)

        y_a = _mix_a(proj3, tabs_a)
        y_b = _mix_b(proj3, tabs_b, row(q_norm[i]), row(k_norm[i]))

        lg = small[:, :D_SMALL].reshape(B, S // width, width, 4, C_HEADS).transpose(0, 3, 4, 1, 2)
        par = jnp.concatenate([gdn_a_log[i], gdn_dt_bias[i]], axis=0)
        par = jnp.broadcast_to(par.T[:, :, None], (C_HEADS, 4, width))
        y_c = _mix_c(proj3, conv_w, lg, par, row(gdn_norm[i]), i)

        h = _out_proj(y_a.reshape(B * S, A_W), y_b.reshape(B * S, B_QW), y_c.reshape(B * S, C_W),
                      row(out_norm_a[i]), row(out_norm_b[i]), w_out, h, i)

        hn, aff_t = _moe_pre(h.reshape(B, S, D), row(norm_moe[i]), w_router[i].T)
        rank, gates, offs = _topk(aff_t, cap)
        offs_flat = offs.reshape(-1)
        rank4 = rank.reshape(B, N_EXPERTS, S // LANES, LANES)
        gate4 = gates.reshape(B, N_EXPERTS, S // LANES, LANES)
        xs, gslot = _gather(offs_flat, hn, rank4, gate4, cap)
        y = _ffn(xs, w_gate, w_up, w_down, gslot, i)
        h = _scatter(offs_flat, h.reshape(B, S, D), y, rank4, cap).reshape(B * S, D)

        h = _ple(h, row(norm_ple[i]), w_ple_gate, p2d, w_ple, i)
    return _final_norm(h, row(norm_final)).reshape(B, S, D)
```
